```python
import math
import jax
import jax.numpy as jnp
from jax import lax
import numpy as np

D_MODEL = 2048
BATCH = 8
SEQ = 2048
DEPTH = 4

GRID_W = 64
CTX_LEN = 256
EPS = 1e-6
ROPE_BASE = 10000.0

RET_HEADS = 4
RET_DK = 128
RET_DV = 128
NA_HEADS = 4
NA_DH = 128
NA_WIN_R = 8
NA_WIN_C = 16
NA_QB_C = 16
NA_KB_C = 32
FNET_GROUPS = 4
FNET_GW = 128
GLA_HEADS = 4
GLA_DK = 64
GLA_DV = 128
GLA_RANK = 16
GLA_TAU = 16.0

SCAN_CHUNK = 64
N_BRANCH = 4
BRANCH_W = 512

D_FF = 5632
N_EXPERTS = 8
TOP_K = 2
D_EXPERT = 4096

kernel_name = 'hybrid_latent_dit_block'


def _in_widths():
    return [RET_HEADS * RET_DK, RET_HEADS * RET_DK, RET_HEADS * RET_DV, RET_HEADS * RET_DV,
            NA_HEADS * NA_DH, NA_HEADS * NA_DH, NA_HEADS * NA_DH,
            FNET_GROUPS * FNET_GW,
            GLA_HEADS * GLA_DK, GLA_HEADS * GLA_DK, GLA_HEADS * GLA_DV, GLA_HEADS * GLA_DV,
            2 * GLA_RANK,
            N_BRANCH * D_MODEL]


def rmsnorm(x, g):
    xf = x.astype(jnp.float32)
    y = xf * lax.rsqrt(jnp.mean(xf * xf, axis=-1, keepdims=True) + EPS)
    return (y * g.astype(jnp.float32)).astype(x.dtype)


def head_rms(o):
    return o * lax.rsqrt(jnp.mean(o * o, axis=-1, keepdims=True) + EPS)


def modulate(h, shift, scale):
    return h * (1.0 + scale) + shift


def axial_rope(x, pos_r, pos_c):
    half = x.shape[-1] // 2
    quarter = half // 2
    inv_freq = ROPE_BASE ** (-jnp.arange(quarter, dtype=jnp.float32) / quarter)

    def rotate(xa, pos):
        ang = pos.astype(jnp.float32)[:, None] * inv_freq[None]
        cos = jnp.cos(ang)[None, :, None, :]
        sin = jnp.sin(ang)[None, :, None, :]
        x1 = xa[..., :quarter].astype(jnp.float32)
        x2 = xa[..., quarter:].astype(jnp.float32)
        return jnp.concatenate([x1 * cos - x2 * sin, x2 * cos + x1 * sin], axis=-1)

    out = jnp.concatenate([rotate(x[..., :half], pos_r), rotate(x[..., half:], pos_c)], axis=-1)
    return out.astype(x.dtype)


def chunked_gated_scan(q, k, v, log_a, s0):
    B, L, H, K = q.shape
    V = v.shape[-1]
    n = L // SCAN_CHUNK

    def to_chunks(t):
        t = t.astype(jnp.float32).reshape(B, n, SCAN_CHUNK, H, t.shape[-1])
        return jnp.moveaxis(t, 1, 0)

    lower = jnp.tril(jnp.ones((SCAN_CHUNK, SCAN_CHUNK), dtype=bool))

    def step(s, inp):
        qi, ki, vi, ai = inp
        b = jnp.cumsum(ai, axis=1)
        b_last = b[:, -1]
        q_dec = qi * jnp.exp(b)
        k_inv = ki * jnp.exp(-b)
        k_end = ki * jnp.exp(b_last[:, None] - b)
        scores = jnp.einsum('bthk,bshk->bhts', q_dec, k_inv)
        scores = jnp.where(lower, scores, 0.0)
        o = (jnp.einsum('bhts,bshv->bthv', scores, vi)
             + jnp.einsum('bthk,bhkv->bthv', q_dec, s))
        s_new = jnp.exp(b_last)[..., None] * s + jnp.einsum('bshk,bshv->bhkv', k_end, vi)
        return s_new, o

    s_fin, o = lax.scan(step, s0, (to_chunks(q), to_chunks(k), to_chunks(v), to_chunks(log_a)))
    return jnp.moveaxis(o, 0, 1).reshape(B, L, H, V), s_fin


def bidirectional_scan(q_c, k_c, v_c, q_l, k_l, v_l, a_c_f, a_c_b, a_l_f, a_l_b):
    B, _, H, K = q_c.shape
    s0 = jnp.zeros((B, H, K, v_c.shape[-1]), jnp.float32)
    o_c_f, s_c_f = chunked_gated_scan(q_c, k_c, v_c, a_c_f, s0)
    o_l_f, _ = chunked_gated_scan(q_l, k_l, v_l, a_l_f, s_c_f)
    rev = lambda t: jnp.flip(t, axis=1)
    o_c_b, s_c_b = chunked_gated_scan(rev(q_c), rev(k_c), rev(v_c), rev(a_c_b), s0)
    o_l_b, _ = chunked_gated_scan(rev(q_l), rev(k_l), rev(v_l), rev(a_l_b), s_c_b)
    return o_c_f + rev(o_c_b), o_l_f + rev(o_l_b)


def dense_attention(q, k, v):
    s = jnp.einsum('bqhd,bkhd->bhqk', q, k).astype(jnp.float32) * (q.shape[-1] ** -0.5)
    p = jax.nn.softmax(s, axis=-1).astype(v.dtype)
    return jnp.einsum('bhqk,bkhd->bqhd', p, v)


def neighbourhood_attention(q, k, v, k_ctx, v_ctx, rpb):
    B, S, H, Dh = q.shape
    rows = S // GRID_W
    kr = min(NA_WIN_R, rows)
    ncb = GRID_W // NA_QB_C
    r = np.arange(rows)
    key_rows = np.clip(r - kr // 2, 0, rows - kr)[:, None] + np.arange(kr)
    blk = np.arange(ncb)
    key_cols = (np.clip(blk * NA_QB_C - NA_WIN_C // 2, 0, GRID_W - NA_KB_C)[:, None]
                + np.arange(NA_KB_C))
    q_cols = blk[:, None] * NA_QB_C + np.arange(NA_QB_C)
    win_start = np.clip(q_cols - NA_WIN_C // 2, 0, GRID_W - NA_WIN_C)
    kc3 = key_cols[:, None, :]
    in_win = (kc3 >= win_start[:, :, None]) & (kc3 < win_start[:, :, None] + NA_WIN_C)
    rel_r = key_rows - r[:, None] + NA_WIN_R - 1
    rel_c = np.clip(kc3 - q_cols[:, :, None], 1 - NA_WIN_C, NA_WIN_C - 1) + NA_WIN_C - 1
    bias = rpb[:, rel_r[:, None, None, :, None], rel_c[None, :, :, None, :]]

    qg = q.reshape(B, rows, ncb, NA_QB_C, H, Dh)
    kg = k.reshape(B, rows, GRID_W, H, Dh)
    vg = v.reshape(B, rows, GRID_W, H, Dh)
    gi_r = key_rows[:, None, :, None]
    gi_c = key_cols[None, :, None, :]
    kb = kg[:, gi_r, gi_c]
    vb = vg[:, gi_r, gi_c]
    scale = Dh ** -0.5
    s_win = jnp.einsum('brjqhd,brjikhd->bhrjqik', qg, kb).astype(jnp.float32) * scale + bias.astype(jnp.float32)
    s_win = jnp.where(in_win[:, :, None, :], s_win, -jnp.inf)
    s_ctx = jnp.einsum('brjqhd,bchd->bhrjqc', qg, k_ctx).astype(jnp.float32) * scale
    n_win = kr * NA_KB_C
    s_all = jnp.concatenate([s_win.reshape(s_win.shape[:5] + (n_win,)), s_ctx], axis=-1)
    p = jax.nn.softmax(s_all, axis=-1).astype(v.dtype)
    p_win = p[..., :n_win].reshape(s_win.shape)
    p_ctx = p[..., n_win:]
    o = (jnp.einsum('bhrjqik,brjikhd->brjqhd', p_win, vb)
         + jnp.einsum('bhrjqc,bchd->brjqhd', p_ctx, v_ctx))
    return o.reshape(B, S, H, Dh)


def fourier_mix(u):
    B, L, _ = u.shape
    ug = u.astype(jnp.float32).reshape(B, L, FNET_GROUPS, FNET_GW)
    f = jnp.fft.fft2(ug, axes=(1, 3), norm='ortho').real
    return f.reshape(B, L, FNET_GROUPS * FNET_GW).astype(u.dtype)


def hybrid_mixer(hc, hl, pos_r, pos_c, w_in, ret_decay, gla_w2, gla_b2, na_rpb, w_up, w_out):
    B, n_ctx, D = hc.shape
    h = jnp.concatenate([hc, hl], axis=1)
    L = h.shape[1]
    S = L - n_ctx
    pts = [int(p) for p in np.cumsum(_in_widths())[:-1]]
    (r_q, r_k, r_v, r_g, n_q, n_k, n_v, f_u,
     g_q, g_k, g_v, g_g, g_lr, gate_pre) = jnp.split(h @ w_in, pts, axis=-1)
    heads = lambda t, n: t.reshape(B, L, n, -1)
    cut = lambda t: (t[:, :n_ctx], t[:, n_ctx:])

    rq_c, rq_l = cut(heads(r_q, RET_HEADS) * (RET_DK ** -0.5))
    rk_c, rk_l = cut(heads(r_k, RET_HEADS))
    rv_c, rv_l = cut(heads(r_v, RET_HEADS))
    rq_l = axial_rope(rq_l, pos_r, pos_c)
    rk_l = axial_rope(rk_l, pos_r, pos_c)
    log_gamma = jnp.log1p(-jnp.exp(ret_decay.astype(jnp.float32)))
    const_dec = lambda n, d: jnp.broadcast_to(d[None, None, :, None], (B, n, RET_HEADS, RET_DK))
    ro_c, ro_l = bidirectional_scan(rq_c, rk_c, rv_c, rq_l, rk_l, rv_l,
                                    const_dec(n_ctx, log_gamma[0]), const_dec(n_ctx, log_gamma[1]),
                                    const_dec(S, log_gamma[0]), const_dec(S, log_gamma[1]))
    y_ret = head_rms(jnp.concatenate([ro_c, ro_l], axis=1)) * jax.nn.silu(heads(r_g, RET_HEADS).astype(jnp.float32))
    y_ret = y_ret.reshape(B, L, BRANCH_W).astype(h.dtype)

    nq_c, nq_l = cut(heads(n_q, NA_HEADS))
    nk_c, nk_l = cut(heads(n_k, NA_HEADS))
    nv_c, nv_l = cut(heads(n_v, NA_HEADS))
    na_c = dense_attention(nq_c, nk_c, nv_c)
    na_l = neighbourhood_attention(nq_l, nk_l, nv_l, nk_c, nv_c, na_rpb)
    y_na = jnp.concatenate([na_c, na_l], axis=1).reshape(B, L, BRANCH_W)

    fu_c, fu_l = cut(f_u)
    y_fn = jnp.concatenate([fourier_mix(fu_c), fourier_mix(fu_l)], axis=1)

    gq_c, gq_l = cut(heads(g_q, GLA_HEADS) * (GLA_DK ** -0.5))
    gk_c, gk_l = cut(heads(g_k, GLA_HEADS))
    gv_c, gv_l = cut(heads(g_v, GLA_HEADS))

    def gla_log_decay(lr, d):
        z = lr.astype(jnp.float32) @ gla_w2[d].astype(jnp.float32) + gla_b2[d].astype(jnp.float32)
        return (jax.nn.log_sigmoid(z) / GLA_TAU).reshape(B, L, GLA_HEADS, GLA_DK)

    la_f_c, la_f_l = cut(gla_log_decay(g_lr[..., :GLA_RANK], 0))
    la_b_c, la_b_l = cut(gla_log_decay(g_lr[..., GLA_RANK:], 1))
    go_c, go_l = bidirectional_scan(gq_c, gk_c, gv_c, gq_l, gk_l, gv_l, la_f_c, la_b_c, la_f_l, la_b_l)
    y_gla = head_rms(jnp.concatenate([go_c, go_l], axis=1)) * jax.nn.silu(heads(g_g, GLA_HEADS).astype(jnp.float32))
    y_gla = y_gla.reshape(B, L, BRANCH_W).astype(h.dtype)

    merged = jnp.zeros_like(h)
    for i, y in enumerate((y_ret, y_na, y_fn, y_gla)):
        gate = jax.nn.sigmoid(gate_pre[..., i * D:(i + 1) * D])
        merged = merged + gate * (y @ w_up[i])
    out = merged @ w_out
    return out[:, :n_ctx], out[:, n_ctx:]


def swiglu(h, w1, w3, w2):
    return (jax.nn.silu(h @ w1) * (h @ w3)) @ w2


def moe_swiglu(h, w_r, b_r, w1, w3, w2):
    logits = (h @ w_r).astype(jnp.float32) + b_r.astype(jnp.float32)
    top_v, top_i = lax.top_k(logits, TOP_K)
    top_w = jax.nn.softmax(top_v, axis=-1)
    comb = jnp.sum(jax.nn.one_hot(top_i, N_EXPERTS, dtype=jnp.float32) * top_w[..., None], axis=-2)
    comb = comb.astype(h.dtype)
    out = jnp.zeros_like(h)
    for e in range(N_EXPERTS):
        out = out + comb[..., e:e + 1] * swiglu(h, w1[e], w3[e], w2[e])
    return out


def channel_mixer(l, h, ffn_w1, ffn_w3, ffn_w2, moe_router, moe_router_b, moe_w1, moe_w3, moe_w2):
    i = l // 2
    if l % 2 == 0:
        return swiglu(h, ffn_w1[i], ffn_w3[i], ffn_w2[i])
    return moe_swiglu(h, moe_router[i], moe_router_b[i], moe_w1[i], moe_w3[i], moe_w2[i])


def setup_inputs(seed: int = 0) -> dict:
    key = jax.random.key(seed)
    ks = jax.random.split(key, 24)
    f32 = jnp.float32

    def nrm(k, shape, scale):
        return jax.random.normal(k, shape, f32) * scale

    D = D_MODEL
    n_dense = (DEPTH + 1) // 2
    n_moe = DEPTH // 2
    d_in = sum(_in_widths())
    multi_scale = -(5.0 + jnp.arange(RET_HEADS, dtype=f32)) * math.log(2.0)
    return {
        'x': nrm(ks[0], (BATCH, SEQ, D), 1.0),
        'c': nrm(ks[1], (BATCH, D), 1.0),
        'ctx': nrm(ks[2], (BATCH, CTX_LEN, D), 1.0),
        'c_ctx': nrm(ks[3], (D,), 1.0),
        'w_ada': nrm(ks[4], (DEPTH, D, 6 * D), 0.5 * D ** -0.5),
        'b_ada': nrm(ks[5], (DEPTH, 6 * D), 0.02),
        'norms': 1.0 + nrm(ks[6], (DEPTH, 4, D), 0.02),
        'w_in': nrm(ks[7], (DEPTH, D, d_in), D ** -0.5),
        'ret_decay': multi_scale + nrm(ks[8], (DEPTH, 2, RET_HEADS), 0.05),
        'gla_w2': nrm(ks[9], (DEPTH, 2, GLA_RANK, GLA_HEADS * GLA_DK), GLA_RANK ** -0.5),
        'gla_b2': nrm(ks[10], (DEPTH, 2, GLA_HEADS * GLA_DK), 0.1),
        'na_rpb': nrm(ks[11], (DEPTH, NA_HEADS, 2 * NA_WIN_R - 1, 2 * NA_WIN_C - 1), 0.1),
        'w_up': nrm(ks[12], (DEPTH, N_BRANCH, BRANCH_W, D), BRANCH_W ** -0.5),
        'w_out': nrm(ks[13], (DEPTH, D, D), D ** -0.5),
        'ffn_w1': nrm(ks[14], (n_dense, D, D_FF), D ** -0.5),
        'ffn_w3': nrm(ks[15], (n_dense, D, D_FF), D ** -0.5),
        'ffn_w2': nrm(ks[16], (n_dense, D_FF, D), D_FF ** -0.5),
        'moe_router': nrm(ks[17], (n_moe, D, N_EXPERTS), D ** -0.5),
        'moe_router_b': nrm(ks[18], (n_moe, N_EXPERTS), 0.01),
        'moe_w1': nrm(ks[19], (n_moe, N_EXPERTS, D, D_EXPERT), D ** -0.5),
        'moe_w3': nrm(ks[20], (n_moe, N_EXPERTS, D, D_EXPERT), D ** -0.5),
        'moe_w2': nrm(ks[21], (n_moe, N_EXPERTS, D_EXPERT, D), D_EXPERT ** -0.5),
    }


def reference(x, c, ctx, c_ctx, w_ada, b_ada, norms, w_in, ret_decay, gla_w2, gla_b2, na_rpb,
              w_up, w_out, ffn_w1, ffn_w3, ffn_w2, moe_router, moe_router_b, moe_w1, moe_w3, moe_w2):
    S = x.shape[1]
    t = jnp.arange(S, dtype=jnp.int32)
    pos_r = t // GRID_W
    pos_c = t % GRID_W
    silu_c = jax.nn.silu(c)
    silu_cc = jax.nn.silu(c_ctx)[None]
    xl, xc = x, ctx
    for l in range(DEPTH):
        last = l == DEPTH - 1
        mod_l = jnp.split((silu_c @ w_ada[l] + b_ada[l])[:, None, :], 6, axis=-1)
        mod_c = jnp.split((silu_cc @ w_ada[l] + b_ada[l])[:, None, :], 6, axis=-1)

        hl = modulate(rmsnorm(xl, norms[l, 0]), mod_l[0], mod_l[1])
        hc = modulate(rmsnorm(xc, norms[l, 0]), mod_c[0], mod_c[1])
        oc, ol = hybrid_mixer(hc, hl, pos_r, pos_c, w_in[l], ret_decay[l], gla_w2[l], gla_b2[l],
                              na_rpb[l], w_up[l], w_out[l])
        xl = xl + mod_l[2] * rmsnorm(ol, norms[l, 1])
        hl = modulate(rmsnorm(xl, norms[l, 2]), mod_l[3], mod_l[4])
        if last:
            fl = channel_mixer(l, hl, ffn_w1, ffn_w3, ffn_w2, moe_router, moe_router_b, moe_w1, moe_w3, moe_w2)
        else:
            xc = xc + mod_c[2] * rmsnorm(oc, norms[l, 1])
            hc = modulate(rmsnorm(xc, norms[l, 2]), mod_c[3], mod_c[4])
            n_ctx = hc.shape[1]
            f_all = channel_mixer(l, jnp.concatenate([hc, hl], axis=1), ffn_w1, ffn_w3, ffn_w2,
                                  moe_router, moe_router_b, moe_w1, moe_w3, moe_w2)
            xc = xc + mod_c[5] * rmsnorm(f_all[:, :n_ctx], norms[l, 3])
            fl = f_all[:, n_ctx:]
        xl = xl + mod_l[5] * rmsnorm(fl, norms[l, 3])
    return xl
```

```python
import functools
import math

import numpy as np
import jax
import jax.numpy as jnp
from jax import lax
from jax.experimental import pallas as pl
from jax.experimental.pallas import tpu as pltpu

F32 = jnp.float32
BF16 = jnp.bfloat16

D_MODEL = 2048
DEPTH = 4
GRID_W = 64
CTX_LEN = 256
EPS = 1e-6
ROPE_BASE = 10000.0
RET_HEADS, RET_DK = 4, 128
NA_HEADS, NA_DH = 4, 128
NA_WIN_R, NA_WIN_C = 8, 16
FNET_GROUPS, FNET_GW = 4, 128
GLA_HEADS, GLA_DK, GLA_DV, GLA_RANK, GLA_TAU = 4, 64, 128, 16, 16.0
GLA_CHUNK = 64
BRANCH_W = 512
N_EXPERTS = 8

LANE = 128
ROW_BLK = 256
VMEM_LIMIT = 56 * 1024 * 1024

C_RET_Q, C_RET_K, C_RET_V, C_RET_G = 0, 4, 8, 12
C_NA_Q, C_NA_K, C_NA_V = 16, 20, 24
C_FU = 28
C_GLA_Q, C_GLA_K, C_GLA_V, C_GLA_G, C_GLA_LR = 32, 34, 36, 40, 44
C_GATE = 48
N_PROJ = (C_GATE + 4 * D_MODEL // LANE) * LANE
W_IN_SPLIT = 5632

NEG = -1e30


def _cparams(sem):
    return pltpu.CompilerParams(dimension_semantics=sem, vmem_limit_bytes=VMEM_LIMIT)


def _silu(x):
    return x * (1.0 / (1.0 + jnp.exp(-x)))


def _sigmoid(x):
    return 1.0 / (1.0 + jnp.exp(-x))


def _rms(x):
    return x * lax.rsqrt(jnp.mean(x * x, axis=-1, keepdims=True) + EPS)


def _ada_kernel(c_ref, w_ref, b_ref, o_ref):
    s = _silu(c_ref[...]).astype(BF16)
    o_ref[...] = jnp.dot(s, w_ref[...].astype(BF16), preferred_element_type=F32) + b_ref[...]


def _adaln(cvec, w_ada, b_ada):
    depth, d, n = w_ada.shape
    tn = 1024
    return pl.pallas_call(
        _ada_kernel,
        grid=(depth, n // tn),
        in_specs=[pl.BlockSpec((16, d), lambda l, j: (0, 0)),
                  pl.BlockSpec((None, d, tn), lambda l, j: (l, 0, j)),
                  pl.BlockSpec((None, 1, tn), lambda l, j: (l, 0, j))],
        out_specs=pl.BlockSpec((None, 16, tn), lambda l, j: (l, 0, j)),
        out_shape=jax.ShapeDtypeStruct((depth, 16, n), F32),
        compiler_params=_cparams(("arbitrary", "arbitrary")),
        name="adaln",
    )(cvec, w_ada, b_ada.reshape(depth, 1, n))


def _resid_kernel(*refs, nb, ctx_first, has_f, has_next, has_router, ia, gate_k, ib, sh_k, sc_k):
    it = iter(refs)
    x_ref = next(it)
    f_ref = next(it) if has_f else None
    moda_ref = next(it) if has_f else None
    na_ref = next(it) if has_f else None
    modb_ref = next(it) if has_next else None
    nbn_ref = next(it) if has_next else None
    wr_ref = next(it) if has_router else None
    br_ref = next(it) if has_router else None
    xo_ref = next(it) if has_f else None
    h_ref = next(it) if has_next else None
    rt_ref = next(it) if has_router else None

    b = pl.program_id(0)
    j = pl.program_id(1)
    row = jnp.where(j == 0, nb, b) if ctx_first else b
    d = x_ref.shape[-1]

    def modv(ref, k):
        return ref[pl.ds(row, 1), k * d:(k + 1) * d]

    x = x_ref[...]
    if has_f:
        f = f_ref[...].astype(F32)
        x = x + modv(moda_ref, gate_k) * (_rms(f) * na_ref[ia:ia + 1, :])
        xo_ref[...] = x
    if has_next:
        h = _rms(x) * nbn_ref[ib:ib + 1, :]
        h = h * (1.0 + modv(modb_ref, sc_k)) + modv(modb_ref, sh_k)
        h_ref[...] = h.astype(h_ref.dtype)
        if has_router:
            logits = jnp.dot(h, wr_ref[...], preferred_element_type=F32,
                             precision=lax.Precision.HIGHEST) + br_ref[...]
            lane = lax.broadcasted_iota(jnp.int32, logits.shape, 1)
            lg = jnp.where(lane < N_EXPERTS, logits, -jnp.inf)
            v1 = jnp.max(lg, axis=-1, keepdims=True)
            i1 = jnp.min(jnp.where(lg == v1, lane, LANE), axis=-1, keepdims=True)
            lg2 = jnp.where(lane == i1, -jnp.inf, lg)
            v2 = jnp.max(lg2, axis=-1, keepdims=True)
            i2 = jnp.min(jnp.where(lg2 == v2, lane, LANE), axis=-1, keepdims=True)
            e2 = jnp.exp(v2 - v1)
            w1 = 1.0 / (1.0 + e2)
            w2 = e2 / (1.0 + e2)
            zero = jnp.zeros_like(logits)
            rt = (jnp.where(lane == 8, i1.astype(F32), zero) + jnp.where(lane == 9, i2.astype(F32), zero)
                  + jnp.where(lane == 10, w1, zero) + jnp.where(lane == 11, w2, zero))
            rt_ref[...] = rt


def _resid(x, f, mod_a, norms_a, mod_b, norms_b, router, *, x_off, f_off, nblk, ctx_first,
           ia=0, gate_k=0, ib=0, sh_k=0, sc_k=0, h_dtype=BF16):
    nb, _, d = x.shape
    has_f = f is not None
    has_next = mod_b is not None
    has_router = router is not None
    rows = nblk * ROW_BLK
    blk = lambda off: pl.BlockSpec((None, ROW_BLK, d), lambda b, j: (b, j + off, 0))
    full = lambda a: pl.BlockSpec(a.shape, lambda b, j: (0,) * a.ndim)
    ins, specs = [x], [blk(x_off)]
    if has_f:
        ins += [f, mod_a, norms_a]
        specs += [blk(f_off), full(mod_a), full(norms_a)]
    if has_next:
        ins += [mod_b, norms_b]
        specs += [full(mod_b), full(norms_b)]
    if has_router:
        ins += list(router)
        specs += [full(router[0]), full(router[1])]
    outs, ospecs = [], []
    if has_f:
        outs.append(jax.ShapeDtypeStruct((nb, rows, d), F32))
        ospecs.append(blk(0))
    if has_next:
        outs.append(jax.ShapeDtypeStruct((nb, rows, d), h_dtype))
        ospecs.append(blk(0))
    if has_router:
        outs.append(jax.ShapeDtypeStruct((nb, rows, LANE), F32))
        ospecs.append(pl.BlockSpec((None, ROW_BLK, LANE), lambda b, j: (b, j, 0)))
    res = pl.pallas_call(
        functools.partial(_resid_kernel, nb=nb, ctx_first=ctx_first, has_f=has_f, has_next=has_next,
                          has_router=has_router, ia=ia, gate_k=gate_k, ib=ib, sh_k=sh_k, sc_k=sc_k),
        grid=(nb, nblk),
        in_specs=specs, out_specs=ospecs, out_shape=outs,
        compiler_params=_cparams(("arbitrary", "arbitrary")),
        name="resid_norm",
    )(*ins)
    res = list(res)
    x_new = res.pop(0) if has_f else None
    h = res.pop(0) if has_next else None
    rt = res.pop(0) if has_router else None
    return x_new, h, rt


def _mm_kernel(x_ref, w_ref, o_ref):
    o_ref[...] = jnp.dot(x_ref[...], w_ref[...], preferred_element_type=F32).astype(o_ref.dtype)


def _matmul(x, w, out_dtype, tm=1024, tn=1024):
    m, k = x.shape
    n = w.shape[1]
    return pl.pallas_call(
        _mm_kernel,
        grid=(m // tm, n // tn),
        in_specs=[pl.BlockSpec((tm, k), lambda i, j: (i, 0)),
                  pl.BlockSpec((k, tn), lambda i, j: (0, j))],
        out_specs=pl.BlockSpec((tm, tn), lambda i, j: (i, j)),
        out_shape=jax.ShapeDtypeStruct((m, n), out_dtype),
        compiler_params=_cparams(("arbitrary", "arbitrary")),
        name="matmul",
    )(x, w)


def _ret_kernel(lg_ref, q_ref, k_ref, v_ref, g_ref, cos_ref, sin_ref, rot_ref, o_ref, q_s, k_s, acc_s):
    h = pl.program_id(1)
    lgf = lg_ref[0, h]
    lgb = lg_ref[1, h]
    c = ROW_BLK
    n = q_ref.shape[0] // c
    scale = RET_DK ** -0.5

    q_s[0:c, :] = (q_ref[0:c, :].astype(F32) * scale).astype(BF16)
    k_s[0:c, :] = k_ref[0:c, :]
    rot = rot_ref[...]
    for i in range(1, n):
        rows = slice(i * c, (i + 1) * c)
        trow = slice((i - 1) * c, i * c)
        cs, sn = cos_ref[trow, :], sin_ref[trow, :]
        qa, ka = q_ref[rows, :], k_ref[rows, :]
        qr = jnp.dot(qa, rot, preferred_element_type=F32)
        kr = jnp.dot(ka, rot, preferred_element_type=F32)
        q_s[rows, :] = ((qa.astype(F32) * cs + qr * sn) * scale).astype(BF16)
        k_s[rows, :] = (ka.astype(F32) * cs + kr * sn).astype(BF16)

    ti = lax.broadcasted_iota(jnp.int32, (c, c), 0)
    si = lax.broadcasted_iota(jnp.int32, (c, c), 1)
    dd = (ti - si).astype(F32)
    dmat = jnp.where(dd > 0, jnp.exp(lgf * dd), jnp.where(dd < 0, jnp.exp(-lgb * dd), 2.0))
    tcol = lax.broadcasted_iota(jnp.int32, (c, 1), 0).astype(F32)
    q_f = jnp.exp(lgf * (tcol + 1.0))
    q_b = jnp.exp(lgb * (c - tcol))
    k_f = jnp.exp(lgf * (c - 1.0 - tcol))
    k_b = jnp.exp(lgb * tcol)
    g_f = jnp.exp(lgf * c)
    g_b = jnp.exp(lgb * c)

    def kv_state(kc, vc, kdec):
        kd = (kc.astype(F32) * kdec).astype(BF16)
        return lax.dot_general(kd, vc, (((0,), (0,)), ((), ())), preferred_element_type=F32)

    s_f = jnp.zeros((RET_DK, q_ref.shape[1]), F32)
    for i in range(n):
        rows = slice(i * c, (i + 1) * c)
        qc, kc, vc = q_s[rows, :], k_s[rows, :], v_ref[rows, :]
        sc = lax.dot_general(qc, kc, (((1,), (1,)), ((), ())), preferred_element_type=F32)
        o = jnp.dot((sc * dmat).astype(BF16), vc, preferred_element_type=F32)
        if i > 0:
            qd = (qc.astype(F32) * q_f).astype(BF16)
            o = o + jnp.dot(qd, s_f.astype(BF16), preferred_element_type=F32)
        acc_s[rows, :] = o
        if i < n - 1:
            s_f = g_f * s_f + kv_state(kc, vc, k_f)

    s_b = kv_state(k_s[0:c, :], v_ref[0:c, :], k_b)
    for i in range(n - 1, 0, -1):
        rows = slice(i * c, (i + 1) * c)
        qc, kc, vc = q_s[rows, :], k_s[rows, :], v_ref[rows, :]
        qd = (qc.astype(F32) * q_b).astype(BF16)
        acc_s[rows, :] = acc_s[rows, :] + jnp.dot(qd, s_b.astype(BF16), preferred_element_type=F32)
        if i > 1:
            s_b = g_b * s_b + kv_state(kc, vc, k_b)

    for i in range(n):
        rows = slice(i * c, (i + 1) * c)
        o_ref[rows, :] = (_rms(acc_s[rows, :]) * _silu(g_ref[rows, :].astype(F32))).astype(o_ref.dtype)


def _retention(proj, lg, cos, sin, rot):
    nb, l, _ = proj.shape
    col = lambda c0: pl.BlockSpec((None, l, LANE), lambda b, h: (b, 0, c0 + h))
    const = lambda a: pl.BlockSpec(a.shape, lambda b, h: (0,) * a.ndim)
    return pl.pallas_call(
        _ret_kernel,
        grid=(nb, RET_HEADS),
        in_specs=[pl.BlockSpec(memory_space=pltpu.SMEM),
                  col(C_RET_Q), col(C_RET_K), col(C_RET_V), col(C_RET_G),
                  const(cos), const(sin), const(rot)],
        out_specs=pl.BlockSpec((None, l, LANE), lambda b, h: (b, 0, h)),
        out_shape=jax.ShapeDtypeStruct((nb, l, BRANCH_W), BF16),
        scratch_shapes=[pltpu.VMEM((l, LANE), BF16), pltpu.VMEM((l, LANE), BF16), pltpu.VMEM((l, LANE), F32)],
        compiler_params=_cparams(("arbitrary", "arbitrary")),
        name="retention",
    )(lg, proj, proj, proj, proj, cos, sin, rot)


def _gla_kernel(q_ref, k_ref, v_ref, g_ref, lr_ref, w2_ref, b2_ref, o_ref, accf_s, accb_s, sf_s, sb_s,
                *, n_ctx_chunks):
    c = GLA_CHUNK
    l = q_ref.shape[0]
    n = l // c
    kw = GLA_HEADS * GLA_DK
    vw = GLA_HEADS * GLA_DV
    scale = GLA_DK ** -0.5

    ti = lax.broadcasted_iota(jnp.int32, (c, c), 0)
    si = lax.broadcasted_iota(jnp.int32, (c, c), 1)
    tri_lo = (ti >= si).astype(F32)
    tri_up = (ti <= si).astype(F32)
    srow = lax.broadcasted_iota(jnp.int32, (GLA_HEADS * c, kw), 0) // c
    slane = lax.broadcasted_iota(jnp.int32, (GLA_HEADS * c, kw), 1) // GLA_DK
    stack_mask = srow == slane
    brow = lax.broadcasted_iota(jnp.int32, (vw, kw), 0) // GLA_DV
    blane = lax.broadcasted_iota(jnp.int32, (vw, kw), 1) // GLA_DK
    state_mask = brow == blane

    sf_s[...] = jnp.zeros_like(sf_s)
    sb_s[...] = jnp.zeros_like(sb_s)

    def chunk(r0, d, tri, st_ref, acc_ref, last_row):
        rows = pl.ds(r0, c)
        qc = q_ref[rows, :].astype(F32) * scale
        kc = k_ref[rows, :].astype(F32)
        vc = v_ref[rows, :]
        z = jnp.dot(lr_ref[rows, :], w2_ref[d], preferred_element_type=F32) + b2_ref[d]
        la = (jnp.minimum(z, 0.0) - jnp.log(1.0 + jnp.exp(-jnp.abs(z)))) * (1.0 / GLA_TAU)
        bcum = jnp.dot(tri, la, preferred_element_type=F32, precision=lax.Precision.HIGHEST)
        b_last = bcum[last_row:last_row + 1, :]
        q_dec = (qc * jnp.exp(bcum)).astype(BF16)
        k_inv = (kc * jnp.exp(-bcum)).astype(BF16)
        k_end = (kc * jnp.exp(b_last - bcum)).astype(BF16)
        q_stack = jnp.where(stack_mask, jnp.concatenate([q_dec] * GLA_HEADS, axis=0), jnp.zeros((), BF16))
        sc = lax.dot_general(q_stack, k_inv, (((1,), (1,)), ((), ())), preferred_element_type=F32)
        sc = (sc * jnp.concatenate([tri] * GLA_HEADS, axis=0)).astype(BF16)
        o_full = jnp.dot(sc, vc, preferred_element_type=F32)
        o = jnp.concatenate([o_full[h * c:(h + 1) * c, h * GLA_DV:(h + 1) * GLA_DV]
                             for h in range(GLA_HEADS)], axis=1)
        st = st_ref[...]
        o = o + lax.dot_general(q_dec, st.astype(BF16), (((1,), (1,)), ((), ())), preferred_element_type=F32)
        acc_ref[rows, :] = o
        kv = lax.dot_general(vc, k_end, (((0,), (0,)), ((), ())), preferred_element_type=F32)
        st_ref[...] = jnp.where(state_mask, st * jnp.exp(b_last) + kv, 0.0)

    def step(i, carry):
        chunk(pl.multiple_of(i * c, c), 0, tri_lo, sf_s, accf_s, c - 1)
        cb = jnp.where(i < n_ctx_chunks, n_ctx_chunks - 1 - i, n + n_ctx_chunks - 1 - i)
        chunk(pl.multiple_of(cb * c, c), 1, tri_up, sb_s, accb_s, 0)
        return carry

    lax.fori_loop(0, n, step, 0)

    for i in range(l // ROW_BLK):
        rows = slice(i * ROW_BLK, (i + 1) * ROW_BLK)
        o = accf_s[rows, :] + accb_s[rows, :]
        o = jnp.concatenate([_rms(o[:, h * GLA_DV:(h + 1) * GLA_DV]) for h in range(GLA_HEADS)], axis=1)
        o_ref[rows, :] = (o * _silu(g_ref[rows, :].astype(F32))).astype(o_ref.dtype)


def _gla(proj, w2p, b2):
    nb, l, _ = proj.shape
    kw, vw = GLA_HEADS * GLA_DK, GLA_HEADS * GLA_DV
    col = lambda c0, w: pl.BlockSpec((None, l, w), lambda b: (b, 0, c0 * LANE // w))
    const = lambda a: pl.BlockSpec(a.shape, lambda b: (0,) * a.ndim)
    return pl.pallas_call(
        functools.partial(_gla_kernel, n_ctx_chunks=CTX_LEN // GLA_CHUNK),
        grid=(nb,),
        in_specs=[col(C_GLA_Q, kw), col(C_GLA_K, kw), col(C_GLA_V, vw), col(C_GLA_G, vw), col(C_GLA_LR, LANE),
                  const(w2p), const(b2)],
        out_specs=pl.BlockSpec((None, l, vw), lambda b: (b, 0, 0)),
        out_shape=jax.ShapeDtypeStruct((nb, l, BRANCH_W), BF16),
        scratch_shapes=[pltpu.VMEM((l, vw), F32), pltpu.VMEM((l, vw), F32),
                        pltpu.VMEM((vw, kw), F32), pltpu.VMEM((vw, kw), F32)],
        compiler_params=_cparams(("arbitrary",)),
        name="gla",
    )(proj, proj, proj, proj, proj, w2p, b2)


NA_QROWS = ROW_BLK // GRID_W
NA_SLAB = 3


def _na_kernel(q_ref, kc_ref, k0_ref, k1_ref, k2_ref, vc_ref, v0_ref, v1_ref, v2_ref, bias_ref, o_ref):
    scale = NA_DH ** -0.5
    q = q_ref[...]
    dims = (((1,), (1,)), ((), ()))
    s = [lax.dot_general(q, kc_ref[...], dims, preferred_element_type=F32) * scale]
    for j, kr in enumerate((k0_ref, k1_ref, k2_ref)):
        sj = lax.dot_general(q, kr[...], dims, preferred_element_type=F32) * scale
        s.append(sj + bias_ref[:, j * ROW_BLK:(j + 1) * ROW_BLK])
    m = functools.reduce(jnp.maximum, [jnp.max(x, axis=-1, keepdims=True) for x in s])
    p = [jnp.exp(x - m) for x in s]
    den = functools.reduce(lambda a, b: a + b, [jnp.sum(x, axis=-1, keepdims=True) for x in p])
    o = jnp.zeros((q.shape[0], NA_DH), F32)
    for pj, vr in zip(p, (vc_ref, v0_ref, v1_ref, v2_ref)):
        o = o + jnp.dot(pj.astype(BF16), vr[...], preferred_element_type=F32)
    o_ref[...] = (o / den).astype(o_ref.dtype)


def _na_slab_start(qb, n_lat_blk):
    return jnp.clip(qb - 2, 0, n_lat_blk - NA_SLAB)


def _na(proj, bias):
    nb, l, _ = proj.shape
    nblk = l // ROW_BLK
    nlat = nblk - 1
    blk = lambda c0, rowfn: pl.BlockSpec((None, ROW_BLK, LANE), lambda h, qb, b: (b, rowfn(qb), c0 + h))
    slab = lambda j: (lambda qb: 1 + _na_slab_start(qb, nlat) + j)
    return pl.pallas_call(
        _na_kernel,
        grid=(NA_HEADS, nblk, nb),
        in_specs=[blk(C_NA_Q, lambda qb: qb),
                  blk(C_NA_K, lambda qb: 0), blk(C_NA_K, slab(0)), blk(C_NA_K, slab(1)), blk(C_NA_K, slab(2)),
                  blk(C_NA_V, lambda qb: 0), blk(C_NA_V, slab(0)), blk(C_NA_V, slab(1)), blk(C_NA_V, slab(2)),
                  pl.BlockSpec((None, None, ROW_BLK, NA_SLAB * ROW_BLK), lambda h, qb, b: (h, qb, 0, 0))],
        out_specs=pl.BlockSpec((None, ROW_BLK, LANE), lambda h, qb, b: (b, qb, h)),
        out_shape=jax.ShapeDtypeStruct((nb, l, BRANCH_W), BF16),
        compiler_params=_cparams(("arbitrary", "arbitrary", "arbitrary")),
        name="neighbourhood_attention",
    )(*([proj] * 9), bias)


def _na_bias(rpb, seq):
    rows = seq // GRID_W
    kr = min(NA_WIN_R, rows)
    nlat = seq // ROW_BLK
    cq = np.arange(GRID_W)[:, None]
    ck = np.arange(GRID_W)[None, :]
    win_start = np.clip(cq - NA_WIN_C // 2, 0, GRID_W - NA_WIN_C)
    in_win = (ck >= win_start) & (ck < win_start + NA_WIN_C)
    rel_c = np.clip(ck - cq, 1 - NA_WIN_C, NA_WIN_C - 1) + NA_WIN_C - 1
    tiles = jnp.where(in_win[None, None], rpb[:, :, rel_c].astype(F32), NEG)
    masked = jnp.full((rpb.shape[0], GRID_W, GRID_W), NEG, F32)
    blocks = [jnp.full((rpb.shape[0], ROW_BLK, NA_SLAB * ROW_BLK), NEG, F32)]
    for qb in range(nlat):
        s0 = int(np.clip(qb - 1, 0, nlat - NA_SLAB)) * NA_QROWS
        qrows = []
        for qr in range(NA_QROWS):
            r = qb * NA_QROWS + qr
            k0 = int(np.clip(r - kr // 2, 0, rows - kr))
            assert s0 <= k0 and k0 + kr <= s0 + NA_SLAB * NA_QROWS
            krows = []
            for kk in range(NA_SLAB * NA_QROWS):
                krow = s0 + kk
                if k0 <= krow < k0 + kr:
                    krows.append(tiles[:, krow - r + NA_WIN_R - 1])
                else:
                    krows.append(masked)
            qrows.append(jnp.concatenate(krows, axis=2))
        blocks.append(jnp.concatenate(qrows, axis=1))
    return jnp.stack(blocks, axis=1)


def _fourier_kernel(u_ref, cg_ref, sg_ref, dft_ref, o_ref, ab_s):
    l = u_ref.shape[0]

    @pl.when(pl.program_id(1) == 0)
    def _():
        for g in range(FNET_GROUPS):
            cols = slice(g * FNET_GW, (g + 1) * FNET_GW)
            u = u_ref[:, cols]
            ab_s[0:l, cols] = jnp.dot(u, cg_ref[...], preferred_element_type=F32).astype(BF16)
            ab_s[l:2 * l, cols] = jnp.dot(u, sg_ref[...], preferred_element_type=F32).astype(BF16)

    o_ref[...] = jnp.dot(dft_ref[...], ab_s[...], preferred_element_type=F32).astype(o_ref.dtype)


def _fourier(proj, cg, sg, dft):
    nb, l, _ = proj.shape
    tr = 3 * ROW_BLK
    w = FNET_GROUPS * FNET_GW
    return pl.pallas_call(
        _fourier_kernel,
        grid=(nb, l // tr),
        in_specs=[pl.BlockSpec((None, l, w), lambda b, i: (b, 0, C_FU * LANE // w)),
                  pl.BlockSpec(cg.shape, lambda b, i: (0, 0)),
                  pl.BlockSpec(sg.shape, lambda b, i: (0, 0)),
                  pl.BlockSpec((tr, 2 * l), lambda b, i: (i, 0))],
        out_specs=pl.BlockSpec((None, tr, w), lambda b, i: (b, i, 0)),
        out_shape=jax.ShapeDtypeStruct((nb, l, BRANCH_W), BF16),
        scratch_shapes=[pltpu.VMEM((2 * l, w), BF16)],
        compiler_params=_cparams(("arbitrary", "arbitrary")),
        name="fourier_mix",
    )(proj, cg, sg, dft)


def _dft_tables(n_ctx, seq):
    def cs(n):
        jk = (np.arange(n)[:, None] * np.arange(n)[None, :]) % n
        ang = 2.0 * np.pi * jk / n
        return np.cos(ang) / np.sqrt(n), np.sin(ang) / np.sqrt(n)

    cg, sg = cs(FNET_GW)
    l = n_ctx + seq
    cl = np.zeros((l, l))
    sl = np.zeros((l, l))
    cc, sc = cs(n_ctx)
    cs_, ss_ = cs(seq)
    cl[:n_ctx, :n_ctx], sl[:n_ctx, :n_ctx] = cc, sc
    cl[n_ctx:, n_ctx:], sl[n_ctx:, n_ctx:] = cs_, ss_
    dft = np.concatenate([cl, -sl], axis=1)
    return (jnp.asarray(cg, BF16), jnp.asarray(sg, BF16), jnp.asarray(dft, BF16))


def _merge_kernel(y0_ref, y1_ref, y2_ref, y3_ref, w_ref, g0_ref, g1_ref, g2_ref, g3_ref, o_ref):
    acc = None
    for i, (y_ref, g_ref) in enumerate(zip((y0_ref, y1_ref, y2_ref, y3_ref), (g0_ref, g1_ref, g2_ref, g3_ref))):
        t = _sigmoid(g_ref[...].astype(F32)) * jnp.dot(y_ref[...], w_ref[i], preferred_element_type=F32)
        acc = t if acc is None else acc + t
    o_ref[...] = acc.astype(o_ref.dtype)


def _merge(ys, w_up, proj2d, tm=1024, tn=512):
    t = proj2d.shape[0]
    d = w_up.shape[2]
    gate0 = C_GATE * LANE // tn
    yspec = pl.BlockSpec((tm, BRANCH_W), lambda i, j: (i, 0))
    gspec = lambda br: pl.BlockSpec((tm, tn), lambda i, j: (i, gate0 + br * (d // tn) + j))
    return pl.pallas_call(
        _merge_kernel,
        grid=(t // tm, d // tn),
        in_specs=[yspec] * 4 + [pl.BlockSpec((4, BRANCH_W, tn), lambda i, j: (0, 0, j))] + [gspec(br) for br in range(4)],
        out_specs=pl.BlockSpec((tm, tn), lambda i, j: (i, j)),
        out_shape=jax.ShapeDtypeStruct((t, d), BF16),
        compiler_params=_cparams(("arbitrary", "arbitrary")),
        name="gated_merge",
    )(*ys, w_up, proj2d, proj2d, proj2d, proj2d)


def _ffn_kernel(x_ref, w1_ref, w3_ref, w2_ref, o_ref, acc_s):
    j = pl.program_id(1)
    x = x_ref[...]
    a = jnp.dot(x, w1_ref[...], preferred_element_type=F32)
    b = jnp.dot(x, w3_ref[...], preferred_element_type=F32)
    g = (_silu(a) * b).astype(BF16)
    part = jnp.dot(g, w2_ref[...], preferred_element_type=F32)

    @pl.when(j == 0)
    def _():
        acc_s[...] = part

    @pl.when(j > 0)
    def _():
        acc_s[...] += part

    @pl.when(j == pl.num_programs(1) - 1)
    def _():
        o_ref[...] = acc_s[...].astype(o_ref.dtype)


def _ffn(x, w1, w3, w2, tm=1024, tf=512):
    t, d = x.shape
    ff = w1.shape[1]
    return pl.pallas_call(
        _ffn_kernel,
        grid=(t // tm, ff // tf),
        in_specs=[pl.BlockSpec((tm, d), lambda i, j: (i, 0)),
                  pl.BlockSpec((d, tf), lambda i, j: (0, j)),
                  pl.BlockSpec((d, tf), lambda i, j: (0, j)),
                  pl.BlockSpec((tf, d), lambda i, j: (j, 0))],
        out_specs=pl.BlockSpec((tm, d), lambda i, j: (i, 0)),
        out_shape=jax.ShapeDtypeStruct((t, d), BF16),
        scratch_shapes=[pltpu.VMEM((tm, d), F32)],
        compiler_params=_cparams(("arbitrary", "arbitrary")),
        name="swiglu",
    )(x, w1, w3, w2)


MOE_TM = 512
MOE_TF = 1024


def _moe_kernel(te_ref, nu_ref, tok_ref, h_hbm, rw_ref, w1_ref, w3_ref, w2_ref, o_ref, xg_s, xb_s, acc_s, sem):
    i = pl.program_id(0)
    j = pl.program_id(1)
    tm = xg_s.shape[0]
    used = i < nu_ref[0]

    @pl.when(jnp.logical_and(used, j == 0))
    def _():
        def issue(r, carry):
            tok = tok_ref[i * tm + r]
            pltpu.make_async_copy(h_hbm.at[pl.ds(tok, 1), :], xg_s.at[pl.ds(r, 1), :], sem).start()
            return carry

        lax.fori_loop(0, tm, issue, 0)
        pltpu.make_async_copy(h_hbm.at[pl.ds(0, tm), :], xg_s, sem).wait()
        xb_s[...] = xg_s[...].astype(BF16)

    @pl.when(used)
    def _():
        x = xb_s[...]
        a = jnp.dot(x, w1_ref[...], preferred_element_type=F32)
        b = jnp.dot(x, w3_ref[...], preferred_element_type=F32)
        g = (_silu(a) * b * rw_ref[...]).astype(BF16)
        part = jnp.dot(g, w2_ref[...], preferred_element_type=F32)

        @pl.when(j == 0)
        def _():
            acc_s[...] = part

        @pl.when(j > 0)
        def _():
            acc_s[...] += part

    last_j = j == pl.num_programs(1) - 1

    @pl.when(jnp.logical_and(used, last_j))
    def _():
        o_ref[...] = acc_s[...]

    @pl.when(jnp.logical_and(jnp.logical_not(used), last_j))
    def _():
        o_ref[...] = jnp.zeros_like(o_ref)


def _moe_ffn(h2d, plan, w1, w3, w2):
    tile_e, n_used, src_tok, roww = plan
    p = src_tok.shape[0]
    tm, tf = MOE_TM, MOE_TF
    d = h2d.shape[1]
    ff = w1.shape[2]
    nj = ff // tf
    jj = lambda i, j, nu: jnp.where(i < nu[0], j, nj - 1)
    grid_spec = pltpu.PrefetchScalarGridSpec(
        num_scalar_prefetch=3,
        grid=(p // tm, nj),
        in_specs=[pl.BlockSpec(memory_space=pl.ANY),
                  pl.BlockSpec((tm, 1), lambda i, j, te, nu, tok: (i, 0)),
                  pl.BlockSpec((None, d, tf), lambda i, j, te, nu, tok: (te[i], 0, jj(i, j, nu))),
                  pl.BlockSpec((None, d, tf), lambda i, j, te, nu, tok: (te[i], 0, jj(i, j, nu))),
                  pl.BlockSpec((None, tf, d), lambda i, j, te, nu, tok: (te[i], jj(i, j, nu), 0))],
        out_specs=pl.BlockSpec((tm, d), lambda i, j, te, nu, tok: (i, 0)),
        scratch_shapes=[pltpu.VMEM((tm, d), F32), pltpu.VMEM((tm, d), BF16), pltpu.VMEM((tm, d), F32),
                        pltpu.SemaphoreType.DMA(())],
    )
    return pl.pallas_call(
        _moe_kernel,
        grid_spec=grid_spec,
        out_shape=jax.ShapeDtypeStruct((p, d), F32),
        compiler_params=_cparams(("arbitrary", "arbitrary")),
        name="moe_grouped_swiglu",
    )(tile_e, n_used, src_tok, h2d, roww, w1, w3, w2)


def _combine_kernel(pos_ref, y_hbm, o_ref, g0_s, g1_s, sem):
    i = pl.program_id(0)
    tc = g0_s.shape[0]

    def issue(r, carry):
        t = i * tc + r
        pltpu.make_async_copy(y_hbm.at[pl.ds(pos_ref[2 * t], 1), :], g0_s.at[pl.ds(r, 1), :], sem.at[0]).start()
        pltpu.make_async_copy(y_hbm.at[pl.ds(pos_ref[2 * t + 1], 1), :], g1_s.at[pl.ds(r, 1), :], sem.at[1]).start()
        return carry

    lax.fori_loop(0, tc, issue, 0)
    pltpu.make_async_copy(y_hbm.at[pl.ds(0, tc), :], g0_s, sem.at[0]).wait()
    pltpu.make_async_copy(y_hbm.at[pl.ds(0, tc), :], g1_s, sem.at[1]).wait()
    o_ref[...] = (g0_s[...] + g1_s[...]).astype(o_ref.dtype)


def _moe_combine(y, pos, t):
    d = y.shape[1]
    tc = ROW_BLK
    grid_spec = pltpu.PrefetchScalarGridSpec(
        num_scalar_prefetch=1,
        grid=(t // tc,),
        in_specs=[pl.BlockSpec(memory_space=pl.ANY)],
        out_specs=pl.BlockSpec((tc, d), lambda i, pos: (i, 0)),
        scratch_shapes=[pltpu.VMEM((tc, d), F32), pltpu.VMEM((tc, d), F32), pltpu.SemaphoreType.DMA((2,))],
    )
    return pl.pallas_call(
        _combine_kernel,
        grid_spec=grid_spec,
        out_shape=jax.ShapeDtypeStruct((t, d), BF16),
        compiler_params=_cparams(("arbitrary",)),
        name="moe_combine",
    )(pos, y)


def _moe_plan(route, tm):
    t = route.shape[0]
    e_flat = route[:, 8:10].astype(jnp.int32).reshape(-1)
    w_flat = route[:, 10:12].reshape(-1)
    na = 2 * t
    n_tiles = -(-(na + N_EXPERTS * (tm - 1)) // tm)
    p = n_tiles * tm
    counts = jnp.sum((e_flat[:, None] == jnp.arange(N_EXPERTS)[None, :]).astype(jnp.int32), axis=0)
    tiles_e = (counts + tm - 1) // tm
    tile_end = jnp.cumsum(tiles_e)
    tile_start = tile_end - tiles_e
    n_used = tile_end[-1]
    cstart = jnp.cumsum(counts) - counts
    order = jnp.argsort(e_flat, stable=True)
    sorted_e = e_flat[order]
    pos_sorted = tile_start[sorted_e] * tm + (jnp.arange(na, dtype=jnp.int32) - cstart[sorted_e])
    src_tok = jnp.zeros((p,), jnp.int32).at[pos_sorted].set((order // 2).astype(jnp.int32))
    roww = jnp.zeros((p,), F32).at[pos_sorted].set(w_flat[order])
    pos_of = jnp.zeros((na,), jnp.int32).at[order].set(pos_sorted.astype(jnp.int32))
    tid = jnp.minimum(jnp.arange(n_tiles, dtype=jnp.int32), n_used - 1)
    tile_e = jnp.minimum(jnp.sum((tid[:, None] >= tile_end[None, :]).astype(jnp.int32), axis=1), N_EXPERTS - 1)
    return (tile_e.astype(jnp.int32), n_used.reshape(1).astype(jnp.int32), src_tok, roww.reshape(p, 1)), pos_of


def _moe(h, route, w1, w3, w2):
    nb, rows, d = h.shape
    t = nb * rows
    plan, pos_of = _moe_plan(route.reshape(t, LANE), MOE_TM)
    y = _moe_ffn(h.reshape(t, d), plan, w1, w3, w2)
    return _moe_combine(y, pos_of, t).reshape(nb, rows, d)


def _rope_tables(seq):
    quarter = RET_DK // 4
    inv_freq = ROPE_BASE ** (-jnp.arange(quarter, dtype=F32) / quarter)
    t = jnp.arange(seq, dtype=jnp.int32)
    ang_r = (t // GRID_W).astype(F32)[:, None] * inv_freq[None]
    ang_c = (t % GRID_W).astype(F32)[:, None] * inv_freq[None]
    ang = jnp.concatenate([ang_r, ang_r, ang_c, ang_c], axis=1)
    rot = np.zeros((RET_DK, RET_DK), np.float32)
    for j in range(RET_DK):
        if (j % (2 * quarter)) < quarter:
            rot[j + quarter, j] = -1.0
        else:
            rot[j - quarter, j] = 1.0
    return jnp.cos(ang), jnp.sin(ang), jnp.asarray(rot, BF16)


def _pack_w_in(w):
    d = w.shape[0]
    lr = w[:, W_IN_SPLIT:W_IN_SPLIT + 2 * GLA_RANK]
    pad = jnp.zeros((d, (C_GATE - C_GLA_LR) * LANE - 2 * GLA_RANK), w.dtype)
    return jnp.concatenate([w[:, :W_IN_SPLIT], lr, pad, w[:, W_IN_SPLIT + 2 * GLA_RANK:]], axis=1).astype(BF16)


def kernel(x, c, ctx, c_ctx, w_ada, b_ada, norms, w_in, ret_decay, gla_w2, gla_b2, na_rpb, w_up, w_out,
           ffn_w1, ffn_w3, ffn_w2, moe_router, moe_router_b, moe_w1, moe_w3, moe_w2):
    nb, seq, d = x.shape
    n_ctx = ctx.shape[1]
    l = n_ctx + seq
    nblk = l // ROW_BLK
    assert n_ctx == ROW_BLK and seq % ROW_BLK == 0 and seq // ROW_BLK >= NA_SLAB and d == D_MODEL and nb < 16

    cvec = jnp.zeros((16, d), F32).at[:nb].set(c).at[nb].set(c_ctx)
    mod = _adaln(cvec, w_ada, b_ada)
    cos, sin, rot = _rope_tables(seq)
    cg, sg, dft = _dft_tables(n_ctx, seq)
    log_gamma = jnp.log1p(-jnp.exp(ret_decay.astype(F32)))

    xs = jnp.concatenate([ctx, x], axis=1)
    _, h, _ = _resid(xs, None, None, None, mod[0], norms[0], None, x_off=0, f_off=0, nblk=nblk,
                     ctx_first=True, ib=0, sh_k=0, sc_k=1)
    for li in range(DEPTH):
        last = li == DEPTH - 1
        is_moe = li % 2 == 1
        proj2d = _matmul(h.reshape(nb * l, d), _pack_w_in(w_in[li]), BF16)
        proj = proj2d.reshape(nb, l, N_PROJ)
        w2p = jnp.zeros((2, LANE, GLA_HEADS * GLA_DK), F32)
        w2p = w2p.at[0, :GLA_RANK].set(gla_w2[li, 0]).at[1, GLA_RANK:2 * GLA_RANK].set(gla_w2[li, 1]).astype(BF16)
        y_ret = _retention(proj, log_gamma[li], cos, sin, rot)
        y_na = _na(proj, _na_bias(na_rpb[li], seq))
        y_fn = _fourier(proj, cg, sg, dft)
        y_gla = _gla(proj, w2p, gla_b2[li].reshape(2, 1, -1).astype(F32))
        ys = [y.reshape(nb * l, BRANCH_W) for y in (y_ret, y_na, y_fn, y_gla)]
        merged = _merge(ys, w_up[li].astype(BF16), proj2d)
        o = _matmul(merged, w_out[li].astype(BF16), BF16).reshape(nb, l, d)

        router = None
        if is_moe:
            wr = jnp.zeros((d, LANE), F32).at[:, :N_EXPERTS].set(moe_router[li // 2])
            br = jnp.zeros((1, LANE), F32).at[0, :N_EXPERTS].set(moe_router_b[li // 2])
            router = (wr, br)
        h_dtype = F32 if is_moe else BF16
        if last:
            xs, h2, route = _resid(xs, o, mod[li], norms[li], mod[li], norms[li], router, x_off=1, f_off=1,
                                   nblk=nblk - 1, ctx_first=False, ia=1, gate_k=2, ib=2, sh_k=3, sc_k=4,
                                   h_dtype=h_dtype)
        else:
            xs, h2, route = _resid(xs, o, mod[li], norms[li], mod[li], norms[li], router, x_off=0, f_off=0,
                                   nblk=nblk, ctx_first=True, ia=1, gate_k=2, ib=2, sh_k=3, sc_k=4,
                                   h_dtype=h_dtype)
        rows = h2.shape[1]
        if is_moe:
            f = _moe(h2, route, moe_w1[li // 2].astype(BF16), moe_w3[li // 2].astype(BF16),
                     moe_w2[li // 2].astype(BF16))
        else:
            f = _ffn(h2.reshape(nb * rows, d), ffn_w1[li // 2].astype(BF16), ffn_w3[li // 2].astype(BF16),
                     ffn_w2[li // 2].astype(BF16)).reshape(nb, rows, d)
        if last:
            xs, _, _ = _resid(xs, f, mod[li], norms[li], None, None, None, x_off=0, f_off=0, nblk=nblk - 1,
                              ctx_first=False, ia=3, gate_k=5)
        else:
            xs, h, _ = _resid(xs, f, mod[li], norms[li], mod[li + 1], norms[li + 1], None, x_off=0, f_off=0,
                              nblk=nblk, ctx_first=True, ia=3, gate_k=5, ib=0, sh_k=0, sc_k=1)
    return xs
```

```python
import functools
import math

import numpy as np
import jax
import jax.numpy as jnp
from jax import lax
from jax.experimental import pallas as pl
from jax.experimental.pallas import tpu as pltpu

F32 = jnp.float32
BF16 = jnp.bfloat16

D_MODEL = 2048
DEPTH = 4
GRID_W = 64
CTX_LEN = 256
EPS = 1e-6
ROPE_BASE = 10000.0
RET_HEADS, RET_DK = 4, 128
NA_HEADS, NA_DH = 4, 128
NA_WIN_R, NA_WIN_C = 8, 16
FNET_GROUPS, FNET_GW = 4, 128
GLA_HEADS, GLA_DK, GLA_DV, GLA_RANK, GLA_TAU = 4, 64, 128, 16, 16.0
GLA_CHUNK = 64
BRANCH_W = 512
N_EXPERTS = 8

LANE = 128
ROW_BLK = 256
VMEM_LIMIT = 56 * 1024 * 1024

C_RET_Q, C_RET_K, C_RET_V, C_RET_G = 0, 4, 8, 12
C_NA_Q, C_NA_K, C_NA_V = 16, 20, 24
C_FU = 28
C_GLA_Q, C_GLA_K, C_GLA_V, C_GLA_G, C_GLA_LR = 32, 34, 36, 40, 44
C_GATE = 48
N_PROJ = (C_GATE + 4 * D_MODEL // LANE) * LANE
W_IN_SPLIT = 5632

NEG = -1e30


def _cparams(sem, row_dma=False):
    return pltpu.CompilerParams(dimension_semantics=sem, vmem_limit_bytes=VMEM_LIMIT,
                                disable_bounds_checks=row_dma)


def _silu(x):
    return x * (1.0 / (1.0 + jnp.exp(-x)))


def _sigmoid(x):
    return 1.0 / (1.0 + jnp.exp(-x))


def _rms(x):
    return x * lax.rsqrt(jnp.mean(x * x, axis=-1, keepdims=True) + EPS)


def _ada_kernel(c_ref, w_ref, b_ref, o_ref):
    s = _silu(c_ref[...]).astype(BF16)
    o_ref[...] = jnp.dot(s, w_ref[...].astype(BF16), preferred_element_type=F32) + b_ref[...]


def _adaln(cvec, w_ada, b_ada):
    depth, d, n = w_ada.shape
    tn = 1024
    return pl.pallas_call(
        _ada_kernel,
        grid=(depth, n // tn),
        in_specs=[pl.BlockSpec((16, d), lambda l, j: (0, 0)),
                  pl.BlockSpec((None, d, tn), lambda l, j: (l, 0, j)),
                  pl.BlockSpec((None, 1, tn), lambda l, j: (l, 0, j))],
        out_specs=pl.BlockSpec((None, 16, tn), lambda l, j: (l, 0, j)),
        out_shape=jax.ShapeDtypeStruct((depth, 16, n), F32),
        compiler_params=_cparams(("arbitrary", "arbitrary")),
        name="adaln",
    )(cvec, w_ada, b_ada.reshape(depth, 1, n))


def _resid_kernel(*refs, nb, ctx_first, has_f, has_next, has_router, ia, gate_k, ib, sh_k, sc_k):
    it = iter(refs)
    x_ref = next(it)
    f_ref = next(it) if has_f else None
    moda_ref = next(it) if has_f else None
    na_ref = next(it) if has_f else None
    modb_ref = next(it) if has_next else None
    nbn_ref = next(it) if has_next else None
    wr_ref = next(it) if has_router else None
    br_ref = next(it) if has_router else None
    xo_ref = next(it) if has_f else None
    h_ref = next(it) if has_next else None
    rt_ref = next(it) if has_router else None

    b = pl.program_id(0)
    j = pl.program_id(1)
    row = jnp.where(j == 0, nb, b) if ctx_first else b
    d = x_ref.shape[-1]

    def modv(ref, k):
        return ref[pl.ds(row, 1), k * d:(k + 1) * d]

    x = x_ref[...]
    if has_f:
        f = f_ref[...].astype(F32)
        x = x + modv(moda_ref, gate_k) * (_rms(f) * na_ref[ia:ia + 1, :])
        xo_ref[...] = x
    if has_next:
        h = _rms(x) * nbn_ref[ib:ib + 1, :]
        h = h * (1.0 + modv(modb_ref, sc_k)) + modv(modb_ref, sh_k)
        h_ref[...] = h.astype(h_ref.dtype)
        if has_router:
            logits = jnp.dot(h, wr_ref[...], preferred_element_type=F32,
                             precision=lax.Precision.HIGHEST) + br_ref[...]
            lane = lax.broadcasted_iota(jnp.int32, logits.shape, 1)
            lg = jnp.where(lane < N_EXPERTS, logits, -jnp.inf)
            v1 = jnp.max(lg, axis=-1, keepdims=True)
            i1 = jnp.min(jnp.where(lg == v1, lane, LANE), axis=-1, keepdims=True)
            lg2 = jnp.where(lane == i1, -jnp.inf, lg)
            v2 = jnp.max(lg2, axis=-1, keepdims=True)
            i2 = jnp.min(jnp.where(lg2 == v2, lane, LANE), axis=-1, keepdims=True)
            e2 = jnp.exp(v2 - v1)
            w1 = 1.0 / (1.0 + e2)
            w2 = e2 / (1.0 + e2)
            zero = jnp.zeros_like(logits)
            rt = (jnp.where(lane == 8, i1.astype(F32), zero) + jnp.where(lane == 9, i2.astype(F32), zero)
                  + jnp.where(lane == 10, w1, zero) + jnp.where(lane == 11, w2, zero))
            rt_ref[...] = rt


def _resid(x, f, mod_a, norms_a, mod_b, norms_b, router, *, x_off, f_off, nblk, ctx_first,
           ia=0, gate_k=0, ib=0, sh_k=0, sc_k=0, h_dtype=BF16):
    nb, _, d = x.shape
    has_f = f is not None
    has_next = mod_b is not None
    has_router = router is not None
    rows = nblk * ROW_BLK
    blk = lambda off: pl.BlockSpec((None, ROW_BLK, d), lambda b, j: (b, j + off, 0))
    full = lambda a: pl.BlockSpec(a.shape, lambda b, j: (0,) * a.ndim)
    ins, specs = [x], [blk(x_off)]
    if has_f:
        ins += [f, mod_a, norms_a]
        specs += [blk(f_off), full(mod_a), full(norms_a)]
    if has_next:
        ins += [mod_b, norms_b]
        specs += [full(mod_b), full(norms_b)]
    if has_router:
        ins += list(router)
        specs += [full(router[0]), full(router[1])]
    outs, ospecs = [], []
    if has_f:
        outs.append(jax.ShapeDtypeStruct((nb, rows, d), F32))
        ospecs.append(blk(0))
    if has_next:
        outs.append(jax.ShapeDtypeStruct((nb, rows, d), h_dtype))
        ospecs.append(blk(0))
    if has_router:
        outs.append(jax.ShapeDtypeStruct((nb, rows, LANE), F32))
        ospecs.append(pl.BlockSpec((None, ROW_BLK, LANE), lambda b, j: (b, j, 0)))
    res = pl.pallas_call(
        functools.partial(_resid_kernel, nb=nb, ctx_first=ctx_first, has_f=has_f, has_next=has_next,
                          has_router=has_router, ia=ia, gate_k=gate_k, ib=ib, sh_k=sh_k, sc_k=sc_k),
        grid=(nb, nblk),
        in_specs=specs, out_specs=ospecs, out_shape=outs,
        compiler_params=_cparams(("arbitrary", "arbitrary")),
        name="resid_norm",
    )(*ins)
    res = list(res)
    x_new = res.pop(0) if has_f else None
    h = res.pop(0) if has_next else None
    rt = res.pop(0) if has_router else None
    return x_new, h, rt


def _mm_kernel(x_ref, w_ref, o_ref):
    o_ref[...] = jnp.dot(x_ref[...], w_ref[...], preferred_element_type=F32).astype(o_ref.dtype)


def _matmul(x, w, li, out_dtype, tm=1024, tn=1024):
    m, k = x.shape
    n = w.shape[2]
    return pl.pallas_call(
        _mm_kernel,
        grid=(m // tm, n // tn),
        in_specs=[pl.BlockSpec((tm, k), lambda i, j: (i, 0)),
                  pl.BlockSpec((None, k, tn), lambda i, j: (li, 0, j))],
        out_specs=pl.BlockSpec((tm, tn), lambda i, j: (i, j)),
        out_shape=jax.ShapeDtypeStruct((m, n), out_dtype),
        compiler_params=_cparams(("arbitrary", "arbitrary")),
        name="matmul",
    )(x, w)


def _ret_kernel(lg_ref, q_ref, k_ref, v_ref, g_ref, cos_ref, sin_ref, rot_ref, o_ref, q_s, k_s, acc_s):
    h = pl.program_id(1)
    lgf = lg_ref[0, h]
    lgb = lg_ref[1, h]
    c = ROW_BLK
    n = q_ref.shape[0] // c
    scale = RET_DK ** -0.5

    q_s[0:c, :] = (q_ref[0:c, :].astype(F32) * scale).astype(BF16)
    k_s[0:c, :] = k_ref[0:c, :]
    rot = rot_ref[...]
    for i in range(1, n):
        rows = slice(i * c, (i + 1) * c)
        trow = slice((i - 1) * c, i * c)
        cs, sn = cos_ref[trow, :], sin_ref[trow, :]
        qa, ka = q_ref[rows, :], k_ref[rows, :]
        qr = jnp.dot(qa, rot, preferred_element_type=F32)
        kr = jnp.dot(ka, rot, preferred_element_type=F32)
        q_s[rows, :] = ((qa.astype(F32) * cs + qr * sn) * scale).astype(BF16)
        k_s[rows, :] = (ka.astype(F32) * cs + kr * sn).astype(BF16)

    ti = lax.broadcasted_iota(jnp.int32, (c, c), 0)
    si = lax.broadcasted_iota(jnp.int32, (c, c), 1)
    dd = (ti - si).astype(F32)
    dmat = jnp.where(dd > 0, jnp.exp(lgf * dd), jnp.where(dd < 0, jnp.exp(-lgb * dd), 2.0))
    tcol = lax.broadcasted_iota(jnp.int32, (c, 1), 0).astype(F32)
    q_f = jnp.exp(lgf * (tcol + 1.0))
    q_b = jnp.exp(lgb * (c - tcol))
    k_f = jnp.exp(lgf * (c - 1.0 - tcol))
    k_b = jnp.exp(lgb * tcol)
    g_f = jnp.exp(lgf * c)
    g_b = jnp.exp(lgb * c)

    def kv_state(kc, vc, kdec):
        kd = (kc.astype(F32) * kdec).astype(BF16)
        return lax.dot_general(kd, vc, (((0,), (0,)), ((), ())), preferred_element_type=F32)

    s_f = jnp.zeros((RET_DK, q_ref.shape[1]), F32)
    for i in range(n):
        rows = slice(i * c, (i + 1) * c)
        qc, kc, vc = q_s[rows, :], k_s[rows, :], v_ref[rows, :]
        sc = lax.dot_general(qc, kc, (((1,), (1,)), ((), ())), preferred_element_type=F32)
        o = jnp.dot((sc * dmat).astype(BF16), vc, preferred_element_type=F32)
        if i > 0:
            qd = (qc.astype(F32) * q_f).astype(BF16)
            o = o + jnp.dot(qd, s_f.astype(BF16), preferred_element_type=F32)
        acc_s[rows, :] = o
        if i < n - 1:
            s_f = g_f * s_f + kv_state(kc, vc, k_f)

    s_b = kv_state(k_s[0:c, :], v_ref[0:c, :], k_b)
    for i in range(n - 1, 0, -1):
        rows = slice(i * c, (i + 1) * c)
        qc, kc, vc = q_s[rows, :], k_s[rows, :], v_ref[rows, :]
        qd = (qc.astype(F32) * q_b).astype(BF16)
        acc_s[rows, :] = acc_s[rows, :] + jnp.dot(qd, s_b.astype(BF16), preferred_element_type=F32)
        if i > 1:
            s_b = g_b * s_b + kv_state(kc, vc, k_b)

    for i in range(n):
        rows = slice(i * c, (i + 1) * c)
        o_ref[rows, :] = (_rms(acc_s[rows, :]) * _silu(g_ref[rows, :].astype(F32))).astype(o_ref.dtype)


def _retention(proj, lg, cos, sin, rot):
    nb, l, _ = proj.shape
    col = lambda c0: pl.BlockSpec((None, l, LANE), lambda b, h: (b, 0, c0 + h))
    const = lambda a: pl.BlockSpec(a.shape, lambda b, h: (0,) * a.ndim)
    return pl.pallas_call(
        _ret_kernel,
        grid=(nb, RET_HEADS),
        in_specs=[pl.BlockSpec(memory_space=pltpu.SMEM),
                  col(C_RET_Q), col(C_RET_K), col(C_RET_V), col(C_RET_G),
                  const(cos), const(sin), const(rot)],
        out_specs=pl.BlockSpec((None, l, LANE), lambda b, h: (b, 0, h)),
        out_shape=jax.ShapeDtypeStruct((nb, l, BRANCH_W), BF16),
        scratch_shapes=[pltpu.VMEM((l, LANE), BF16), pltpu.VMEM((l, LANE), BF16), pltpu.VMEM((l, LANE), F32)],
        compiler_params=_cparams(("arbitrary", "arbitrary")),
        name="retention",
    )(lg, proj, proj, proj, proj, cos, sin, rot)


def _gla_kernel(q_ref, k_ref, v_ref, g_ref, lr_ref, w2_ref, b2_ref, o_ref,
                accf_s, accb_s, st_s, qd_s, ki_s, ke_s, dl_s, *, n_ctx_chunks):
    c = GLA_CHUNK
    l = q_ref.shape[0]
    n = l // c
    cpb = ROW_BLK // c
    kw = GLA_HEADS * GLA_DK
    scale = GLA_DK ** -0.5

    ti = lax.broadcasted_iota(jnp.int32, (ROW_BLK, ROW_BLK), 0)
    si = lax.broadcasted_iota(jnp.int32, (ROW_BLK, ROW_BLK), 1)
    same = ((ti // c) == (si // c)).astype(F32)
    cum_lo = (same * (ti >= si).astype(F32)).astype(BF16)
    cum_up = (same * (ti <= si).astype(F32)).astype(BF16)

    def prep(bi, carry):
        rows = pl.ds(pl.multiple_of(bi * ROW_BLK, ROW_BLK), ROW_BLK)
        lr = lr_ref[rows, :]
        qf = q_ref[rows, :].astype(F32) * scale
        kf = k_ref[rows, :].astype(F32)
        for d, cum, last in ((0, cum_lo, c - 1), (1, cum_up, 0)):
            z = jnp.dot(lr, w2_ref[d], preferred_element_type=F32) + b2_ref[d]
            la = (jnp.minimum(z, 0.0) - jnp.log(1.0 + jnp.exp(-jnp.abs(z)))) * (1.0 / GLA_TAU)
            la_hi = la.astype(BF16)
            la_lo = (la - la_hi.astype(F32)).astype(BF16)
            bcum = (jnp.dot(cum, la_hi, preferred_element_type=F32)
                    + jnp.dot(cum, la_lo, preferred_element_type=F32))
            bl = jnp.concatenate([jnp.broadcast_to(bcum[cc * c + last:cc * c + last + 1, :], (c, kw))
                                  for cc in range(cpb)], axis=0)
            qd_s[d, rows, :] = (qf * jnp.exp(bcum)).astype(BF16)
            ki_s[d, rows, :] = (kf * jnp.exp(-bcum)).astype(BF16)
            ke_s[d, rows, :] = (kf * jnp.exp(bl - bcum)).astype(BF16)
            dec = jnp.exp(bl)
            for cc in range(cpb):
                dl_s[d, pl.ds(pl.multiple_of((bi * cpb + cc) * 8, 8), 8), :] = dec[cc * c:cc * c + 8, :]
        return carry

    lax.fori_loop(0, l // ROW_BLK, prep, 0)

    t4 = lax.broadcasted_iota(jnp.int32, (GLA_HEADS * c, c), 0) % c
    s4 = lax.broadcasted_iota(jnp.int32, (GLA_HEADS * c, c), 1)
    mask_lo = (t4 >= s4).astype(F32)
    mask_up = (t4 <= s4).astype(F32)
    srow = lax.broadcasted_iota(jnp.int32, (GLA_HEADS * c, kw), 0) // c
    slane = lax.broadcasted_iota(jnp.int32, (GLA_HEADS * c, kw), 1) // GLA_DK
    stack_mask = srow == slane
    lane_head = lax.broadcasted_iota(jnp.int32, (GLA_DV, kw), 1) // GLA_DK
    lanes = (((1,), (1,)), ((), ()))
    st_s[...] = jnp.zeros_like(st_s)

    def chunk(ci, d, mask, acc_ref):
        rows = pl.ds(pl.multiple_of(ci * c, c), c)
        q_dec, k_inv, k_end = qd_s[d, rows, :], ki_s[d, rows, :], ke_s[d, rows, :]
        vc = v_ref[rows, :]
        dec = dl_s[d, pl.ds(pl.multiple_of(ci * 8, 8), 1), :]
        q_stack = jnp.where(stack_mask, jnp.concatenate([q_dec] * GLA_HEADS, axis=0), jnp.zeros((), BF16))
        sc = (lax.dot_general(q_stack, k_inv, lanes, preferred_element_type=F32) * mask).astype(BF16)
        st = st_s[d]
        inter = lax.dot_general(q_stack, st.astype(BF16), lanes, preferred_element_type=F32)
        acc_ref[rows, :] = jnp.concatenate(
            [jnp.dot(sc[h * c:(h + 1) * c, :], vc[:, h * GLA_DV:(h + 1) * GLA_DV], preferred_element_type=F32)
             + inter[h * c:(h + 1) * c, :] for h in range(GLA_HEADS)], axis=1)
        full = lax.dot_general(vc, k_end, (((0,), (0,)), ((), ())), preferred_element_type=F32)
        comp = full[0:GLA_DV, :]
        for h in range(1, GLA_HEADS):
            comp = jnp.where(lane_head == h, full[h * GLA_DV:(h + 1) * GLA_DV, :], comp)
        st_s[d] = st * dec + comp

    def step(i, carry):
        chunk(i, 0, mask_lo, accf_s)
        chunk(jnp.where(i < n_ctx_chunks, n_ctx_chunks - 1 - i, n + n_ctx_chunks - 1 - i), 1, mask_up, accb_s)
        return carry

    lax.fori_loop(0, n, step, 0, unroll=2)

    for i in range(l // ROW_BLK):
        rows = slice(i * ROW_BLK, (i + 1) * ROW_BLK)
        o = accf_s[rows, :] + accb_s[rows, :]
        o = jnp.concatenate([_rms(o[:, h * GLA_DV:(h + 1) * GLA_DV]) for h in range(GLA_HEADS)], axis=1)
        o_ref[rows, :] = (o * _silu(g_ref[rows, :].astype(F32))).astype(o_ref.dtype)


def _gla(proj, w2p, b2):
    nb, l, _ = proj.shape
    kw, vw = GLA_HEADS * GLA_DK, GLA_HEADS * GLA_DV
    col = lambda c0, w: pl.BlockSpec((None, l, w), lambda b: (b, 0, c0 * LANE // w))
    const = lambda a: pl.BlockSpec(a.shape, lambda b: (0,) * a.ndim)
    return pl.pallas_call(
        functools.partial(_gla_kernel, n_ctx_chunks=CTX_LEN // GLA_CHUNK),
        grid=(nb,),
        in_specs=[col(C_GLA_Q, kw), col(C_GLA_K, kw), col(C_GLA_V, vw), col(C_GLA_G, vw), col(C_GLA_LR, LANE),
                  const(w2p), const(b2)],
        out_specs=pl.BlockSpec((None, l, vw), lambda b: (b, 0, 0)),
        out_shape=jax.ShapeDtypeStruct((nb, l, BRANCH_W), BF16),
        scratch_shapes=[pltpu.VMEM((l, vw), F32), pltpu.VMEM((l, vw), F32),
                        pltpu.VMEM((2, GLA_DV, kw), F32),
                        pltpu.VMEM((2, l, kw), BF16), pltpu.VMEM((2, l, kw), BF16), pltpu.VMEM((2, l, kw), BF16),
                        pltpu.VMEM((2, l // GLA_CHUNK * 8, kw), F32)],
        compiler_params=_cparams(("arbitrary",)),
        name="gla",
    )(proj, proj, proj, proj, proj, w2p, b2)


NA_QROWS = ROW_BLK // GRID_W
NA_SLAB = 3


def _na_kernel(q_ref, kc_ref, k0_ref, k1_ref, k2_ref, vc_ref, v0_ref, v1_ref, v2_ref, bias_ref, o_ref):
    scale = NA_DH ** -0.5
    dims = (((1,), (1,)), ((), ()))
    for h in range(NA_HEADS):
        cols = slice(h * NA_DH, (h + 1) * NA_DH)
        q = q_ref[:, cols]
        s = [lax.dot_general(q, kc_ref[:, cols], dims, preferred_element_type=F32) * scale]
        for j, kr in enumerate((k0_ref, k1_ref, k2_ref)):
            sj = lax.dot_general(q, kr[:, cols], dims, preferred_element_type=F32) * scale
            s.append(sj + bias_ref[h, :, j * ROW_BLK:(j + 1) * ROW_BLK])
        m = functools.reduce(jnp.maximum, [jnp.max(x, axis=-1, keepdims=True) for x in s])
        p = [jnp.exp(x - m) for x in s]
        den = functools.reduce(lambda a, b: a + b, [jnp.sum(x, axis=-1, keepdims=True) for x in p])
        o = jnp.zeros((q.shape[0], NA_DH), F32)
        for pj, vr in zip(p, (vc_ref, v0_ref, v1_ref, v2_ref)):
            o = o + jnp.dot(pj.astype(BF16), vr[:, cols], preferred_element_type=F32)
        o_ref[:, cols] = (o / den).astype(o_ref.dtype)


def _na_slab_start(qb, n_lat_blk):
    return jnp.clip(qb - 2, 0, n_lat_blk - NA_SLAB)


def _na(proj, bias):
    nb, l, _ = proj.shape
    nblk = l // ROW_BLK
    nlat = nblk - 1
    w = NA_HEADS * NA_DH
    blk = lambda c0, rowfn: pl.BlockSpec((None, ROW_BLK, w), lambda qb, b: (b, rowfn(qb), c0 * LANE // w))
    slab = lambda j: (lambda qb: 1 + _na_slab_start(qb, nlat) + j)
    return pl.pallas_call(
        _na_kernel,
        grid=(nblk, nb),
        in_specs=[blk(C_NA_Q, lambda qb: qb),
                  blk(C_NA_K, lambda qb: 0), blk(C_NA_K, slab(0)), blk(C_NA_K, slab(1)), blk(C_NA_K, slab(2)),
                  blk(C_NA_V, lambda qb: 0), blk(C_NA_V, slab(0)), blk(C_NA_V, slab(1)), blk(C_NA_V, slab(2)),
                  pl.BlockSpec((NA_HEADS, None, ROW_BLK, NA_SLAB * ROW_BLK), lambda qb, b: (0, qb, 0, 0))],
        out_specs=pl.BlockSpec((None, ROW_BLK, w), lambda qb, b: (b, qb, 0)),
        out_shape=jax.ShapeDtypeStruct((nb, l, BRANCH_W), BF16),
        compiler_params=_cparams(("arbitrary", "arbitrary")),
        name="neighbourhood_attention",
    )(*([proj] * 9), bias)


def _na_bias(rpb, seq):
    rows = seq // GRID_W
    kr = min(NA_WIN_R, rows)
    nlat = seq // ROW_BLK
    cq = np.arange(GRID_W)[:, None]
    ck = np.arange(GRID_W)[None, :]
    win_start = np.clip(cq - NA_WIN_C // 2, 0, GRID_W - NA_WIN_C)
    in_win = (ck >= win_start) & (ck < win_start + NA_WIN_C)
    rel_c = np.clip(ck - cq, 1 - NA_WIN_C, NA_WIN_C - 1) + NA_WIN_C - 1
    tiles = jnp.where(in_win[None, None], rpb[:, :, rel_c].astype(F32), NEG)
    masked = jnp.full((rpb.shape[0], GRID_W, GRID_W), NEG, F32)
    blocks = [jnp.full((rpb.shape[0], ROW_BLK, NA_SLAB * ROW_BLK), NEG, F32)]
    for qb in range(nlat):
        s0 = int(np.clip(qb - 1, 0, nlat - NA_SLAB)) * NA_QROWS
        qrows = []
        for qr in range(NA_QROWS):
            r = qb * NA_QROWS + qr
            k0 = int(np.clip(r - kr // 2, 0, rows - kr))
            assert s0 <= k0 and k0 + kr <= s0 + NA_SLAB * NA_QROWS
            krows = []
            for kk in range(NA_SLAB * NA_QROWS):
                krow = s0 + kk
                if k0 <= krow < k0 + kr:
                    krows.append(tiles[:, krow - r + NA_WIN_R - 1])
                else:
                    krows.append(masked)
            qrows.append(jnp.concatenate(krows, axis=2))
        blocks.append(jnp.concatenate(qrows, axis=1))
    return jnp.stack(blocks, axis=1)


def _fourier_kernel(u_ref, cg_ref, sg_ref, dft_ref, o_ref, ab_s):
    l = u_ref.shape[0]

    @pl.when(pl.program_id(1) == 0)
    def _():
        for g in range(FNET_GROUPS):
            cols = slice(g * FNET_GW, (g + 1) * FNET_GW)
            u = u_ref[:, cols]
            ab_s[0:l, cols] = jnp.dot(u, cg_ref[...], preferred_element_type=F32).astype(BF16)
            ab_s[l:2 * l, cols] = jnp.dot(u, sg_ref[...], preferred_element_type=F32).astype(BF16)

    o_ref[...] = jnp.dot(dft_ref[...], ab_s[...], preferred_element_type=F32).astype(o_ref.dtype)


def _fourier(proj, cg, sg, dft):
    nb, l, _ = proj.shape
    tr = 3 * ROW_BLK
    w = FNET_GROUPS * FNET_GW
    return pl.pallas_call(
        _fourier_kernel,
        grid=(nb, l // tr),
        in_specs=[pl.BlockSpec((None, l, w), lambda b, i: (b, 0, C_FU * LANE // w)),
                  pl.BlockSpec(cg.shape, lambda b, i: (0, 0)),
                  pl.BlockSpec(sg.shape, lambda b, i: (0, 0)),
                  pl.BlockSpec((tr, 2 * l), lambda b, i: (i, 0))],
        out_specs=pl.BlockSpec((None, tr, w), lambda b, i: (b, i, 0)),
        out_shape=jax.ShapeDtypeStruct((nb, l, BRANCH_W), BF16),
        scratch_shapes=[pltpu.VMEM((2 * l, w), BF16)],
        compiler_params=_cparams(("arbitrary", "arbitrary")),
        name="fourier_mix",
    )(proj, cg, sg, dft)


def _dft_tables(n_ctx, seq):
    def cs(n):
        jk = (np.arange(n)[:, None] * np.arange(n)[None, :]) % n
        ang = 2.0 * np.pi * jk / n
        return np.cos(ang) / np.sqrt(n), np.sin(ang) / np.sqrt(n)

    cg, sg = cs(FNET_GW)
    l = n_ctx + seq
    cl = np.zeros((l, l))
    sl = np.zeros((l, l))
    cc, sc = cs(n_ctx)
    cs_, ss_ = cs(seq)
    cl[:n_ctx, :n_ctx], sl[:n_ctx, :n_ctx] = cc, sc
    cl[n_ctx:, n_ctx:], sl[n_ctx:, n_ctx:] = cs_, ss_
    dft = np.concatenate([cl, -sl], axis=1)
    return (jnp.asarray(cg, BF16), jnp.asarray(sg, BF16), jnp.asarray(dft, BF16))


def _merge_kernel(y0_ref, y1_ref, y2_ref, y3_ref, w_ref, g0_ref, g1_ref, g2_ref, g3_ref, o_ref):
    acc = None
    for i, (y_ref, g_ref) in enumerate(zip((y0_ref, y1_ref, y2_ref, y3_ref), (g0_ref, g1_ref, g2_ref, g3_ref))):
        t = _sigmoid(g_ref[...].astype(F32)) * jnp.dot(y_ref[...], w_ref[i], preferred_element_type=F32)
        acc = t if acc is None else acc + t
    o_ref[...] = acc.astype(o_ref.dtype)


def _merge(ys, w_up, li, proj2d, tm=1024, tn=512):
    t = proj2d.shape[0]
    d = w_up.shape[3]
    gate0 = C_GATE * LANE // tn
    yspec = pl.BlockSpec((tm, BRANCH_W), lambda i, j: (i, 0))
    gspec = lambda br: pl.BlockSpec((tm, tn), lambda i, j: (i, gate0 + br * (d // tn) + j))
    return pl.pallas_call(
        _merge_kernel,
        grid=(t // tm, d // tn),
        in_specs=[yspec] * 4 + [pl.BlockSpec((None, 4, BRANCH_W, tn), lambda i, j: (li, 0, 0, j))]
        + [gspec(br) for br in range(4)],
        out_specs=pl.BlockSpec((tm, tn), lambda i, j: (i, j)),
        out_shape=jax.ShapeDtypeStruct((t, d), BF16),
        compiler_params=_cparams(("arbitrary", "arbitrary")),
        name="gated_merge",
    )(*ys, w_up, proj2d, proj2d, proj2d, proj2d)


def _ffn_kernel(x_ref, w1_ref, w3_ref, w2_ref, o_ref, acc_s):
    j = pl.program_id(1)
    x = x_ref[...]
    a = jnp.dot(x, w1_ref[...], preferred_element_type=F32)
    b = jnp.dot(x, w3_ref[...], preferred_element_type=F32)
    g = (_silu(a) * b).astype(BF16)
    part = jnp.dot(g, w2_ref[...], preferred_element_type=F32)

    @pl.when(j == 0)
    def _():
        acc_s[...] = part

    @pl.when(j > 0)
    def _():
        acc_s[...] += part

    @pl.when(j == pl.num_programs(1) - 1)
    def _():
        o_ref[...] = acc_s[...].astype(o_ref.dtype)


def _ffn(x, w1, w3, w2, li, tm=1024, tf=512):
    t, d = x.shape
    ff = w1.shape[2]
    return pl.pallas_call(
        _ffn_kernel,
        grid=(t // tm, ff // tf),
        in_specs=[pl.BlockSpec((tm, d), lambda i, j: (i, 0)),
                  pl.BlockSpec((None, d, tf), lambda i, j: (li, 0, j)),
                  pl.BlockSpec((None, d, tf), lambda i, j: (li, 0, j)),
                  pl.BlockSpec((None, tf, d), lambda i, j: (li, j, 0))],
        out_specs=pl.BlockSpec((tm, d), lambda i, j: (i, 0)),
        out_shape=jax.ShapeDtypeStruct((t, d), BF16),
        scratch_shapes=[pltpu.VMEM((tm, d), F32)],
        compiler_params=_cparams(("arbitrary", "arbitrary")),
        name="swiglu",
    )(x, w1, w3, w2)


MOE_TM = 512
MOE_TF = 1024


def _moe_kernel(te_ref, nu_ref, tok_ref, h_hbm, rw_ref, w1_ref, w3_ref, w2_ref, o_ref, xg_s, xb_s, acc_s, sem):
    i = pl.program_id(0)
    j = pl.program_id(1)
    tm = xg_s.shape[0]
    used = i < nu_ref[0]

    @pl.when(jnp.logical_and(used, j == 0))
    def _():
        def issue(r, carry):
            tok = tok_ref[i * tm + r]
            pltpu.make_async_copy(h_hbm.at[pl.ds(tok, 1), :], xg_s.at[pl.ds(r, 1), :], sem).start()
            return carry

        lax.fori_loop(0, tm, issue, 0)
        pltpu.make_async_copy(h_hbm.at[pl.ds(0, tm), :], xg_s, sem).wait()
        xb_s[...] = xg_s[...].astype(BF16)

    @pl.when(used)
    def _():
        x = xb_s[...]
        a = jnp.dot(x, w1_ref[...], preferred_element_type=F32)
        b = jnp.dot(x, w3_ref[...], preferred_element_type=F32)
        g = (_silu(a) * b * rw_ref[...]).astype(BF16)
        part = jnp.dot(g, w2_ref[...], preferred_element_type=F32)

        @pl.when(j == 0)
        def _():
            acc_s[...] = part

        @pl.when(j > 0)
        def _():
            acc_s[...] += part

    last_j = j == pl.num_programs(1) - 1

    @pl.when(jnp.logical_and(used, last_j))
    def _():
        o_ref[...] = acc_s[...]

    @pl.when(jnp.logical_and(jnp.logical_not(used), last_j))
    def _():
        o_ref[...] = jnp.zeros_like(o_ref)


def _moe_ffn(h2d, plan, w1, w3, w2, li):
    tile_e, n_used, src_tok, roww = plan
    p = src_tok.shape[0]
    tm, tf = MOE_TM, MOE_TF
    d = h2d.shape[1]
    ff = w1.shape[3]
    nj = ff // tf
    jj = lambda i, j, nu: jnp.where(i < nu[0], j, nj - 1)
    grid_spec = pltpu.PrefetchScalarGridSpec(
        num_scalar_prefetch=3,
        grid=(p // tm, nj),
        in_specs=[pl.BlockSpec(memory_space=pl.ANY),
                  pl.BlockSpec((tm, 1), lambda i, j, te, nu, tok: (i, 0)),
                  pl.BlockSpec((None, None, d, tf), lambda i, j, te, nu, tok: (li, te[i], 0, jj(i, j, nu))),
                  pl.BlockSpec((None, None, d, tf), lambda i, j, te, nu, tok: (li, te[i], 0, jj(i, j, nu))),
                  pl.BlockSpec((None, None, tf, d), lambda i, j, te, nu, tok: (li, te[i], jj(i, j, nu), 0))],
        out_specs=pl.BlockSpec((tm, d), lambda i, j, te, nu, tok: (i, 0)),
        scratch_shapes=[pltpu.VMEM((tm, d), F32), pltpu.VMEM((tm, d), BF16), pltpu.VMEM((tm, d), F32),
                        pltpu.SemaphoreType.DMA(())],
    )
    return pl.pallas_call(
        _moe_kernel,
        grid_spec=grid_spec,
        out_shape=jax.ShapeDtypeStruct((p, d), F32),
        compiler_params=_cparams(("arbitrary", "arbitrary"), row_dma=True),
        name="moe_grouped_swiglu",
    )(tile_e, n_used, src_tok, h2d, roww, w1, w3, w2)


def _combine_kernel(pos_ref, y_hbm, o_ref, g0_s, g1_s, sem):
    i = pl.program_id(0)
    tc = g0_s.shape[0]

    def issue(r, carry):
        t = i * tc + r
        pltpu.make_async_copy(y_hbm.at[pl.ds(pos_ref[2 * t], 1), :], g0_s.at[pl.ds(r, 1), :], sem.at[0]).start()
        pltpu.make_async_copy(y_hbm.at[pl.ds(pos_ref[2 * t + 1], 1), :], g1_s.at[pl.ds(r, 1), :], sem.at[1]).start()
        return carry

    lax.fori_loop(0, tc, issue, 0)
    pltpu.make_async_copy(y_hbm.at[pl.ds(0, tc), :], g0_s, sem.at[0]).wait()
    pltpu.make_async_copy(y_hbm.at[pl.ds(0, tc), :], g1_s, sem.at[1]).wait()
    o_ref[...] = (g0_s[...] + g1_s[...]).astype(o_ref.dtype)


def _moe_combine(y, pos, t):
    d = y.shape[1]
    tc = ROW_BLK
    grid_spec = pltpu.PrefetchScalarGridSpec(
        num_scalar_prefetch=1,
        grid=(t // tc,),
        in_specs=[pl.BlockSpec(memory_space=pl.ANY)],
        out_specs=pl.BlockSpec((tc, d), lambda i, pos: (i, 0)),
        scratch_shapes=[pltpu.VMEM((tc, d), F32), pltpu.VMEM((tc, d), F32), pltpu.SemaphoreType.DMA((2,))],
    )
    return pl.pallas_call(
        _combine_kernel,
        grid_spec=grid_spec,
        out_shape=jax.ShapeDtypeStruct((t, d), BF16),
        compiler_params=_cparams(("arbitrary",), row_dma=True),
        name="moe_combine",
    )(pos, y)


def _moe_plan(route, tm):
    t = route.shape[0]
    e_flat = route[:, 8:10].astype(jnp.int32).reshape(-1)
    w_flat = route[:, 10:12].reshape(-1)
    na = 2 * t
    n_tiles = -(-(na + N_EXPERTS * (tm - 1)) // tm)
    p = n_tiles * tm
    onehot = (e_flat[:, None] == jnp.arange(N_EXPERTS)[None, :]).astype(jnp.int32)
    csum = jnp.cumsum(onehot, axis=0)
    counts = csum[-1]
    rank = jnp.sum((csum - onehot) * onehot, axis=1)
    tiles_e = (counts + tm - 1) // tm
    tile_end = jnp.cumsum(tiles_e)
    tile_start = tile_end - tiles_e
    n_used = tile_end[-1]
    cstart = jnp.cumsum(counts) - counts
    pos_of = (jnp.sum(onehot * tile_start[None, :], axis=1) * tm + rank).astype(jnp.int32)
    _, tok_sorted, w_sorted = lax.sort((pos_of, jnp.arange(na, dtype=jnp.int32) // 2, w_flat), num_keys=1)
    tid = jnp.arange(n_tiles, dtype=jnp.int32)
    e_raw = jnp.sum((tid[:, None] >= tile_end[None, :]).astype(jnp.int32), axis=1)
    tile_e = jnp.minimum(jnp.sum((jnp.minimum(tid, n_used - 1)[:, None] >= tile_end[None, :]).astype(jnp.int32),
                                 axis=1), N_EXPERTS - 1)
    oh_t = (jnp.minimum(e_raw, N_EXPERTS - 1)[:, None] == jnp.arange(N_EXPERTS)[None, :]).astype(jnp.int32)
    t_start = jnp.sum(oh_t * tile_start[None, :], axis=1)
    t_count = jnp.where(tid < n_used, jnp.sum(oh_t * counts[None, :], axis=1), 0)
    t_cstart = jnp.sum(oh_t * cstart[None, :], axis=1)
    local = (tid - t_start)[:, None] * tm + jnp.arange(tm, dtype=jnp.int32)[None, :]
    valid = (local < t_count[:, None]).reshape(p)
    src = jnp.clip(t_cstart[:, None] + local, 0, na - 1).reshape(p)
    src_tok = jnp.where(valid, tok_sorted[src], 0)
    roww = jnp.where(valid, w_sorted[src], 0.0)
    return (tile_e.astype(jnp.int32), n_used.reshape(1).astype(jnp.int32), src_tok, roww.reshape(p, 1)), pos_of


def _moe(h, route, w1, w3, w2, li):
    nb, rows, d = h.shape
    t = nb * rows
    plan, pos_of = _moe_plan(route.reshape(t, LANE), MOE_TM)
    y = _moe_ffn(h.reshape(t, d), plan, w1, w3, w2, li)
    return _moe_combine(y, pos_of, t).reshape(nb, rows, d)


def _rope_tables(seq):
    quarter = RET_DK // 4
    inv_freq = ROPE_BASE ** (-jnp.arange(quarter, dtype=F32) / quarter)
    t = jnp.arange(seq, dtype=jnp.int32)
    ang_r = (t // GRID_W).astype(F32)[:, None] * inv_freq[None]
    ang_c = (t % GRID_W).astype(F32)[:, None] * inv_freq[None]
    ang = jnp.concatenate([ang_r, ang_r, ang_c, ang_c], axis=1)
    rot = np.zeros((RET_DK, RET_DK), np.float32)
    for j in range(RET_DK):
        if (j % (2 * quarter)) < quarter:
            rot[j + quarter, j] = -1.0
        else:
            rot[j - quarter, j] = 1.0
    return jnp.cos(ang), jnp.sin(ang), jnp.asarray(rot, BF16)


def _pack_w_in(w):
    lr = w[..., W_IN_SPLIT:W_IN_SPLIT + 2 * GLA_RANK]
    pad = jnp.zeros(w.shape[:-1] + ((C_GATE - C_GLA_LR) * LANE - 2 * GLA_RANK,), w.dtype)
    return jnp.concatenate([w[..., :W_IN_SPLIT], lr, pad, w[..., W_IN_SPLIT + 2 * GLA_RANK:]], axis=-1).astype(BF16)


def kernel(x, c, ctx, c_ctx, w_ada, b_ada, norms, w_in, ret_decay, gla_w2, gla_b2, na_rpb, w_up, w_out,
           ffn_w1, ffn_w3, ffn_w2, moe_router, moe_router_b, moe_w1, moe_w3, moe_w2):
    nb, seq, d = x.shape
    n_ctx = ctx.shape[1]
    l = n_ctx + seq
    nblk = l // ROW_BLK
    assert n_ctx == ROW_BLK and seq % ROW_BLK == 0 and seq // ROW_BLK >= NA_SLAB and d == D_MODEL and nb < 16

    cvec = jnp.zeros((16, d), F32).at[:nb].set(c).at[nb].set(c_ctx)
    mod = _adaln(cvec, w_ada, b_ada)
    cos, sin, rot = _rope_tables(seq)
    cg, sg, dft = _dft_tables(n_ctx, seq)
    log_gamma = jnp.log1p(-jnp.exp(ret_decay.astype(F32)))

    w_in_b = _pack_w_in(w_in)
    w_up_b, w_out_b = w_up.astype(BF16), w_out.astype(BF16)
    ffn_b = (ffn_w1.astype(BF16), ffn_w3.astype(BF16), ffn_w2.astype(BF16))
    moe_b = (moe_w1.astype(BF16), moe_w3.astype(BF16), moe_w2.astype(BF16))

    xs = jnp.concatenate([ctx, x], axis=1)
    _, h, _ = _resid(xs, None, None, None, mod[0], norms[0], None, x_off=0, f_off=0, nblk=nblk,
                     ctx_first=True, ib=0, sh_k=0, sc_k=1)
    for li in range(DEPTH):
        last = li == DEPTH - 1
        is_moe = li % 2 == 1
        proj2d = _matmul(h.reshape(nb * l, d), w_in_b, li, BF16)
        proj = proj2d.reshape(nb, l, N_PROJ)
        w2p = jnp.zeros((2, LANE, GLA_HEADS * GLA_DK), F32)
        w2p = w2p.at[0, :GLA_RANK].set(gla_w2[li, 0]).at[1, GLA_RANK:2 * GLA_RANK].set(gla_w2[li, 1]).astype(BF16)
        y_ret = _retention(proj, log_gamma[li], cos, sin, rot)
        y_na = _na(proj, _na_bias(na_rpb[li], seq))
        y_fn = _fourier(proj, cg, sg, dft)
        y_gla = _gla(proj, w2p, gla_b2[li].reshape(2, 1, -1).astype(F32))
        ys = [y.reshape(nb * l, BRANCH_W) for y in (y_ret, y_na, y_fn, y_gla)]
        merged = _merge(ys, w_up_b, li, proj2d)
        o = _matmul(merged, w_out_b, li, BF16).reshape(nb, l, d)

        router = None
        if is_moe:
            wr = jnp.zeros((d, LANE), F32).at[:, :N_EXPERTS].set(moe_router[li // 2])
            br = jnp.zeros((1, LANE), F32).at[0, :N_EXPERTS].set(moe_router_b[li // 2])
            router = (wr, br)
        h_dtype = F32 if is_moe else BF16
        if last:
            xs, h2, route = _resid(xs, o, mod[li], norms[li], mod[li], norms[li], router, x_off=1, f_off=1,
                                   nblk=nblk - 1, ctx_first=False, ia=1, gate_k=2, ib=2, sh_k=3, sc_k=4,
                                   h_dtype=h_dtype)
        else:
            xs, h2, route = _resid(xs, o, mod[li], norms[li], mod[li], norms[li], router, x_off=0, f_off=0,
                                   nblk=nblk, ctx_first=True, ia=1, gate_k=2, ib=2, sh_k=3, sc_k=4,
                                   h_dtype=h_dtype)
        rows = h2.shape[1]
        if is_moe:
            f = _moe(h2, route, *moe_b, li // 2)
        else:
            f = _ffn(h2.reshape(nb * rows, d), *ffn_b, li // 2).reshape(nb, rows, d)
        if last:
            xs, _, _ = _resid(xs, f, mod[li], norms[li], None, None, None, x_off=0, f_off=0, nblk=nblk - 1,
                              ctx_first=False, ia=3, gate_k=5)
        else:
            xs, h, _ = _resid(xs, f, mod[li], norms[li], mod[li + 1], norms[li + 1], None, x_off=0, f_off=0,
                              nblk=nblk, ctx_first=True, ia=3, gate_k=5, ib=0, sh_k=0, sc_k=1)
    return xs
```

```python
import functools
import math

import numpy as np
import jax
import jax.numpy as jnp
from jax import lax
from jax.experimental import pallas as pl
from jax.experimental.pallas import tpu as pltpu

F32 = jnp.float32
BF16 = jnp.bfloat16

D_MODEL = 2048
DEPTH = 4
GRID_W = 64
CTX_LEN = 256
EPS = 1e-6
ROPE_BASE = 10000.0
RET_HEADS, RET_DK = 4, 128
NA_HEADS, NA_DH = 4, 128
NA_WIN_R, NA_WIN_C = 8, 16
FNET_GROUPS, FNET_GW = 4, 128
GLA_HEADS, GLA_DK, GLA_DV, GLA_RANK, GLA_TAU = 4, 64, 128, 16, 16.0
GLA_CHUNK = 64
BRANCH_W = 512
N_EXPERTS = 8

LANE = 128
ROW_BLK = 256
VMEM_LIMIT = 56 * 1024 * 1024

C_RET_Q, C_RET_K, C_RET_V, C_RET_G = 0, 4, 8, 12
C_NA_Q, C_NA_K, C_NA_V = 16, 20, 24
C_FU = 28
C_GLA_Q, C_GLA_K, C_GLA_V, C_GLA_G, C_GLA_LR = 32, 34, 36, 40, 44
C_GATE = 48
N_PROJ = (C_GATE + 4 * D_MODEL // LANE) * LANE
W_IN_SPLIT = 5632

NEG = -1e30


def _cparams(sem, row_dma=False):
    return pltpu.CompilerParams(dimension_semantics=sem, vmem_limit_bytes=VMEM_LIMIT,
                                disable_bounds_checks=row_dma)


def _silu(x):
    return x * (1.0 / (1.0 + jnp.exp(-x)))


def _sigmoid(x):
    return 1.0 / (1.0 + jnp.exp(-x))


def _rms(x):
    return x * lax.rsqrt(jnp.mean(x * x, axis=-1, keepdims=True) + EPS)


def _ada_kernel(c_ref, w_ref, b_ref, o_ref):
    s = _silu(c_ref[...]).astype(BF16)
    o_ref[...] = jnp.dot(s, w_ref[...].astype(BF16), preferred_element_type=F32) + b_ref[...]


def _adaln(cvec, w_ada, b_ada):
    depth, d, n = w_ada.shape
    tn = 1024
    return pl.pallas_call(
        _ada_kernel,
        grid=(depth, n // tn),
        in_specs=[pl.BlockSpec((16, d), lambda l, j: (0, 0)),
                  pl.BlockSpec((None, d, tn), lambda l, j: (l, 0, j)),
                  pl.BlockSpec((None, 1, tn), lambda l, j: (l, 0, j))],
        out_specs=pl.BlockSpec((None, 16, tn), lambda l, j: (l, 0, j)),
        out_shape=jax.ShapeDtypeStruct((depth, 16, n), F32),
        compiler_params=_cparams(("arbitrary", "arbitrary")),
        name="adaln",
    )(cvec, w_ada, b_ada.reshape(depth, 1, n))


def _resid_kernel(*refs, nb, ctx_first, has_f, has_next, has_router, ia, gate_k, ib, sh_k, sc_k):
    it = iter(refs)
    x_ref = next(it)
    f_ref = next(it) if has_f else None
    moda_ref = next(it) if has_f else None
    na_ref = next(it) if has_f else None
    modb_ref = next(it) if has_next else None
    nbn_ref = next(it) if has_next else None
    wr_ref = next(it) if has_router else None
    br_ref = next(it) if has_router else None
    xo_ref = next(it) if has_f else None
    h_ref = next(it) if has_next else None
    rt_ref = next(it) if has_router else None

    b = pl.program_id(0)
    j = pl.program_id(1)
    row = jnp.where(j == 0, nb, b) if ctx_first else b
    d = x_ref.shape[-1]

    def modv(ref, k):
        return ref[pl.ds(row, 1), k * d:(k + 1) * d]

    x = x_ref[...]
    if has_f:
        f = f_ref[...].astype(F32)
        x = x + modv(moda_ref, gate_k) * (_rms(f) * na_ref[ia:ia + 1, :])
        xo_ref[...] = x
    if has_next:
        h = _rms(x) * nbn_ref[ib:ib + 1, :]
        h = h * (1.0 + modv(modb_ref, sc_k)) + modv(modb_ref, sh_k)
        h_ref[...] = h.astype(h_ref.dtype)
        if has_router:
            logits = jnp.dot(h, wr_ref[...], preferred_element_type=F32,
                             precision=lax.Precision.HIGHEST) + br_ref[...]
            lane = lax.broadcasted_iota(jnp.int32, logits.shape, 1)
            lg = jnp.where(lane < N_EXPERTS, logits, -jnp.inf)
            v1 = jnp.max(lg, axis=-1, keepdims=True)
            i1 = jnp.min(jnp.where(lg == v1, lane, LANE), axis=-1, keepdims=True)
            lg2 = jnp.where(lane == i1, -jnp.inf, lg)
            v2 = jnp.max(lg2, axis=-1, keepdims=True)
            i2 = jnp.min(jnp.where(lg2 == v2, lane, LANE), axis=-1, keepdims=True)
            e2 = jnp.exp(v2 - v1)
            w1 = 1.0 / (1.0 + e2)
            w2 = e2 / (1.0 + e2)
            zero = jnp.zeros_like(logits)
            rt = (jnp.where(lane == 8, i1.astype(F32), zero) + jnp.where(lane == 9, i2.astype(F32), zero)
                  + jnp.where(lane == 10, w1, zero) + jnp.where(lane == 11, w2, zero))
            rt_ref[...] = rt


def _resid(x, f, mod_a, norms_a, mod_b, norms_b, router, *, x_off, f_off, nblk, ctx_first,
           ia=0, gate_k=0, ib=0, sh_k=0, sc_k=0, h_dtype=BF16):
    nb, _, d = x.shape
    has_f = f is not None
    has_next = mod_b is not None
    has_router = router is not None
    rows = nblk * ROW_BLK
    blk = lambda off: pl.BlockSpec((None, ROW_BLK, d), lambda b, j: (b, j + off, 0))
    full = lambda a: pl.BlockSpec(a.shape, lambda b, j: (0,) * a.ndim)
    ins, specs = [x], [blk(x_off)]
    if has_f:
        ins += [f, mod_a, norms_a]
        specs += [blk(f_off), full(mod_a), full(norms_a)]
    if has_next:
        ins += [mod_b, norms_b]
        specs += [full(mod_b), full(norms_b)]
    if has_router:
        ins += list(router)
        specs += [full(router[0]), full(router[1])]
    outs, ospecs = [], []
    if has_f:
        outs.append(jax.ShapeDtypeStruct((nb, rows, d), F32))
        ospecs.append(blk(0))
    if has_next:
        outs.append(jax.ShapeDtypeStruct((nb, rows, d), h_dtype))
        ospecs.append(blk(0))
    if has_router:
        outs.append(jax.ShapeDtypeStruct((nb, rows, LANE), F32))
        ospecs.append(pl.BlockSpec((None, ROW_BLK, LANE), lambda b, j: (b, j, 0)))
    res = pl.pallas_call(
        functools.partial(_resid_kernel, nb=nb, ctx_first=ctx_first, has_f=has_f, has_next=has_next,
                          has_router=has_router, ia=ia, gate_k=gate_k, ib=ib, sh_k=sh_k, sc_k=sc_k),
        grid=(nb, nblk),
        in_specs=specs, out_specs=ospecs, out_shape=outs,
        compiler_params=_cparams(("arbitrary", "arbitrary")),
        name="resid_norm",
    )(*ins)
    res = list(res)
    x_new = res.pop(0) if has_f else None
    h = res.pop(0) if has_next else None
    rt = res.pop(0) if has_router else None
    return x_new, h, rt


def _mm_kernel(x_ref, w_ref, o_ref):
    o_ref[...] = jnp.dot(x_ref[...], w_ref[...], preferred_element_type=F32).astype(o_ref.dtype)


def _matmul(x, w, li, out_dtype, tm=1024, tn=1024):
    m, k = x.shape
    n = w.shape[2]
    return pl.pallas_call(
        _mm_kernel,
        grid=(m // tm, n // tn),
        in_specs=[pl.BlockSpec((tm, k), lambda i, j: (i, 0)),
                  pl.BlockSpec((None, k, tn), lambda i, j: (li, 0, j))],
        out_specs=pl.BlockSpec((tm, tn), lambda i, j: (i, j)),
        out_shape=jax.ShapeDtypeStruct((m, n), out_dtype),
        compiler_params=_cparams(("arbitrary", "arbitrary")),
        name="matmul",
    )(x, w)


def _ret_kernel(lg_ref, q_ref, k_ref, v_ref, g_ref, cos_ref, sin_ref, rot_ref, o_ref, q_s, k_s, acc_s):
    h = pl.program_id(1)
    lgf = lg_ref[0, h]
    lgb = lg_ref[1, h]
    c = ROW_BLK
    n = q_ref.shape[0] // c
    scale = RET_DK ** -0.5

    q_s[0:c, :] = (q_ref[0:c, :].astype(F32) * scale).astype(BF16)
    k_s[0:c, :] = k_ref[0:c, :]
    rot = rot_ref[...]
    for i in range(1, n):
        rows = slice(i * c, (i + 1) * c)
        trow = slice((i - 1) * c, i * c)
        cs, sn = cos_ref[trow, :], sin_ref[trow, :]
        qa, ka = q_ref[rows, :], k_ref[rows, :]
        qr = jnp.dot(qa, rot, preferred_element_type=F32)
        kr = jnp.dot(ka, rot, preferred_element_type=F32)
        q_s[rows, :] = ((qa.astype(F32) * cs + qr * sn) * scale).astype(BF16)
        k_s[rows, :] = (ka.astype(F32) * cs + kr * sn).astype(BF16)

    ti = lax.broadcasted_iota(jnp.int32, (c, c), 0)
    si = lax.broadcasted_iota(jnp.int32, (c, c), 1)
    dd = (ti - si).astype(F32)
    dmat = jnp.where(dd > 0, jnp.exp(lgf * dd), jnp.where(dd < 0, jnp.exp(-lgb * dd), 2.0))
    tcol = lax.broadcasted_iota(jnp.int32, (c, 1), 0).astype(F32)
    q_f = jnp.exp(lgf * (tcol + 1.0))
    q_b = jnp.exp(lgb * (c - tcol))
    k_f = jnp.exp(lgf * (c - 1.0 - tcol))
    k_b = jnp.exp(lgb * tcol)
    g_f = jnp.exp(lgf * c)
    g_b = jnp.exp(lgb * c)

    def kv_state(kc, vc, kdec):
        kd = (kc.astype(F32) * kdec).astype(BF16)
        return lax.dot_general(kd, vc, (((0,), (0,)), ((), ())), preferred_element_type=F32)

    s_f = jnp.zeros((RET_DK, q_ref.shape[1]), F32)
    for i in range(n):
        rows = slice(i * c, (i + 1) * c)
        qc, kc, vc = q_s[rows, :], k_s[rows, :], v_ref[rows, :]
        sc = lax.dot_general(qc, kc, (((1,), (1,)), ((), ())), preferred_element_type=F32)
        o = jnp.dot((sc * dmat).astype(BF16), vc, preferred_element_type=F32)
        if i > 0:
            qd = (qc.astype(F32) * q_f).astype(BF16)
            o = o + jnp.dot(qd, s_f.astype(BF16), preferred_element_type=F32)
        acc_s[rows, :] = o
        if i < n - 1:
            s_f = g_f * s_f + kv_state(kc, vc, k_f)

    s_b = kv_state(k_s[0:c, :], v_ref[0:c, :], k_b)
    for i in range(n - 1, 0, -1):
        rows = slice(i * c, (i + 1) * c)
        qc, kc, vc = q_s[rows, :], k_s[rows, :], v_ref[rows, :]
        qd = (qc.astype(F32) * q_b).astype(BF16)
        acc_s[rows, :] = acc_s[rows, :] + jnp.dot(qd, s_b.astype(BF16), preferred_element_type=F32)
        if i > 1:
            s_b = g_b * s_b + kv_state(kc, vc, k_b)

    for i in range(n):
        rows = slice(i * c, (i + 1) * c)
        o_ref[rows, :] = (_rms(acc_s[rows, :]) * _silu(g_ref[rows, :].astype(F32))).astype(o_ref.dtype)


def _retention(proj, lg, cos, sin, rot):
    nb, l, _ = proj.shape
    col = lambda c0: pl.BlockSpec((None, l, LANE), lambda b, h: (b, 0, c0 + h))
    const = lambda a: pl.BlockSpec(a.shape, lambda b, h: (0,) * a.ndim)
    return pl.pallas_call(
        _ret_kernel,
        grid=(nb, RET_HEADS),
        in_specs=[pl.BlockSpec(memory_space=pltpu.SMEM),
                  col(C_RET_Q), col(C_RET_K), col(C_RET_V), col(C_RET_G),
                  const(cos), const(sin), const(rot)],
        out_specs=pl.BlockSpec((None, l, LANE), lambda b, h: (b, 0, h)),
        out_shape=jax.ShapeDtypeStruct((nb, l, BRANCH_W), BF16),
        scratch_shapes=[pltpu.VMEM((l, LANE), BF16), pltpu.VMEM((l, LANE), BF16), pltpu.VMEM((l, LANE), F32)],
        compiler_params=_cparams(("arbitrary", "arbitrary")),
        name="retention",
    )(lg, proj, proj, proj, proj, cos, sin, rot)


def _gla_kernel(q_ref, k_ref, v_ref, g_ref, lr_ref, w2_ref, b2_ref, o_ref,
                accf_s, accb_s, st_s, qd_s, ki_s, ke_s, dl_s, *, n_ctx_chunks):
    c = GLA_CHUNK
    l = q_ref.shape[0]
    n = l // c
    cpb = ROW_BLK // c
    kw = GLA_HEADS * GLA_DK
    scale = GLA_DK ** -0.5

    ti = lax.broadcasted_iota(jnp.int32, (ROW_BLK, ROW_BLK), 0)
    si = lax.broadcasted_iota(jnp.int32, (ROW_BLK, ROW_BLK), 1)
    same = ((ti // c) == (si // c)).astype(F32)
    cum_lo = (same * (ti >= si).astype(F32)).astype(BF16)
    cum_up = (same * (ti <= si).astype(F32)).astype(BF16)

    def prep(bi, carry):
        rows = pl.ds(pl.multiple_of(bi * ROW_BLK, ROW_BLK), ROW_BLK)
        lr = lr_ref[rows, :]
        qf = q_ref[rows, :].astype(F32) * scale
        kf = k_ref[rows, :].astype(F32)
        for d, cum, last in ((0, cum_lo, c - 1), (1, cum_up, 0)):
            z = jnp.dot(lr, w2_ref[d], preferred_element_type=F32) + b2_ref[d]
            la = (jnp.minimum(z, 0.0) - jnp.log(1.0 + jnp.exp(-jnp.abs(z)))) * (1.0 / GLA_TAU)
            la_hi = la.astype(BF16)
            la_lo = (la - la_hi.astype(F32)).astype(BF16)
            bcum = (jnp.dot(cum, la_hi, preferred_element_type=F32)
                    + jnp.dot(cum, la_lo, preferred_element_type=F32))
            bl = jnp.concatenate([jnp.broadcast_to(bcum[cc * c + last:cc * c + last + 1, :], (c, kw))
                                  for cc in range(cpb)], axis=0)
            qd_s[d, rows, :] = (qf * jnp.exp(bcum)).astype(BF16)
            ki_s[d, rows, :] = (kf * jnp.exp(-bcum)).astype(BF16)
            ke_s[d, rows, :] = (kf * jnp.exp(bl - bcum)).astype(BF16)
            dec = jnp.exp(bl)
            for cc in range(cpb):
                dl_s[d, pl.ds(pl.multiple_of((bi * cpb + cc) * 8, 8), 8), :] = dec[cc * c:cc * c + 8, :]
        return carry

    lax.fori_loop(0, l // ROW_BLK, prep, 0)

    t4 = lax.broadcasted_iota(jnp.int32, (GLA_HEADS * c, c), 0) % c
    s4 = lax.broadcasted_iota(jnp.int32, (GLA_HEADS * c, c), 1)
    mask_lo = (t4 >= s4).astype(F32)
    mask_up = (t4 <= s4).astype(F32)
    srow = lax.broadcasted_iota(jnp.int32, (GLA_HEADS * c, kw), 0) // c
    slane = lax.broadcasted_iota(jnp.int32, (GLA_HEADS * c, kw), 1) // GLA_DK
    stack_mask = srow == slane
    lane_head = lax.broadcasted_iota(jnp.int32, (GLA_DV, kw), 1) // GLA_DK
    lanes = (((1,), (1,)), ((), ()))
    st_s[...] = jnp.zeros_like(st_s)

    def chunk(ci, d, mask, acc_ref):
        rows = pl.ds(pl.multiple_of(ci * c, c), c)
        q_dec, k_inv, k_end = qd_s[d, rows, :], ki_s[d, rows, :], ke_s[d, rows, :]
        vc = v_ref[rows, :]
        dec = dl_s[d, pl.ds(pl.multiple_of(ci * 8, 8), 1), :]
        q_stack = jnp.where(stack_mask, jnp.concatenate([q_dec] * GLA_HEADS, axis=0), jnp.zeros((), BF16))
        sc = (lax.dot_general(q_stack, k_inv, lanes, preferred_element_type=F32) * mask).astype(BF16)
        st = st_s[d]
        inter = lax.dot_general(q_stack, st.astype(BF16), lanes, preferred_element_type=F32)
        acc_ref[rows, :] = jnp.concatenate(
            [jnp.dot(sc[h * c:(h + 1) * c, :], vc[:, h * GLA_DV:(h + 1) * GLA_DV], preferred_element_type=F32)
             + inter[h * c:(h + 1) * c, :] for h in range(GLA_HEADS)], axis=1)
        full = lax.dot_general(vc, k_end, (((0,), (0,)), ((), ())), preferred_element_type=F32)
        comp = full[0:GLA_DV, :]
        for h in range(1, GLA_HEADS):
            comp = jnp.where(lane_head == h, full[h * GLA_DV:(h + 1) * GLA_DV, :], comp)
        st_s[d] = st * dec + comp

    def step(i, carry):
        chunk(i, 0, mask_lo, accf_s)
        chunk(jnp.where(i < n_ctx_chunks, n_ctx_chunks - 1 - i, n + n_ctx_chunks - 1 - i), 1, mask_up, accb_s)
        return carry

    lax.fori_loop(0, n, step, 0, unroll=2)

    for i in range(l // ROW_BLK):
        rows = slice(i * ROW_BLK, (i + 1) * ROW_BLK)
        o = accf_s[rows, :] + accb_s[rows, :]
        o = jnp.concatenate([_rms(o[:, h * GLA_DV:(h + 1) * GLA_DV]) for h in range(GLA_HEADS)], axis=1)
        o_ref[rows, :] = (o * _silu(g_ref[rows, :].astype(F32))).astype(o_ref.dtype)


def _gla(proj, w2p, b2):
    nb, l, _ = proj.shape
    kw, vw = GLA_HEADS * GLA_DK, GLA_HEADS * GLA_DV
    col = lambda c0, w: pl.BlockSpec((None, l, w), lambda b: (b, 0, c0 * LANE // w))
    const = lambda a: pl.BlockSpec(a.shape, lambda b: (0,) * a.ndim)
    return pl.pallas_call(
        functools.partial(_gla_kernel, n_ctx_chunks=CTX_LEN // GLA_CHUNK),
        grid=(nb,),
        in_specs=[col(C_GLA_Q, kw), col(C_GLA_K, kw), col(C_GLA_V, vw), col(C_GLA_G, vw), col(C_GLA_LR, LANE),
                  const(w2p), const(b2)],
        out_specs=pl.BlockSpec((None, l, vw), lambda b: (b, 0, 0)),
        out_shape=jax.ShapeDtypeStruct((nb, l, BRANCH_W), BF16),
        scratch_shapes=[pltpu.VMEM((l, vw), F32), pltpu.VMEM((l, vw), F32),
                        pltpu.VMEM((2, GLA_DV, kw), F32),
                        pltpu.VMEM((2, l, kw), BF16), pltpu.VMEM((2, l, kw), BF16), pltpu.VMEM((2, l, kw), BF16),
                        pltpu.VMEM((2, l // GLA_CHUNK * 8, kw), F32)],
        compiler_params=_cparams(("arbitrary",)),
        name="gla",
    )(proj, proj, proj, proj, proj, w2p, b2)


NA_QROWS = ROW_BLK // GRID_W
NA_SLAB = 3


def _na_kernel(q_ref, kc_ref, k0_ref, k1_ref, k2_ref, vc_ref, v0_ref, v1_ref, v2_ref, bias_ref, o_ref):
    scale = NA_DH ** -0.5
    dims = (((1,), (1,)), ((), ()))
    for h in range(NA_HEADS):
        cols = slice(h * NA_DH, (h + 1) * NA_DH)
        q = q_ref[:, cols]
        s = [lax.dot_general(q, kc_ref[:, cols], dims, preferred_element_type=F32) * scale]
        for j, kr in enumerate((k0_ref, k1_ref, k2_ref)):
            sj = lax.dot_general(q, kr[:, cols], dims, preferred_element_type=F32) * scale
            s.append(sj + bias_ref[h, :, j * ROW_BLK:(j + 1) * ROW_BLK])
        m = functools.reduce(jnp.maximum, [jnp.max(x, axis=-1, keepdims=True) for x in s])
        p = [jnp.exp(x - m) for x in s]
        den = functools.reduce(lambda a, b: a + b, [jnp.sum(x, axis=-1, keepdims=True) for x in p])
        o = jnp.zeros((q.shape[0], NA_DH), F32)
        for pj, vr in zip(p, (vc_ref, v0_ref, v1_ref, v2_ref)):
            o = o + jnp.dot(pj.astype(BF16), vr[:, cols], preferred_element_type=F32)
        o_ref[:, cols] = (o / den).astype(o_ref.dtype)


def _na_slab_start(qb, n_lat_blk):
    return jnp.clip(qb - 2, 0, n_lat_blk - NA_SLAB)


def _na(proj, bias):
    nb, l, _ = proj.shape
    nblk = l // ROW_BLK
    nlat = nblk - 1
    w = NA_HEADS * NA_DH
    blk = lambda c0, rowfn: pl.BlockSpec((None, ROW_BLK, w), lambda qb, b: (b, rowfn(qb), c0 * LANE // w))
    slab = lambda j: (lambda qb: 1 + _na_slab_start(qb, nlat) + j)
    return pl.pallas_call(
        _na_kernel,
        grid=(nblk, nb),
        in_specs=[blk(C_NA_Q, lambda qb: qb),
                  blk(C_NA_K, lambda qb: 0), blk(C_NA_K, slab(0)), blk(C_NA_K, slab(1)), blk(C_NA_K, slab(2)),
                  blk(C_NA_V, lambda qb: 0), blk(C_NA_V, slab(0)), blk(C_NA_V, slab(1)), blk(C_NA_V, slab(2)),
                  pl.BlockSpec((NA_HEADS, None, ROW_BLK, NA_SLAB * ROW_BLK), lambda qb, b: (0, qb, 0, 0))],
        out_specs=pl.BlockSpec((None, ROW_BLK, w), lambda qb, b: (b, qb, 0)),
        out_shape=jax.ShapeDtypeStruct((nb, l, BRANCH_W), BF16),
        compiler_params=_cparams(("arbitrary", "arbitrary")),
        name="neighbourhood_attention",
    )(*([proj] * 9), bias)


def _na_bias(rpb, seq):
    rows = seq // GRID_W
    kr = min(NA_WIN_R, rows)
    nlat = seq // ROW_BLK
    cq = np.arange(GRID_W)[:, None]
    ck = np.arange(GRID_W)[None, :]
    win_start = np.clip(cq - NA_WIN_C // 2, 0, GRID_W - NA_WIN_C)
    in_win = (ck >= win_start) & (ck < win_start + NA_WIN_C)
    rel_c = np.clip(ck - cq, 1 - NA_WIN_C, NA_WIN_C - 1) + NA_WIN_C - 1
    tiles = jnp.where(in_win[None, None], rpb[:, :, rel_c].astype(F32), NEG)
    masked = jnp.full((rpb.shape[0], GRID_W, GRID_W), NEG, F32)
    blocks = [jnp.full((rpb.shape[0], ROW_BLK, NA_SLAB * ROW_BLK), NEG, F32)]
    for qb in range(nlat):
        s0 = int(np.clip(qb - 1, 0, nlat - NA_SLAB)) * NA_QROWS
        qrows = []
        for qr in range(NA_QROWS):
            r = qb * NA_QROWS + qr
            k0 = int(np.clip(r - kr // 2, 0, rows - kr))
            assert s0 <= k0 and k0 + kr <= s0 + NA_SLAB * NA_QROWS
            krows = []
            for kk in range(NA_SLAB * NA_QROWS):
                krow = s0 + kk
                if k0 <= krow < k0 + kr:
                    krows.append(tiles[:, krow - r + NA_WIN_R - 1])
                else:
                    krows.append(masked)
            qrows.append(jnp.concatenate(krows, axis=2))
        blocks.append(jnp.concatenate(qrows, axis=1))
    return jnp.stack(blocks, axis=1)


def _fourier_kernel(u_ref, cg_ref, sg_ref, dft_ref, o_ref, ab_s):
    l = u_ref.shape[0]

    @pl.when(pl.program_id(1) == 0)
    def _():
        for g in range(FNET_GROUPS):
            cols = slice(g * FNET_GW, (g + 1) * FNET_GW)
            u = u_ref[:, cols]
            ab_s[0:l, cols] = jnp.dot(u, cg_ref[...], preferred_element_type=F32).astype(BF16)
            ab_s[l:2 * l, cols] = jnp.dot(u, sg_ref[...], preferred_element_type=F32).astype(BF16)

    o_ref[...] = jnp.dot(dft_ref[...], ab_s[...], preferred_element_type=F32).astype(o_ref.dtype)


def _fourier(proj, cg, sg, dft):
    nb, l, _ = proj.shape
    tr = 3 * ROW_BLK
    w = FNET_GROUPS * FNET_GW
    return pl.pallas_call(
        _fourier_kernel,
        grid=(nb, l // tr),
        in_specs=[pl.BlockSpec((None, l, w), lambda b, i: (b, 0, C_FU * LANE // w)),
                  pl.BlockSpec(cg.shape, lambda b, i: (0, 0)),
                  pl.BlockSpec(sg.shape, lambda b, i: (0, 0)),
                  pl.BlockSpec((tr, 2 * l), lambda b, i: (i, 0))],
        out_specs=pl.BlockSpec((None, tr, w), lambda b, i: (b, i, 0)),
        out_shape=jax.ShapeDtypeStruct((nb, l, BRANCH_W), BF16),
        scratch_shapes=[pltpu.VMEM((2 * l, w), BF16)],
        compiler_params=_cparams(("arbitrary", "arbitrary")),
        name="fourier_mix",
    )(proj, cg, sg, dft)


def _dft_tables(n_ctx, seq):
    def cs(n):
        jk = (np.arange(n)[:, None] * np.arange(n)[None, :]) % n
        ang = 2.0 * np.pi * jk / n
        return np.cos(ang) / np.sqrt(n), np.sin(ang) / np.sqrt(n)

    cg, sg = cs(FNET_GW)
    l = n_ctx + seq
    cl = np.zeros((l, l))
    sl = np.zeros((l, l))
    cc, sc = cs(n_ctx)
    cs_, ss_ = cs(seq)
    cl[:n_ctx, :n_ctx], sl[:n_ctx, :n_ctx] = cc, sc
    cl[n_ctx:, n_ctx:], sl[n_ctx:, n_ctx:] = cs_, ss_
    dft = np.concatenate([cl, -sl], axis=1)
    return (jnp.asarray(cg, BF16), jnp.asarray(sg, BF16), jnp.asarray(dft, BF16))


def _merge_kernel(y0_ref, y1_ref, y2_ref, y3_ref, w_ref, g0_ref, g1_ref, g2_ref, g3_ref, o_ref):
    acc = None
    for i, (y_ref, g_ref) in enumerate(zip((y0_ref, y1_ref, y2_ref, y3_ref), (g0_ref, g1_ref, g2_ref, g3_ref))):
        t = _sigmoid(g_ref[...].astype(F32)) * jnp.dot(y_ref[...], w_ref[i], preferred_element_type=F32)
        acc = t if acc is None else acc + t
    o_ref[...] = acc.astype(o_ref.dtype)


def _merge(ys, w_up, li, proj2d, tm=1024, tn=512):
    t = proj2d.shape[0]
    d = w_up.shape[3]
    gate0 = C_GATE * LANE // tn
    yspec = pl.BlockSpec((tm, BRANCH_W), lambda i, j: (i, 0))
    gspec = lambda br: pl.BlockSpec((tm, tn), lambda i, j: (i, gate0 + br * (d // tn) + j))
    return pl.pallas_call(
        _merge_kernel,
        grid=(t // tm, d // tn),
        in_specs=[yspec] * 4 + [pl.BlockSpec((None, 4, BRANCH_W, tn), lambda i, j: (li, 0, 0, j))]
        + [gspec(br) for br in range(4)],
        out_specs=pl.BlockSpec((tm, tn), lambda i, j: (i, j)),
        out_shape=jax.ShapeDtypeStruct((t, d), BF16),
        compiler_params=_cparams(("arbitrary", "arbitrary")),
        name="gated_merge",
    )(*ys, w_up, proj2d, proj2d, proj2d, proj2d)


def _ffn_kernel(x_ref, w1_ref, w3_ref, w2_ref, o_ref, acc_s):
    j = pl.program_id(1)
    x = x_ref[...]
    a = jnp.dot(x, w1_ref[...], preferred_element_type=F32)
    b = jnp.dot(x, w3_ref[...], preferred_element_type=F32)
    g = (_silu(a) * b).astype(BF16)
    part = jnp.dot(g, w2_ref[...], preferred_element_type=F32)

    @pl.when(j == 0)
    def _():
        acc_s[...] = part

    @pl.when(j > 0)
    def _():
        acc_s[...] += part

    @pl.when(j == pl.num_programs(1) - 1)
    def _():
        o_ref[...] = acc_s[...].astype(o_ref.dtype)


def _ffn(x, w1, w3, w2, li, tm=1024, tf=512):
    t, d = x.shape
    ff = w1.shape[2]
    return pl.pallas_call(
        _ffn_kernel,
        grid=(t // tm, ff // tf),
        in_specs=[pl.BlockSpec((tm, d), lambda i, j: (i, 0)),
                  pl.BlockSpec((None, d, tf), lambda i, j: (li, 0, j)),
                  pl.BlockSpec((None, d, tf), lambda i, j: (li, 0, j)),
                  pl.BlockSpec((None, tf, d), lambda i, j: (li, j, 0))],
        out_specs=pl.BlockSpec((tm, d), lambda i, j: (i, 0)),
        out_shape=jax.ShapeDtypeStruct((t, d), BF16),
        scratch_shapes=[pltpu.VMEM((tm, d), F32)],
        compiler_params=_cparams(("arbitrary", "arbitrary")),
        name="swiglu",
    )(x, w1, w3, w2)


MOE_TM = 512
MOE_TF = 1024
MOE_TN = 512


def _dispatch_kernel(nu_ref, tok_ref, h_hbm, o_ref, xg_s, sem):
    i = pl.program_id(0)
    tm = xg_s.shape[0]
    used = i < nu_ref[0]

    @pl.when(used)
    def _():
        def issue(r, carry):
            tok = tok_ref[i * tm + r]
            pltpu.make_async_copy(h_hbm.at[pl.ds(tok, 1), :], xg_s.at[pl.ds(r, 1), :], sem).start()
            return carry

        lax.fori_loop(0, tm, issue, 0, unroll=8)
        pltpu.make_async_copy(h_hbm.at[pl.ds(0, tm), :], xg_s, sem).wait()
        o_ref[...] = xg_s[...].astype(o_ref.dtype)

    @pl.when(jnp.logical_not(used))
    def _():
        o_ref[...] = jnp.zeros_like(o_ref)


def _moe_dispatch(h2d, n_used, src_tok):
    p = src_tok.shape[0]
    d = h2d.shape[1]
    tm = MOE_TM
    grid_spec = pltpu.PrefetchScalarGridSpec(
        num_scalar_prefetch=2,
        grid=(p // tm,),
        in_specs=[pl.BlockSpec(memory_space=pl.ANY)],
        out_specs=pl.BlockSpec((tm, d), lambda i, nu, tok: (i, 0)),
        scratch_shapes=[pltpu.VMEM((tm, d), F32), pltpu.SemaphoreType.DMA(())],
    )
    return pl.pallas_call(
        _dispatch_kernel,
        grid_spec=grid_spec,
        out_shape=jax.ShapeDtypeStruct((p, d), BF16),
        compiler_params=_cparams(("arbitrary",), row_dma=True),
        name="moe_dispatch",
    )(n_used, src_tok, h2d)


def _expert_changed(te_ref, i):
    return jnp.logical_or(i == 0, te_ref[i] != te_ref[jnp.maximum(i - 1, 0)])


def _moe_up_kernel(te_ref, nu_ref, x_ref, w1_ref, w3_ref, o_ref, w1b_s, w3b_s):
    i = pl.program_id(1)

    @pl.when(_expert_changed(te_ref, i))
    def _():
        w1b_s[...] = w1_ref[...].astype(BF16)
        w3b_s[...] = w3_ref[...].astype(BF16)

    @pl.when(i < nu_ref[0])
    def _():
        x = x_ref[...]
        a = jnp.dot(x, w1b_s[...], preferred_element_type=F32)
        b = jnp.dot(x, w3b_s[...], preferred_element_type=F32)
        o_ref[...] = (_silu(a) * b).astype(o_ref.dtype)

    @pl.when(i >= nu_ref[0])
    def _():
        o_ref[...] = jnp.zeros_like(o_ref)


def _moe_down_kernel(te_ref, nu_ref, g_ref, w2_ref, o_ref, w2b_s):
    i = pl.program_id(1)

    @pl.when(_expert_changed(te_ref, i))
    def _():
        w2b_s[...] = w2_ref[...].astype(BF16)

    @pl.when(i < nu_ref[0])
    def _():
        o_ref[...] = jnp.dot(g_ref[...], w2b_s[...], preferred_element_type=F32)

    @pl.when(i >= nu_ref[0])
    def _():
        o_ref[...] = jnp.zeros_like(o_ref)


def _moe_ffn(xs, tile_e, n_used, w1, w3, w2, li):
    p, d = xs.shape
    ff = w1.shape[3]
    tm, tf, tn = MOE_TM, MOE_TF, MOE_TN
    up_spec = pltpu.PrefetchScalarGridSpec(
        num_scalar_prefetch=2,
        grid=(ff // tf, p // tm),
        in_specs=[pl.BlockSpec((tm, d), lambda j, i, te, nu: (i, 0)),
                  pl.BlockSpec((None, None, d, tf), lambda j, i, te, nu: (li, te[i], 0, j)),
                  pl.BlockSpec((None, None, d, tf), lambda j, i, te, nu: (li, te[i], 0, j))],
        out_specs=pl.BlockSpec((tm, tf), lambda j, i, te, nu: (i, j)),
        scratch_shapes=[pltpu.VMEM((d, tf), BF16), pltpu.VMEM((d, tf), BF16)],
    )
    g = pl.pallas_call(
        _moe_up_kernel,
        grid_spec=up_spec,
        out_shape=jax.ShapeDtypeStruct((p, ff), BF16),
        compiler_params=_cparams(("arbitrary", "arbitrary")),
        name="moe_up",
    )(tile_e, n_used, xs, w1, w3)
    down_spec = pltpu.PrefetchScalarGridSpec(
        num_scalar_prefetch=2,
        grid=(d // tn, p // tm),
        in_specs=[pl.BlockSpec((tm, ff), lambda j, i, te, nu: (i, 0)),
                  pl.BlockSpec((None, None, ff, tn), lambda j, i, te, nu: (li, te[i], 0, j))],
        out_specs=pl.BlockSpec((tm, tn), lambda j, i, te, nu: (i, j)),
        scratch_shapes=[pltpu.VMEM((ff, tn), BF16)],
    )
    return pl.pallas_call(
        _moe_down_kernel,
        grid_spec=down_spec,
        out_shape=jax.ShapeDtypeStruct((p, d), F32),
        compiler_params=_cparams(("arbitrary", "arbitrary")),
        name="moe_down",
    )(tile_e, n_used, g, w2)


def _combine_kernel(pos_ref, y_hbm, rt_ref, o_ref, g0_s, g1_s, sem):
    i = pl.program_id(0)
    tc = g0_s.shape[0]

    def issue(r, carry):
        t = i * tc + r
        pltpu.make_async_copy(y_hbm.at[pl.ds(pos_ref[2 * t], 1), :], g0_s.at[pl.ds(r, 1), :], sem.at[0]).start()
        pltpu.make_async_copy(y_hbm.at[pl.ds(pos_ref[2 * t + 1], 1), :], g1_s.at[pl.ds(r, 1), :], sem.at[1]).start()
        return carry

    lax.fori_loop(0, tc, issue, 0, unroll=8)
    pltpu.make_async_copy(y_hbm.at[pl.ds(0, tc), :], g0_s, sem.at[0]).wait()
    pltpu.make_async_copy(y_hbm.at[pl.ds(0, tc), :], g1_s, sem.at[1]).wait()
    w0 = rt_ref[:, 10:11]
    w1 = rt_ref[:, 11:12]
    o_ref[...] = (w0 * g0_s[...] + w1 * g1_s[...]).astype(o_ref.dtype)


def _moe_combine(y, pos, route2d):
    t = route2d.shape[0]
    d = y.shape[1]
    tc = ROW_BLK
    grid_spec = pltpu.PrefetchScalarGridSpec(
        num_scalar_prefetch=1,
        grid=(t // tc,),
        in_specs=[pl.BlockSpec(memory_space=pl.ANY),
                  pl.BlockSpec((tc, LANE), lambda i, pos: (i, 0))],
        out_specs=pl.BlockSpec((tc, d), lambda i, pos: (i, 0)),
        scratch_shapes=[pltpu.VMEM((tc, d), F32), pltpu.VMEM((tc, d), F32), pltpu.SemaphoreType.DMA((2,))],
    )
    return pl.pallas_call(
        _combine_kernel,
        grid_spec=grid_spec,
        out_shape=jax.ShapeDtypeStruct((t, d), BF16),
        compiler_params=_cparams(("arbitrary",), row_dma=True),
        name="moe_combine",
    )(pos, y, route2d)


def _moe_plan(route, tm):
    t = route.shape[0]
    e_flat = route[:, 8:10].astype(jnp.int32).reshape(-1)
    na = 2 * t
    n_tiles = -(-(na + N_EXPERTS * (tm - 1)) // tm)
    p = n_tiles * tm
    onehot = (e_flat[:, None] == jnp.arange(N_EXPERTS)[None, :]).astype(jnp.int32)
    csum = jnp.cumsum(onehot, axis=0)
    counts = csum[-1]
    rank = jnp.sum((csum - onehot) * onehot, axis=1)
    tiles_e = (counts + tm - 1) // tm
    tile_end = jnp.cumsum(tiles_e)
    tile_start = tile_end - tiles_e
    n_used = tile_end[-1]
    cstart = jnp.cumsum(counts) - counts
    pos_of = (jnp.sum(onehot * tile_start[None, :], axis=1) * tm + rank).astype(jnp.int32)
    _, tok_sorted = lax.sort((pos_of, jnp.arange(na, dtype=jnp.int32) // 2), num_keys=1)
    tid = jnp.arange(n_tiles, dtype=jnp.int32)
    e_raw = jnp.sum((tid[:, None] >= tile_end[None, :]).astype(jnp.int32), axis=1)
    tile_e = jnp.minimum(jnp.sum((jnp.minimum(tid, n_used - 1)[:, None] >= tile_end[None, :]).astype(jnp.int32),
                                 axis=1), N_EXPERTS - 1)
    oh_t = (jnp.minimum(e_raw, N_EXPERTS - 1)[:, None] == jnp.arange(N_EXPERTS)[None, :]).astype(jnp.int32)
    t_start = jnp.sum(oh_t * tile_start[None, :], axis=1)
    t_count = jnp.where(tid < n_used, jnp.sum(oh_t * counts[None, :], axis=1), 0)
    t_cstart = jnp.sum(oh_t * cstart[None, :], axis=1)
    local = (tid - t_start)[:, None] * tm + jnp.arange(tm, dtype=jnp.int32)[None, :]
    valid = (local < t_count[:, None]).reshape(p)
    src = jnp.clip(t_cstart[:, None] + local, 0, na - 1).reshape(p)
    src_tok = jnp.where(valid, tok_sorted[src], 0)
    return tile_e.astype(jnp.int32), n_used.reshape(1).astype(jnp.int32), src_tok, pos_of


def _moe(h, route, w1, w3, w2, li):
    nb, rows, d = h.shape
    t = nb * rows
    route2d = route.reshape(t, LANE)
    tile_e, n_used, src_tok, pos_of = _moe_plan(route2d, MOE_TM)
    xs = _moe_dispatch(h.reshape(t, d), n_used, src_tok)
    y = _moe_ffn(xs, tile_e, n_used, w1, w3, w2, li)
    return _moe_combine(y, pos_of, route2d).reshape(nb, rows, d)


def _rope_tables(seq):
    quarter = RET_DK // 4
    inv_freq = ROPE_BASE ** (-jnp.arange(quarter, dtype=F32) / quarter)
    t = jnp.arange(seq, dtype=jnp.int32)
    ang_r = (t // GRID_W).astype(F32)[:, None] * inv_freq[None]
    ang_c = (t % GRID_W).astype(F32)[:, None] * inv_freq[None]
    ang = jnp.concatenate([ang_r, ang_r, ang_c, ang_c], axis=1)
    rot = np.zeros((RET_DK, RET_DK), np.float32)
    for j in range(RET_DK):
        if (j % (2 * quarter)) < quarter:
            rot[j + quarter, j] = -1.0
        else:
            rot[j - quarter, j] = 1.0
    return jnp.cos(ang), jnp.sin(ang), jnp.asarray(rot, BF16)


PACK_TN = 512


def _pack_kernel(a_ref, b_ref, o_ref):
    blk = pl.program_id(1)
    main_blks = W_IN_SPLIT // PACK_TN
    sh = 2 * GLA_RANK

    @pl.when(blk < main_blks)
    def _():
        o_ref[...] = a_ref[...].astype(BF16)

    @pl.when(blk == main_blks)
    def _():
        lane = lax.broadcasted_iota(jnp.int32, a_ref.shape, 1)
        o_ref[...] = jnp.where(lane < sh, a_ref[...], 0.0).astype(BF16)

    @pl.when(blk > main_blks)
    def _():
        o_ref[...] = jnp.concatenate([a_ref[:, sh:], b_ref[:, :sh]], axis=1).astype(BF16)


def _pack_w_in(w):
    depth, d, n_src = w.shape
    main_blks = W_IN_SPLIT // PACK_TN
    a_idx = lambda j: jnp.where(j <= main_blks, j, j - 1)
    last_b = (n_src - 1) // LANE
    return pl.pallas_call(
        _pack_kernel,
        grid=(depth, N_PROJ // PACK_TN),
        in_specs=[pl.BlockSpec((None, d, PACK_TN), lambda l, j: (l, 0, a_idx(j))),
                  pl.BlockSpec((None, d, LANE),
                               lambda l, j: (l, 0, jnp.minimum((a_idx(j) + 1) * (PACK_TN // LANE), last_b)))],
        out_specs=pl.BlockSpec((None, d, PACK_TN), lambda l, j: (l, 0, j)),
        out_shape=jax.ShapeDtypeStruct((depth, d, N_PROJ), BF16),
        compiler_params=_cparams(("arbitrary", "arbitrary")),
        name="pack_w_in",
    )(w, w)


def kernel(x, c, ctx, c_ctx, w_ada, b_ada, norms, w_in, ret_decay, gla_w2, gla_b2, na_rpb, w_up, w_out,
           ffn_w1, ffn_w3, ffn_w2, moe_router, moe_router_b, moe_w1, moe_w3, moe_w2):
    nb, seq, d = x.shape
    n_ctx = ctx.shape[1]
    l = n_ctx + seq
    nblk = l // ROW_BLK
    assert n_ctx == ROW_BLK and seq % ROW_BLK == 0 and seq // ROW_BLK >= NA_SLAB and d == D_MODEL and nb < 16

    cvec = jnp.zeros((16, d), F32).at[:nb].set(c).at[nb].set(c_ctx)
    mod = _adaln(cvec, w_ada, b_ada)
    cos, sin, rot = _rope_tables(seq)
    cg, sg, dft = _dft_tables(n_ctx, seq)
    log_gamma = jnp.log1p(-jnp.exp(ret_decay.astype(F32)))

    w_in_b = _pack_w_in(w_in)
    w_up_b, w_out_b = w_up.astype(BF16), w_out.astype(BF16)
    ffn_b = (ffn_w1.astype(BF16), ffn_w3.astype(BF16), ffn_w2.astype(BF16))

    xs = jnp.concatenate([ctx, x], axis=1)
    _, h, _ = _resid(xs, None, None, None, mod[0], norms[0], None, x_off=0, f_off=0, nblk=nblk,
                     ctx_first=True, ib=0, sh_k=0, sc_k=1)
    for li in range(DEPTH):
        last = li == DEPTH - 1
        is_moe = li % 2 == 1
        proj2d = _matmul(h.reshape(nb * l, d), w_in_b, li, BF16)
        proj = proj2d.reshape(nb, l, N_PROJ)
        w2p = jnp.zeros((2, LANE, GLA_HEADS * GLA_DK), F32)
        w2p = w2p.at[0, :GLA_RANK].set(gla_w2[li, 0]).at[1, GLA_RANK:2 * GLA_RANK].set(gla_w2[li, 1]).astype(BF16)
        y_ret = _retention(proj, log_gamma[li], cos, sin, rot)
        y_na = _na(proj, _na_bias(na_rpb[li], seq))
        y_fn = _fourier(proj, cg, sg, dft)
        y_gla = _gla(proj, w2p, gla_b2[li].reshape(2, 1, -1).astype(F32))
        ys = [y.reshape(nb * l, BRANCH_W) for y in (y_ret, y_na, y_fn, y_gla)]
        merged = _merge(ys, w_up_b, li, proj2d)
        o = _matmul(merged, w_out_b, li, BF16).reshape(nb, l, d)

        router = None
        if is_moe:
            wr = jnp.zeros((d, LANE), F32).at[:, :N_EXPERTS].set(moe_router[li // 2])
            br = jnp.zeros((1, LANE), F32).at[0, :N_EXPERTS].set(moe_router_b[li // 2])
            router = (wr, br)
        h_dtype = F32 if is_moe else BF16
        if last:
            xs, h2, route = _resid(xs, o, mod[li], norms[li], mod[li], norms[li], router, x_off=1, f_off=1,
                                   nblk=nblk - 1, ctx_first=False, ia=1, gate_k=2, ib=2, sh_k=3, sc_k=4,
                                   h_dtype=h_dtype)
        else:
            xs, h2, route = _resid(xs, o, mod[li], norms[li], mod[li], norms[li], router, x_off=0, f_off=0,
                                   nblk=nblk, ctx_first=True, ia=1, gate_k=2, ib=2, sh_k=3, sc_k=4,
                                   h_dtype=h_dtype)
        rows = h2.shape[1]
        if is_moe:
            f = _moe(h2, route, moe_w1, moe_w3, moe_w2, li // 2)
        else:
            f = _ffn(h2.reshape(nb * rows, d), *ffn_b, li // 2).reshape(nb, rows, d)
        if last:
            xs, _, _ = _resid(xs, f, mod[li], norms[li], None, None, None, x_off=0, f_off=0, nblk=nblk - 1,
                              ctx_first=False, ia=3, gate_k=5)
        else:
            xs, h, _ = _resid(xs, f, mod[li], norms[li], mod[li + 1], norms[li + 1], None, x_off=0, f_off=0,
                              nblk=nblk, ctx_first=True, ia=3, gate_k=5, ib=0, sh_k=0, sc_k=1)
    return xs
```

```python
import functools
import math

import numpy as np
import jax
import jax.numpy as jnp
from jax import lax
from jax.experimental import pallas as pl
from jax.experimental.pallas import tpu as pltpu

F32 = jnp.float32
BF16 = jnp.bfloat16

D_MODEL = 2048
DEPTH = 4
GRID_W = 64
CTX_LEN = 256
EPS = 1e-6
ROPE_BASE = 10000.0
RET_HEADS, RET_DK = 4, 128
NA_HEADS, NA_DH = 4, 128
NA_WIN_R, NA_WIN_C = 8, 16
FNET_GROUPS, FNET_GW = 4, 128
GLA_HEADS, GLA_DK, GLA_DV, GLA_RANK, GLA_TAU = 4, 64, 128, 16, 16.0
GLA_CHUNK = 64
BRANCH_W = 512
N_EXPERTS = 8

LANE = 128
ROW_BLK = 256
VMEM_LIMIT = 56 * 1024 * 1024

C_RET_Q, C_RET_K, C_RET_V, C_RET_G = 0, 4, 8, 12
C_NA_Q, C_NA_K, C_NA_V = 16, 20, 24
C_FU = 28
C_GLA_Q, C_GLA_K, C_GLA_V, C_GLA_G, C_GLA_LR = 32, 34, 36, 40, 44
C_GATE = 48
N_PROJ = (C_GATE + 4 * D_MODEL // LANE) * LANE
W_IN_SPLIT = 5632

NEG = -1e30


def _cparams(sem, row_dma=False):
    return pltpu.CompilerParams(dimension_semantics=sem, vmem_limit_bytes=VMEM_LIMIT,
                                disable_bounds_checks=row_dma)


def _silu(x):
    return x * (1.0 / (1.0 + jnp.exp(-x)))


def _sigmoid(x):
    return 1.0 / (1.0 + jnp.exp(-x))


def _rms(x):
    return x * lax.rsqrt(jnp.mean(x * x, axis=-1, keepdims=True) + EPS)


HI_MASK = 0xFFFF0000


def _pack_halves(x):
    n = x.shape[1] // 2
    lo = pltpu.bitcast(x[:, :n].astype(BF16).astype(F32), jnp.uint32)
    hi = pltpu.bitcast(x[:, n:].astype(BF16).astype(F32), jnp.uint32)
    return (lo >> 16) | (hi & jnp.uint32(HI_MASK))


def _unpack_halves(p):
    return pltpu.bitcast(p << 16, F32), pltpu.bitcast(p & jnp.uint32(HI_MASK), F32)


def _ada_kernel(c_ref, w_ref, b_ref, o_ref):
    s = _silu(c_ref[...]).astype(BF16)
    o_ref[...] = jnp.dot(s, w_ref[...].astype(BF16), preferred_element_type=F32) + b_ref[...]


def _adaln(cvec, w_ada, b_ada):
    depth, d, n = w_ada.shape
    tn = 1024
    return pl.pallas_call(
        _ada_kernel,
        grid=(depth, n // tn),
        in_specs=[pl.BlockSpec((16, d), lambda l, j: (0, 0)),
                  pl.BlockSpec((None, d, tn), lambda l, j: (l, 0, j)),
                  pl.BlockSpec((None, 1, tn), lambda l, j: (l, 0, j))],
        out_specs=pl.BlockSpec((None, 16, tn), lambda l, j: (l, 0, j)),
        out_shape=jax.ShapeDtypeStruct((depth, 16, n), F32),
        compiler_params=_cparams(("arbitrary", "arbitrary")),
        name="adaln",
    )(cvec, w_ada, b_ada.reshape(depth, 1, n))


def _resid_kernel(*refs, nb, ctx_first, has_f, has_next, has_router, ia, gate_k, ib, sh_k, sc_k):
    it = iter(refs)
    x_ref = next(it)
    f_ref = next(it) if has_f else None
    moda_ref = next(it) if has_f else None
    na_ref = next(it) if has_f else None
    modb_ref = next(it) if has_next else None
    nbn_ref = next(it) if has_next else None
    wr_ref = next(it) if has_router else None
    br_ref = next(it) if has_router else None
    xo_ref = next(it) if has_f else None
    h_ref = next(it) if has_next else None
    rt_ref = next(it) if has_router else None

    b = pl.program_id(0)
    j = pl.program_id(1)
    row = jnp.where(j == 0, nb, b) if ctx_first else b
    d = x_ref.shape[-1]

    def modv(ref, k):
        return ref[pl.ds(row, 1), k * d:(k + 1) * d]

    x = x_ref[...]
    if has_f:
        f = f_ref[...].astype(F32)
        x = x + modv(moda_ref, gate_k) * (_rms(f) * na_ref[ia:ia + 1, :])
        xo_ref[...] = x
    if has_next:
        h = _rms(x) * nbn_ref[ib:ib + 1, :]
        h = h * (1.0 + modv(modb_ref, sc_k)) + modv(modb_ref, sh_k)
        if not has_router:
            h_ref[...] = h.astype(h_ref.dtype)
        if has_router:
            h_ref[...] = _pack_halves(h)
            logits = jnp.dot(h, wr_ref[...], preferred_element_type=F32,
                             precision=lax.Precision.HIGHEST) + br_ref[...]
            lane = lax.broadcasted_iota(jnp.int32, logits.shape, 1)
            lg = jnp.where(lane < N_EXPERTS, logits, -jnp.inf)
            v1 = jnp.max(lg, axis=-1, keepdims=True)
            i1 = jnp.min(jnp.where(lg == v1, lane, LANE), axis=-1, keepdims=True)
            lg2 = jnp.where(lane == i1, -jnp.inf, lg)
            v2 = jnp.max(lg2, axis=-1, keepdims=True)
            i2 = jnp.min(jnp.where(lg2 == v2, lane, LANE), axis=-1, keepdims=True)
            e2 = jnp.exp(v2 - v1)
            w1 = 1.0 / (1.0 + e2)
            w2 = e2 / (1.0 + e2)
            zero = jnp.zeros_like(logits)
            rt = (jnp.where(lane == 8, i1.astype(F32), zero) + jnp.where(lane == 9, i2.astype(F32), zero)
                  + jnp.where(lane == 10, w1, zero) + jnp.where(lane == 11, w2, zero))
            rt_ref[...] = rt


def _resid(x, f, mod_a, norms_a, mod_b, norms_b, router, *, x_off, f_off, nblk, ctx_first,
           ia=0, gate_k=0, ib=0, sh_k=0, sc_k=0):
    nb, _, d = x.shape
    has_f = f is not None
    has_next = mod_b is not None
    has_router = router is not None
    rows = nblk * ROW_BLK
    blk = lambda off: pl.BlockSpec((None, ROW_BLK, d), lambda b, j: (b, j + off, 0))
    full = lambda a: pl.BlockSpec(a.shape, lambda b, j: (0,) * a.ndim)
    ins, specs = [x], [blk(x_off)]
    if has_f:
        ins += [f, mod_a, norms_a]
        specs += [blk(f_off), full(mod_a), full(norms_a)]
    if has_next:
        ins += [mod_b, norms_b]
        specs += [full(mod_b), full(norms_b)]
    if has_router:
        ins += list(router)
        specs += [full(router[0]), full(router[1])]
    outs, ospecs = [], []
    if has_f:
        outs.append(jax.ShapeDtypeStruct((nb, rows, d), F32))
        ospecs.append(blk(0))
    if has_next and not has_router:
        outs.append(jax.ShapeDtypeStruct((nb, rows, d), BF16))
        ospecs.append(blk(0))
    if has_next and has_router:
        outs.append(jax.ShapeDtypeStruct((nb, rows, d // 2), jnp.uint32))
        ospecs.append(pl.BlockSpec((None, ROW_BLK, d // 2), lambda b, j: (b, j, 0)))
    if has_router:
        outs.append(jax.ShapeDtypeStruct((nb, rows, LANE), F32))
        ospecs.append(pl.BlockSpec((None, ROW_BLK, LANE), lambda b, j: (b, j, 0)))
    res = pl.pallas_call(
        functools.partial(_resid_kernel, nb=nb, ctx_first=ctx_first, has_f=has_f, has_next=has_next,
                          has_router=has_router, ia=ia, gate_k=gate_k, ib=ib, sh_k=sh_k, sc_k=sc_k),
        grid=(nb, nblk),
        in_specs=specs, out_specs=ospecs, out_shape=outs,
        compiler_params=_cparams(("arbitrary", "arbitrary")),
        name="resid_norm",
    )(*ins)
    res = list(res)
    x_new = res.pop(0) if has_f else None
    h = res.pop(0) if has_next else None
    rt = res.pop(0) if has_router else None
    return x_new, h, rt


def _mm_kernel(x_ref, w_ref, o_ref):
    o_ref[...] = jnp.dot(x_ref[...], w_ref[...], preferred_element_type=F32).astype(o_ref.dtype)


def _matmul(x, w, li, out_dtype, tm=1024, tn=1024):
    m, k = x.shape
    n = w.shape[2]
    return pl.pallas_call(
        _mm_kernel,
        grid=(m // tm, n // tn),
        in_specs=[pl.BlockSpec((tm, k), lambda i, j: (i, 0)),
                  pl.BlockSpec((None, k, tn), lambda i, j: (li, 0, j))],
        out_specs=pl.BlockSpec((tm, tn), lambda i, j: (i, j)),
        out_shape=jax.ShapeDtypeStruct((m, n), out_dtype),
        compiler_params=_cparams(("arbitrary", "arbitrary")),
        name="matmul",
    )(x, w)


def _ret_kernel(lg_ref, q_ref, k_ref, v_ref, g_ref, cos_ref, sin_ref, rot_ref, o_ref, q_s, k_s, acc_s):
    h = pl.program_id(1)
    lgf = lg_ref[0, h]
    lgb = lg_ref[1, h]
    c = ROW_BLK
    n = q_ref.shape[0] // c
    scale = RET_DK ** -0.5

    q_s[0:c, :] = (q_ref[0:c, :].astype(F32) * scale).astype(BF16)
    k_s[0:c, :] = k_ref[0:c, :]
    rot = rot_ref[...]
    for i in range(1, n):
        rows = slice(i * c, (i + 1) * c)
        trow = slice((i - 1) * c, i * c)
        cs, sn = cos_ref[trow, :], sin_ref[trow, :]
        qa, ka = q_ref[rows, :], k_ref[rows, :]
        qr = jnp.dot(qa, rot, preferred_element_type=F32)
        kr = jnp.dot(ka, rot, preferred_element_type=F32)
        q_s[rows, :] = ((qa.astype(F32) * cs + qr * sn) * scale).astype(BF16)
        k_s[rows, :] = (ka.astype(F32) * cs + kr * sn).astype(BF16)

    ti = lax.broadcasted_iota(jnp.int32, (c, c), 0)
    si = lax.broadcasted_iota(jnp.int32, (c, c), 1)
    dd = (ti - si).astype(F32)
    dmat = jnp.where(dd > 0, jnp.exp(lgf * dd), jnp.where(dd < 0, jnp.exp(-lgb * dd), 2.0))
    tcol = lax.broadcasted_iota(jnp.int32, (c, 1), 0).astype(F32)
    q_f = jnp.exp(lgf * (tcol + 1.0))
    q_b = jnp.exp(lgb * (c - tcol))
    k_f = jnp.exp(lgf * (c - 1.0 - tcol))
    k_b = jnp.exp(lgb * tcol)
    g_f = jnp.exp(lgf * c)
    g_b = jnp.exp(lgb * c)

    def kv_state(kc, vc, kdec):
        kd = (kc.astype(F32) * kdec).astype(BF16)
        return lax.dot_general(kd, vc, (((0,), (0,)), ((), ())), preferred_element_type=F32)

    s_f = jnp.zeros((RET_DK, q_ref.shape[1]), F32)
    for i in range(n):
        rows = slice(i * c, (i + 1) * c)
        qc, kc, vc = q_s[rows, :], k_s[rows, :], v_ref[rows, :]
        sc = lax.dot_general(qc, kc, (((1,), (1,)), ((), ())), preferred_element_type=F32)
        o = jnp.dot((sc * dmat).astype(BF16), vc, preferred_element_type=F32)
        if i > 0:
            qd = (qc.astype(F32) * q_f).astype(BF16)
            o = o + jnp.dot(qd, s_f.astype(BF16), preferred_element_type=F32)
        acc_s[rows, :] = o
        if i < n - 1:
            s_f = g_f * s_f + kv_state(kc, vc, k_f)

    s_b = kv_state(k_s[0:c, :], v_ref[0:c, :], k_b)
    for i in range(n - 1, 0, -1):
        rows = slice(i * c, (i + 1) * c)
        qc, kc, vc = q_s[rows, :], k_s[rows, :], v_ref[rows, :]
        qd = (qc.astype(F32) * q_b).astype(BF16)
        acc_s[rows, :] = acc_s[rows, :] + jnp.dot(qd, s_b.astype(BF16), preferred_element_type=F32)
        if i > 1:
            s_b = g_b * s_b + kv_state(kc, vc, k_b)

    for i in range(n):
        rows = slice(i * c, (i + 1) * c)
        o_ref[rows, :] = (_rms(acc_s[rows, :]) * _silu(g_ref[rows, :].astype(F32))).astype(o_ref.dtype)


def _retention(proj, lg, cos, sin, rot):
    nb, l, _ = proj.shape
    col = lambda c0: pl.BlockSpec((None, l, LANE), lambda b, h: (b, 0, c0 + h))
    const = lambda a: pl.BlockSpec(a.shape, lambda b, h: (0,) * a.ndim)
    return pl.pallas_call(
        _ret_kernel,
        grid=(nb, RET_HEADS),
        in_specs=[pl.BlockSpec(memory_space=pltpu.SMEM),
                  col(C_RET_Q), col(C_RET_K), col(C_RET_V), col(C_RET_G),
                  const(cos), const(sin), const(rot)],
        out_specs=pl.BlockSpec((None, l, LANE), lambda b, h: (b, 0, h)),
        out_shape=jax.ShapeDtypeStruct((nb, l, BRANCH_W), BF16),
        scratch_shapes=[pltpu.VMEM((l, LANE), BF16), pltpu.VMEM((l, LANE), BF16), pltpu.VMEM((l, LANE), F32)],
        compiler_params=_cparams(("arbitrary", "arbitrary")),
        name="retention",
    )(lg, proj, proj, proj, proj, cos, sin, rot)


def _gla_kernel(q_ref, k_ref, v_ref, g_ref, lr_ref, w2_ref, b2_ref, o_ref,
                accf_s, accb_s, st_s, qd_s, ki_s, ke_s, dl_s, *, n_ctx_chunks):
    c = GLA_CHUNK
    l = q_ref.shape[0]
    n = l // c
    cpb = ROW_BLK // c
    kw = GLA_HEADS * GLA_DK
    scale = GLA_DK ** -0.5

    ti = lax.broadcasted_iota(jnp.int32, (ROW_BLK, ROW_BLK), 0)
    si = lax.broadcasted_iota(jnp.int32, (ROW_BLK, ROW_BLK), 1)
    same = ((ti // c) == (si // c)).astype(F32)
    cum_lo = (same * (ti >= si).astype(F32)).astype(BF16)
    cum_up = (same * (ti <= si).astype(F32)).astype(BF16)

    def prep(bi, carry):
        rows = pl.ds(pl.multiple_of(bi * ROW_BLK, ROW_BLK), ROW_BLK)
        lr = lr_ref[rows, :]
        qf = q_ref[rows, :].astype(F32) * scale
        kf = k_ref[rows, :].astype(F32)
        for d, cum, last in ((0, cum_lo, c - 1), (1, cum_up, 0)):
            z = jnp.dot(lr, w2_ref[d], preferred_element_type=F32) + b2_ref[d]
            la = (jnp.minimum(z, 0.0) - jnp.log(1.0 + jnp.exp(-jnp.abs(z)))) * (1.0 / GLA_TAU)
            la_hi = la.astype(BF16)
            la_lo = (la - la_hi.astype(F32)).astype(BF16)
            bcum = (jnp.dot(cum, la_hi, preferred_element_type=F32)
                    + jnp.dot(cum, la_lo, preferred_element_type=F32))
            bl = jnp.concatenate([jnp.broadcast_to(bcum[cc * c + last:cc * c + last + 1, :], (c, kw))
                                  for cc in range(cpb)], axis=0)
            qd_s[d, rows, :] = (qf * jnp.exp(bcum)).astype(BF16)
            ki_s[d, rows, :] = (kf * jnp.exp(-bcum)).astype(BF16)
            ke_s[d, rows, :] = (kf * jnp.exp(bl - bcum)).astype(BF16)
            dec = jnp.exp(bl)
            for cc in range(cpb):
                dl_s[d, pl.ds(pl.multiple_of((bi * cpb + cc) * 8, 8), 8), :] = dec[cc * c:cc * c + 8, :]
        return carry

    lax.fori_loop(0, l // ROW_BLK, prep, 0)

    t4 = lax.broadcasted_iota(jnp.int32, (GLA_HEADS * c, c), 0) % c
    s4 = lax.broadcasted_iota(jnp.int32, (GLA_HEADS * c, c), 1)
    mask_lo = (t4 >= s4).astype(F32)
    mask_up = (t4 <= s4).astype(F32)
    srow = lax.broadcasted_iota(jnp.int32, (GLA_HEADS * c, kw), 0) // c
    slane = lax.broadcasted_iota(jnp.int32, (GLA_HEADS * c, kw), 1) // GLA_DK
    stack_mask = srow == slane
    lane_head = lax.broadcasted_iota(jnp.int32, (GLA_DV, kw), 1) // GLA_DK
    lanes = (((1,), (1,)), ((), ()))
    st_s[...] = jnp.zeros_like(st_s)

    def chunk(ci, d, mask, acc_ref):
        rows = pl.ds(pl.multiple_of(ci * c, c), c)
        q_dec, k_inv, k_end = qd_s[d, rows, :], ki_s[d, rows, :], ke_s[d, rows, :]
        vc = v_ref[rows, :]
        dec = dl_s[d, pl.ds(pl.multiple_of(ci * 8, 8), 1), :]
        q_stack = jnp.where(stack_mask, jnp.concatenate([q_dec] * GLA_HEADS, axis=0), jnp.zeros((), BF16))
        sc = (lax.dot_general(q_stack, k_inv, lanes, preferred_element_type=F32) * mask).astype(BF16)
        st = st_s[d]
        inter = lax.dot_general(q_stack, st.astype(BF16), lanes, preferred_element_type=F32)
        acc_ref[rows, :] = jnp.concatenate(
            [jnp.dot(sc[h * c:(h + 1) * c, :], vc[:, h * GLA_DV:(h + 1) * GLA_DV], preferred_element_type=F32)
             + inter[h * c:(h + 1) * c, :] for h in range(GLA_HEADS)], axis=1)
        full = lax.dot_general(vc, k_end, (((0,), (0,)), ((), ())), preferred_element_type=F32)
        comp = full[0:GLA_DV, :]
        for h in range(1, GLA_HEADS):
            comp = jnp.where(lane_head == h, full[h * GLA_DV:(h + 1) * GLA_DV, :], comp)
        st_s[d] = st * dec + comp

    def step(i, carry):
        chunk(i, 0, mask_lo, accf_s)
        chunk(jnp.where(i < n_ctx_chunks, n_ctx_chunks - 1 - i, n + n_ctx_chunks - 1 - i), 1, mask_up, accb_s)
        return carry

    lax.fori_loop(0, n, step, 0, unroll=2)

    for i in range(l // ROW_BLK):
        rows = slice(i * ROW_BLK, (i + 1) * ROW_BLK)
        o = accf_s[rows, :] + accb_s[rows, :]
        o = jnp.concatenate([_rms(o[:, h * GLA_DV:(h + 1) * GLA_DV]) for h in range(GLA_HEADS)], axis=1)
        o_ref[rows, :] = (o * _silu(g_ref[rows, :].astype(F32))).astype(o_ref.dtype)


def _gla(proj, w2p, b2):
    nb, l, _ = proj.shape
    kw, vw = GLA_HEADS * GLA_DK, GLA_HEADS * GLA_DV
    col = lambda c0, w: pl.BlockSpec((None, l, w), lambda b: (b, 0, c0 * LANE // w))
    const = lambda a: pl.BlockSpec(a.shape, lambda b: (0,) * a.ndim)
    return pl.pallas_call(
        functools.partial(_gla_kernel, n_ctx_chunks=CTX_LEN // GLA_CHUNK),
        grid=(nb,),
        in_specs=[col(C_GLA_Q, kw), col(C_GLA_K, kw), col(C_GLA_V, vw), col(C_GLA_G, vw), col(C_GLA_LR, LANE),
                  const(w2p), const(b2)],
        out_specs=pl.BlockSpec((None, l, vw), lambda b: (b, 0, 0)),
        out_shape=jax.ShapeDtypeStruct((nb, l, BRANCH_W), BF16),
        scratch_shapes=[pltpu.VMEM((l, vw), F32), pltpu.VMEM((l, vw), F32),
                        pltpu.VMEM((2, GLA_DV, kw), F32),
                        pltpu.VMEM((2, l, kw), BF16), pltpu.VMEM((2, l, kw), BF16), pltpu.VMEM((2, l, kw), BF16),
                        pltpu.VMEM((2, l // GLA_CHUNK * 8, kw), F32)],
        compiler_params=_cparams(("arbitrary",)),
        name="gla",
    )(proj, proj, proj, proj, proj, w2p, b2)


NA_QROWS = ROW_BLK // GRID_W
NA_SLAB = 3


def _na_kernel(q_ref, kc_ref, k0_ref, k1_ref, k2_ref, vc_ref, v0_ref, v1_ref, v2_ref, bias_ref, o_ref):
    scale = NA_DH ** -0.5
    dims = (((1,), (1,)), ((), ()))
    for h in range(NA_HEADS):
        cols = slice(h * NA_DH, (h + 1) * NA_DH)
        q = q_ref[:, cols]
        s = [lax.dot_general(q, kc_ref[:, cols], dims, preferred_element_type=F32) * scale]
        for j, kr in enumerate((k0_ref, k1_ref, k2_ref)):
            sj = lax.dot_general(q, kr[:, cols], dims, preferred_element_type=F32) * scale
            s.append(sj + bias_ref[h, :, j * ROW_BLK:(j + 1) * ROW_BLK])
        m = functools.reduce(jnp.maximum, [jnp.max(x, axis=-1, keepdims=True) for x in s])
        p = [jnp.exp(x - m) for x in s]
        den = functools.reduce(lambda a, b: a + b, [jnp.sum(x, axis=-1, keepdims=True) for x in p])
        o = jnp.zeros((q.shape[0], NA_DH), F32)
        for pj, vr in zip(p, (vc_ref, v0_ref, v1_ref, v2_ref)):
            o = o + jnp.dot(pj.astype(BF16), vr[:, cols], preferred_element_type=F32)
        o_ref[:, cols] = (o / den).astype(o_ref.dtype)


def _na_slab_start(qb, n_lat_blk):
    return jnp.clip(qb - 2, 0, n_lat_blk - NA_SLAB)


def _na(proj, bias):
    nb, l, _ = proj.shape
    nblk = l // ROW_BLK
    nlat = nblk - 1
    w = NA_HEADS * NA_DH
    blk = lambda c0, rowfn: pl.BlockSpec((None, ROW_BLK, w), lambda qb, b: (b, rowfn(qb), c0 * LANE // w))
    slab = lambda j: (lambda qb: 1 + _na_slab_start(qb, nlat) + j)
    return pl.pallas_call(
        _na_kernel,
        grid=(nblk, nb),
        in_specs=[blk(C_NA_Q, lambda qb: qb),
                  blk(C_NA_K, lambda qb: 0), blk(C_NA_K, slab(0)), blk(C_NA_K, slab(1)), blk(C_NA_K, slab(2)),
                  blk(C_NA_V, lambda qb: 0), blk(C_NA_V, slab(0)), blk(C_NA_V, slab(1)), blk(C_NA_V, slab(2)),
                  pl.BlockSpec((NA_HEADS, None, ROW_BLK, NA_SLAB * ROW_BLK), lambda qb, b: (0, qb, 0, 0))],
        out_specs=pl.BlockSpec((None, ROW_BLK, w), lambda qb, b: (b, qb, 0)),
        out_shape=jax.ShapeDtypeStruct((nb, l, BRANCH_W), BF16),
        compiler_params=_cparams(("arbitrary", "arbitrary")),
        name="neighbourhood_attention",
    )(*([proj] * 9), bias)


def _na_bias(rpb, seq):
    rows = seq // GRID_W
    kr = min(NA_WIN_R, rows)
    nlat = seq // ROW_BLK
    cq = np.arange(GRID_W)[:, None]
    ck = np.arange(GRID_W)[None, :]
    win_start = np.clip(cq - NA_WIN_C // 2, 0, GRID_W - NA_WIN_C)
    in_win = (ck >= win_start) & (ck < win_start + NA_WIN_C)
    rel_c = np.clip(ck - cq, 1 - NA_WIN_C, NA_WIN_C - 1) + NA_WIN_C - 1
    tiles = jnp.where(in_win[None, None], rpb[:, :, rel_c].astype(F32), NEG)
    masked = jnp.full((rpb.shape[0], GRID_W, GRID_W), NEG, F32)
    blocks = [jnp.full((rpb.shape[0], ROW_BLK, NA_SLAB * ROW_BLK), NEG, F32)]
    for qb in range(nlat):
        s0 = int(np.clip(qb - 1, 0, nlat - NA_SLAB)) * NA_QROWS
        qrows = []
        for qr in range(NA_QROWS):
            r = qb * NA_QROWS + qr
            k0 = int(np.clip(r - kr // 2, 0, rows - kr))
            assert s0 <= k0 and k0 + kr <= s0 + NA_SLAB * NA_QROWS
            krows = []
            for kk in range(NA_SLAB * NA_QROWS):
                krow = s0 + kk
                if k0 <= krow < k0 + kr:
                    krows.append(tiles[:, krow - r + NA_WIN_R - 1])
                else:
                    krows.append(masked)
            qrows.append(jnp.concatenate(krows, axis=2))
        blocks.append(jnp.concatenate(qrows, axis=1))
    return jnp.stack(blocks, axis=1)


def _fourier_kernel(u_ref, cg_ref, sg_ref, dft_ref, o_ref, ab_s):
    l = u_ref.shape[0]

    @pl.when(pl.program_id(1) == 0)
    def _():
        for g in range(FNET_GROUPS):
            cols = slice(g * FNET_GW, (g + 1) * FNET_GW)
            u = u_ref[:, cols]
            ab_s[0:l, cols] = jnp.dot(u, cg_ref[...], preferred_element_type=F32).astype(BF16)
            ab_s[l:2 * l, cols] = jnp.dot(u, sg_ref[...], preferred_element_type=F32).astype(BF16)

    o_ref[...] = jnp.dot(dft_ref[...], ab_s[...], preferred_element_type=F32).astype(o_ref.dtype)


def _fourier(proj, cg, sg, dft):
    nb, l, _ = proj.shape
    tr = 3 * ROW_BLK
    w = FNET_GROUPS * FNET_GW
    return pl.pallas_call(
        _fourier_kernel,
        grid=(nb, l // tr),
        in_specs=[pl.BlockSpec((None, l, w), lambda b, i: (b, 0, C_FU * LANE // w)),
                  pl.BlockSpec(cg.shape, lambda b, i: (0, 0)),
                  pl.BlockSpec(sg.shape, lambda b, i: (0, 0)),
                  pl.BlockSpec((tr, 2 * l), lambda b, i: (i, 0))],
        out_specs=pl.BlockSpec((None, tr, w), lambda b, i: (b, i, 0)),
        out_shape=jax.ShapeDtypeStruct((nb, l, BRANCH_W), BF16),
        scratch_shapes=[pltpu.VMEM((2 * l, w), BF16)],
        compiler_params=_cparams(("arbitrary", "arbitrary")),
        name="fourier_mix",
    )(proj, cg, sg, dft)


def _dft_tables(n_ctx, seq):
    def cs(n):
        jk = (np.arange(n)[:, None] * np.arange(n)[None, :]) % n
        ang = 2.0 * np.pi * jk / n
        return np.cos(ang) / np.sqrt(n), np.sin(ang) / np.sqrt(n)

    cg, sg = cs(FNET_GW)
    l = n_ctx + seq
    cl = np.zeros((l, l))
    sl = np.zeros((l, l))
    cc, sc = cs(n_ctx)
    cs_, ss_ = cs(seq)
    cl[:n_ctx, :n_ctx], sl[:n_ctx, :n_ctx] = cc, sc
    cl[n_ctx:, n_ctx:], sl[n_ctx:, n_ctx:] = cs_, ss_
    dft = np.concatenate([cl, -sl], axis=1)
    return (jnp.asarray(cg, BF16), jnp.asarray(sg, BF16), jnp.asarray(dft, BF16))


def _merge_kernel(y0_ref, y1_ref, y2_ref, y3_ref, w_ref, g0_ref, g1_ref, g2_ref, g3_ref, o_ref):
    acc = None
    for i, (y_ref, g_ref) in enumerate(zip((y0_ref, y1_ref, y2_ref, y3_ref), (g0_ref, g1_ref, g2_ref, g3_ref))):
        t = _sigmoid(g_ref[...].astype(F32)) * jnp.dot(y_ref[...], w_ref[i], preferred_element_type=F32)
        acc = t if acc is None else acc + t
    o_ref[...] = acc.astype(o_ref.dtype)


def _merge(ys, w_up, li, proj2d, tm=1024, tn=512):
    t = proj2d.shape[0]
    d = w_up.shape[3]
    gate0 = C_GATE * LANE // tn
    yspec = pl.BlockSpec((tm, BRANCH_W), lambda i, j: (i, 0))
    gspec = lambda br: pl.BlockSpec((tm, tn), lambda i, j: (i, gate0 + br * (d // tn) + j))
    return pl.pallas_call(
        _merge_kernel,
        grid=(t // tm, d // tn),
        in_specs=[yspec] * 4 + [pl.BlockSpec((None, 4, BRANCH_W, tn), lambda i, j: (li, 0, 0, j))]
        + [gspec(br) for br in range(4)],
        out_specs=pl.BlockSpec((tm, tn), lambda i, j: (i, j)),
        out_shape=jax.ShapeDtypeStruct((t, d), BF16),
        compiler_params=_cparams(("arbitrary", "arbitrary")),
        name="gated_merge",
    )(*ys, w_up, proj2d, proj2d, proj2d, proj2d)


def _ffn_kernel(x_ref, w1_ref, w3_ref, w2_ref, o_ref, acc_s):
    j = pl.program_id(1)
    x = x_ref[...]
    a = jnp.dot(x, w1_ref[...], preferred_element_type=F32)
    b = jnp.dot(x, w3_ref[...], preferred_element_type=F32)
    g = (_silu(a) * b).astype(BF16)
    part = jnp.dot(g, w2_ref[...], preferred_element_type=F32)

    @pl.when(j == 0)
    def _():
        acc_s[...] = part

    @pl.when(j > 0)
    def _():
        acc_s[...] += part

    @pl.when(j == pl.num_programs(1) - 1)
    def _():
        o_ref[...] = acc_s[...].astype(o_ref.dtype)


def _ffn(x, w1, w3, w2, li, tm=1024, tf=512):
    t, d = x.shape
    ff = w1.shape[2]
    return pl.pallas_call(
        _ffn_kernel,
        grid=(t // tm, ff // tf),
        in_specs=[pl.BlockSpec((tm, d), lambda i, j: (i, 0)),
                  pl.BlockSpec((None, d, tf), lambda i, j: (li, 0, j)),
                  pl.BlockSpec((None, d, tf), lambda i, j: (li, 0, j)),
                  pl.BlockSpec((None, tf, d), lambda i, j: (li, j, 0))],
        out_specs=pl.BlockSpec((tm, d), lambda i, j: (i, 0)),
        out_shape=jax.ShapeDtypeStruct((t, d), BF16),
        scratch_shapes=[pltpu.VMEM((tm, d), F32)],
        compiler_params=_cparams(("arbitrary", "arbitrary")),
        name="swiglu",
    )(x, w1, w3, w2)


MOE_TM = 512
MOE_TF = 1024
MOE_TN = 256


def _dispatch_kernel(nu_ref, tok_ref, h_hbm, o_ref, xg_s, sem):
    i = pl.program_id(0)
    tm = xg_s.shape[0]
    used = i < nu_ref[0]

    @pl.when(used)
    def _():
        def issue(r, carry):
            tok = tok_ref[i * tm + r]
            pltpu.make_async_copy(h_hbm.at[pl.ds(tok, 1), :], xg_s.at[pl.ds(r, 1), :], sem).start()
            return carry

        lax.fori_loop(0, tm, issue, 0, unroll=8)
        pltpu.make_async_copy(h_hbm.at[pl.ds(0, tm), :], xg_s, sem).wait()
        lo, hi = _unpack_halves(xg_s[...])
        half = xg_s.shape[1]
        o_ref[:, :half] = lo.astype(o_ref.dtype)
        o_ref[:, half:] = hi.astype(o_ref.dtype)

    @pl.when(jnp.logical_not(used))
    def _():
        o_ref[...] = jnp.zeros_like(o_ref)


def _moe_dispatch(hp, n_used, src_tok):
    p = src_tok.shape[0]
    half = hp.shape[1]
    d = 2 * half
    tm = MOE_TM
    grid_spec = pltpu.PrefetchScalarGridSpec(
        num_scalar_prefetch=2,
        grid=(p // tm,),
        in_specs=[pl.BlockSpec(memory_space=pl.ANY)],
        out_specs=pl.BlockSpec((tm, d), lambda i, nu, tok: (i, 0)),
        scratch_shapes=[pltpu.VMEM((tm, half), jnp.uint32), pltpu.SemaphoreType.DMA(())],
    )
    return pl.pallas_call(
        _dispatch_kernel,
        grid_spec=grid_spec,
        out_shape=jax.ShapeDtypeStruct((p, d), BF16),
        compiler_params=_cparams(("arbitrary",), row_dma=True),
        name="moe_dispatch",
    )(n_used, src_tok, hp)


def _expert_changed(te_ref, i):
    return jnp.logical_or(i == 0, te_ref[i] != te_ref[jnp.maximum(i - 1, 0)])


def _moe_up_kernel(te_ref, nu_ref, x_ref, w1_ref, w3_ref, o_ref, w1b_s, w3b_s):
    i = pl.program_id(1)

    @pl.when(_expert_changed(te_ref, i))
    def _():
        w1b_s[...] = w1_ref[...].astype(BF16)
        w3b_s[...] = w3_ref[...].astype(BF16)

    @pl.when(i < nu_ref[0])
    def _():
        x = x_ref[...]
        a = jnp.dot(x, w1b_s[...], preferred_element_type=F32)
        b = jnp.dot(x, w3b_s[...], preferred_element_type=F32)
        o_ref[...] = (_silu(a) * b).astype(o_ref.dtype)

    @pl.when(i >= nu_ref[0])
    def _():
        o_ref[...] = jnp.zeros_like(o_ref)


def _moe_down_kernel(te_ref, nu_ref, g_ref, w2a_ref, w2b_ref, o_ref, w2a_s, w2b_s):
    i = pl.program_id(1)

    @pl.when(_expert_changed(te_ref, i))
    def _():
        w2a_s[...] = w2a_ref[...].astype(BF16)
        w2b_s[...] = w2b_ref[...].astype(BF16)

    @pl.when(i < nu_ref[0])
    def _():
        g = g_ref[...]
        ya = jnp.dot(g, w2a_s[...], preferred_element_type=F32)
        yb = jnp.dot(g, w2b_s[...], preferred_element_type=F32)
        o_ref[...] = _pack_halves(jnp.concatenate([ya, yb], axis=1))

    @pl.when(i >= nu_ref[0])
    def _():
        o_ref[...] = jnp.zeros_like(o_ref)


def _moe_ffn(xs, tile_e, n_used, w1, w3, w2, li):
    p, d = xs.shape
    ff = w1.shape[3]
    tm, tf, tn = MOE_TM, MOE_TF, MOE_TN
    up_spec = pltpu.PrefetchScalarGridSpec(
        num_scalar_prefetch=2,
        grid=(ff // tf, p // tm),
        in_specs=[pl.BlockSpec((tm, d), lambda j, i, te, nu: (i, 0)),
                  pl.BlockSpec((None, None, d, tf), lambda j, i, te, nu: (li, te[i], 0, j)),
                  pl.BlockSpec((None, None, d, tf), lambda j, i, te, nu: (li, te[i], 0, j))],
        out_specs=pl.BlockSpec((tm, tf), lambda j, i, te, nu: (i, j)),
        scratch_shapes=[pltpu.VMEM((d, tf), BF16), pltpu.VMEM((d, tf), BF16)],
    )
    g = pl.pallas_call(
        _moe_up_kernel,
        grid_spec=up_spec,
        out_shape=jax.ShapeDtypeStruct((p, ff), BF16),
        compiler_params=_cparams(("arbitrary", "arbitrary")),
        name="moe_up",
    )(tile_e, n_used, xs, w1, w3)
    half_blks = d // 2 // tn
    down_spec = pltpu.PrefetchScalarGridSpec(
        num_scalar_prefetch=2,
        grid=(half_blks, p // tm),
        in_specs=[pl.BlockSpec((tm, ff), lambda j, i, te, nu: (i, 0)),
                  pl.BlockSpec((None, None, ff, tn), lambda j, i, te, nu: (li, te[i], 0, j)),
                  pl.BlockSpec((None, None, ff, tn), lambda j, i, te, nu: (li, te[i], 0, half_blks + j))],
        out_specs=pl.BlockSpec((tm, tn), lambda j, i, te, nu: (i, j)),
        scratch_shapes=[pltpu.VMEM((ff, tn), BF16), pltpu.VMEM((ff, tn), BF16)],
    )
    return pl.pallas_call(
        _moe_down_kernel,
        grid_spec=down_spec,
        out_shape=jax.ShapeDtypeStruct((p, d // 2), jnp.uint32),
        compiler_params=_cparams(("arbitrary", "arbitrary")),
        name="moe_down",
    )(tile_e, n_used, g, w2, w2)


def _combine_kernel(pos_ref, y_hbm, rt_ref, o_ref, g0_s, g1_s, sem):
    i = pl.program_id(0)
    tc = g0_s.shape[0]

    def issue(r, carry):
        t = i * tc + r
        pltpu.make_async_copy(y_hbm.at[pl.ds(pos_ref[2 * t], 1), :], g0_s.at[pl.ds(r, 1), :], sem.at[0]).start()
        pltpu.make_async_copy(y_hbm.at[pl.ds(pos_ref[2 * t + 1], 1), :], g1_s.at[pl.ds(r, 1), :], sem.at[1]).start()
        return carry

    lax.fori_loop(0, tc, issue, 0, unroll=8)
    pltpu.make_async_copy(y_hbm.at[pl.ds(0, tc), :], g0_s, sem.at[0]).wait()
    pltpu.make_async_copy(y_hbm.at[pl.ds(0, tc), :], g1_s, sem.at[1]).wait()
    w0 = rt_ref[:, 10:11]
    w1 = rt_ref[:, 11:12]
    lo0, hi0 = _unpack_halves(g0_s[...])
    lo1, hi1 = _unpack_halves(g1_s[...])
    half = g0_s.shape[1]
    o_ref[:, :half] = (w0 * lo0 + w1 * lo1).astype(o_ref.dtype)
    o_ref[:, half:] = (w0 * hi0 + w1 * hi1).astype(o_ref.dtype)


def _moe_combine(y, pos, route2d):
    t = route2d.shape[0]
    half = y.shape[1]
    d = 2 * half
    tc = ROW_BLK
    grid_spec = pltpu.PrefetchScalarGridSpec(
        num_scalar_prefetch=1,
        grid=(t // tc,),
        in_specs=[pl.BlockSpec(memory_space=pl.ANY),
                  pl.BlockSpec((tc, LANE), lambda i, pos: (i, 0))],
        out_specs=pl.BlockSpec((tc, d), lambda i, pos: (i, 0)),
        scratch_shapes=[pltpu.VMEM((tc, half), jnp.uint32), pltpu.VMEM((tc, half), jnp.uint32),
                        pltpu.SemaphoreType.DMA((2,))],
    )
    return pl.pallas_call(
        _combine_kernel,
        grid_spec=grid_spec,
        out_shape=jax.ShapeDtypeStruct((t, d), BF16),
        compiler_params=_cparams(("arbitrary",), row_dma=True),
        name="moe_combine",
    )(pos, y, route2d)


def _moe_plan(route, tm):
    t = route.shape[0]
    e_flat = route[:, 8:10].astype(jnp.int32).reshape(-1)
    na = 2 * t
    n_tiles = -(-(na + N_EXPERTS * (tm - 1)) // tm)
    p = n_tiles * tm
    onehot = (e_flat[:, None] == jnp.arange(N_EXPERTS)[None, :]).astype(jnp.int32)
    csum = jnp.cumsum(onehot, axis=0)
    counts = csum[-1]
    rank = jnp.sum((csum - onehot) * onehot, axis=1)
    tiles_e = (counts + tm - 1) // tm
    tile_end = jnp.cumsum(tiles_e)
    tile_start = tile_end - tiles_e
    n_used = tile_end[-1]
    cstart = jnp.cumsum(counts) - counts
    pos_of = (jnp.sum(onehot * tile_start[None, :], axis=1) * tm + rank).astype(jnp.int32)
    _, tok_sorted = lax.sort((pos_of, jnp.arange(na, dtype=jnp.int32) // 2), num_keys=1)
    tid = jnp.arange(n_tiles, dtype=jnp.int32)
    e_raw = jnp.sum((tid[:, None] >= tile_end[None, :]).astype(jnp.int32), axis=1)
    tile_e = jnp.minimum(jnp.sum((jnp.minimum(tid, n_used - 1)[:, None] >= tile_end[None, :]).astype(jnp.int32),
                                 axis=1), N_EXPERTS - 1)
    oh_t = (jnp.minimum(e_raw, N_EXPERTS - 1)[:, None] == jnp.arange(N_EXPERTS)[None, :]).astype(jnp.int32)
    t_start = jnp.sum(oh_t * tile_start[None, :], axis=1)
    t_count = jnp.where(tid < n_used, jnp.sum(oh_t * counts[None, :], axis=1), 0)
    t_cstart = jnp.sum(oh_t * cstart[None, :], axis=1)
    local = (tid - t_start)[:, None] * tm + jnp.arange(tm, dtype=jnp.int32)[None, :]
    valid = (local < t_count[:, None]).reshape(p)
    src = jnp.clip(t_cstart[:, None] + local, 0, na - 1).reshape(p)
    src_tok = jnp.where(valid, tok_sorted[src], 0)
    return tile_e.astype(jnp.int32), n_used.reshape(1).astype(jnp.int32), src_tok, pos_of


def _moe(hp, route, w1, w3, w2, li):
    nb, rows, half = hp.shape
    t = nb * rows
    route2d = route.reshape(t, LANE)
    tile_e, n_used, src_tok, pos_of = _moe_plan(route2d, MOE_TM)
    xs = _moe_dispatch(hp.reshape(t, half), n_used, src_tok)
    y = _moe_ffn(xs, tile_e, n_used, w1, w3, w2, li)
    return _moe_combine(y, pos_of, route2d).reshape(nb, rows, 2 * half)


def _rope_tables(seq):
    quarter = RET_DK // 4
    inv_freq = ROPE_BASE ** (-jnp.arange(quarter, dtype=F32) / quarter)
    t = jnp.arange(seq, dtype=jnp.int32)
    ang_r = (t // GRID_W).astype(F32)[:, None] * inv_freq[None]
    ang_c = (t % GRID_W).astype(F32)[:, None] * inv_freq[None]
    ang = jnp.concatenate([ang_r, ang_r, ang_c, ang_c], axis=1)
    rot = np.zeros((RET_DK, RET_DK), np.float32)
    for j in range(RET_DK):
        if (j % (2 * quarter)) < quarter:
            rot[j + quarter, j] = -1.0
        else:
            rot[j - quarter, j] = 1.0
    return jnp.cos(ang), jnp.sin(ang), jnp.asarray(rot, BF16)


PACK_TN = 512


def _pack_kernel(a_ref, b_ref, o_ref):
    blk = pl.program_id(1)
    main_blks = W_IN_SPLIT // PACK_TN
    sh = 2 * GLA_RANK

    @pl.when(blk < main_blks)
    def _():
        o_ref[...] = a_ref[...].T.astype(BF16)

    @pl.when(blk == main_blks)
    def _():
        row = lax.broadcasted_iota(jnp.int32, a_ref.shape, 0)
        o_ref[...] = jnp.where(row < sh, a_ref[...], 0.0).T.astype(BF16)

    @pl.when(blk > main_blks)
    def _():
        o_ref[...] = jnp.concatenate([a_ref[sh:, :], b_ref[:sh, :]], axis=0).T.astype(BF16)


def _pack_w_in(w):
    depth, d, n_src = w.shape
    wt = jnp.swapaxes(w, 1, 2)
    main_blks = W_IN_SPLIT // PACK_TN
    a_idx = lambda j: jnp.where(j <= main_blks, j, j - 1)
    last_b = (n_src - 1) // LANE
    return pl.pallas_call(
        _pack_kernel,
        grid=(depth, N_PROJ // PACK_TN),
        in_specs=[pl.BlockSpec((None, PACK_TN, d), lambda l, j: (l, a_idx(j), 0)),
                  pl.BlockSpec((None, LANE, d),
                               lambda l, j: (l, jnp.minimum((a_idx(j) + 1) * (PACK_TN // LANE), last_b), 0))],
        out_specs=pl.BlockSpec((None, d, PACK_TN), lambda l, j: (l, 0, j)),
        out_shape=jax.ShapeDtypeStruct((depth, d, N_PROJ), BF16),
        compiler_params=_cparams(("arbitrary", "arbitrary")),
        name="pack_w_in",
    )(wt, wt)


def kernel(x, c, ctx, c_ctx, w_ada, b_ada, norms, w_in, ret_decay, gla_w2, gla_b2, na_rpb, w_up, w_out,
           ffn_w1, ffn_w3, ffn_w2, moe_router, moe_router_b, moe_w1, moe_w3, moe_w2):
    nb, seq, d = x.shape
    n_ctx = ctx.shape[1]
    l = n_ctx + seq
    nblk = l // ROW_BLK
    assert n_ctx == ROW_BLK and seq % ROW_BLK == 0 and seq // ROW_BLK >= NA_SLAB and d == D_MODEL and nb < 16

    cvec = jnp.zeros((16, d), F32).at[:nb].set(c).at[nb].set(c_ctx)
    mod = _adaln(cvec, w_ada, b_ada)
    cos, sin, rot = _rope_tables(seq)
    cg, sg, dft = _dft_tables(n_ctx, seq)
    log_gamma = jnp.log1p(-jnp.exp(ret_decay.astype(F32)))

    w_in_b = _pack_w_in(w_in)
    w_up_b, w_out_b = w_up.astype(BF16), w_out.astype(BF16)
    ffn_b = (ffn_w1.astype(BF16), ffn_w3.astype(BF16), ffn_w2.astype(BF16))

    xs = jnp.concatenate([ctx, x], axis=1)
    _, h, _ = _resid(xs, None, None, None, mod[0], norms[0], None, x_off=0, f_off=0, nblk=nblk,
                     ctx_first=True, ib=0, sh_k=0, sc_k=1)
    for li in range(DEPTH):
        last = li == DEPTH - 1
        is_moe = li % 2 == 1
        proj2d = _matmul(h.reshape(nb * l, d), w_in_b, li, BF16)
        proj = proj2d.reshape(nb, l, N_PROJ)
        w2p = jnp.zeros((2, LANE, GLA_HEADS * GLA_DK), F32)
        w2p = w2p.at[0, :GLA_RANK].set(gla_w2[li, 0]).at[1, GLA_RANK:2 * GLA_RANK].set(gla_w2[li, 1]).astype(BF16)
        y_ret = _retention(proj, log_gamma[li], cos, sin, rot)
        y_na = _na(proj, _na_bias(na_rpb[li], seq))
        y_fn = _fourier(proj, cg, sg, dft)
        y_gla = _gla(proj, w2p, gla_b2[li].reshape(2, 1, -1).astype(F32))
        ys = [y.reshape(nb * l, BRANCH_W) for y in (y_ret, y_na, y_fn, y_gla)]
        merged = _merge(ys, w_up_b, li, proj2d)
        o = _matmul(merged, w_out_b, li, BF16).reshape(nb, l, d)

        router = None
        if is_moe:
            wr = jnp.zeros((d, LANE), F32).at[:, :N_EXPERTS].set(moe_router[li // 2])
            br = jnp.zeros((1, LANE), F32).at[0, :N_EXPERTS].set(moe_router_b[li // 2])
            router = (wr, br)
        if last:
            xs, h2, route = _resid(xs, o, mod[li], norms[li], mod[li], norms[li], router, x_off=1, f_off=1,
                                   nblk=nblk - 1, ctx_first=False, ia=1, gate_k=2, ib=2, sh_k=3, sc_k=4)
        else:
            xs, h2, route = _resid(xs, o, mod[li], norms[li], mod[li], norms[li], router, x_off=0, f_off=0,
                                   nblk=nblk, ctx_first=True, ia=1, gate_k=2, ib=2, sh_k=3, sc_k=4)
        rows = h2.shape[1]
        if is_moe:
            f = _moe(h2, route, moe_w1, moe_w3, moe_w2, li // 2)
        else:
            f = _ffn(h2.reshape(nb * rows, d), *ffn_b, li // 2).reshape(nb, rows, d)
        if last:
            xs, _, _ = _resid(xs, f, mod[li], norms[li], None, None, None, x_off=0, f_off=0, nblk=nblk - 1,
                              ctx_first=False, ia=3, gate_k=5)
        else:
            xs, h, _ = _resid(xs, f, mod[li], norms[li], mod[li + 1], norms[li + 1], None, x_off=0, f_off=0,
                              nblk=nblk, ctx_first=True, ia=3, gate_k=5, ib=0, sh_k=0, sc_k=1)
    return xs
```

```python
import functools
import math

import numpy as np
import jax
import jax.numpy as jnp
from jax import lax
from jax.experimental import pallas as pl
from jax.experimental.pallas import tpu as pltpu

F32 = jnp.float32
BF16 = jnp.bfloat16

D_MODEL = 2048
DEPTH = 4
GRID_W = 64
CTX_LEN = 256
EPS = 1e-6
ROPE_BASE = 10000.0
RET_HEADS, RET_DK = 4, 128
NA_HEADS, NA_DH = 4, 128
NA_WIN_R, NA_WIN_C = 8, 16
FNET_GROUPS, FNET_GW = 4, 128
GLA_HEADS, GLA_DK, GLA_DV, GLA_RANK, GLA_TAU = 4, 64, 128, 16, 16.0
GLA_CHUNK = 64
BRANCH_W = 512
N_EXPERTS = 8

LANE = 128
ROW_BLK = 256
VMEM_LIMIT = 56 * 1024 * 1024

C_RET_Q, C_RET_K, C_RET_V, C_RET_G = 0, 4, 8, 12
C_NA_Q, C_NA_K, C_NA_V = 16, 20, 24
C_FU = 28
C_GLA_Q, C_GLA_K, C_GLA_V, C_GLA_G, C_GLA_LR = 32, 34, 36, 40, 44
C_GATE = 48
N_PROJ = (C_GATE + 4 * D_MODEL // LANE) * LANE
W_IN_SPLIT = 5632

NEG = -1e30


def _cparams(sem, row_dma=False):
    return pltpu.CompilerParams(dimension_semantics=sem, vmem_limit_bytes=VMEM_LIMIT,
                                disable_bounds_checks=row_dma)


def _silu(x):
    return x * (1.0 / (1.0 + jnp.exp(-x)))


def _sigmoid(x):
    return 0.5 * jnp.tanh(0.5 * x) + 0.5


def _rms(x):
    return x * lax.rsqrt(jnp.mean(x * x, axis=-1, keepdims=True) + EPS)


HI_MASK = 0xFFFF0000


def _pack_halves(x):
    n = x.shape[1] // 2
    lo = pltpu.bitcast(x[:, :n].astype(BF16).astype(F32), jnp.uint32)
    hi = pltpu.bitcast(x[:, n:].astype(BF16).astype(F32), jnp.uint32)
    return (lo >> 16) | (hi & jnp.uint32(HI_MASK))


def _unpack_halves(p):
    return pltpu.bitcast(p << 16, F32), pltpu.bitcast(p & jnp.uint32(HI_MASK), F32)


def _ada_kernel(c_ref, w_ref, b_ref, o_ref):
    s = _silu(c_ref[...]).astype(BF16)
    o_ref[...] = jnp.dot(s, w_ref[...].astype(BF16), preferred_element_type=F32) + b_ref[...]


def _adaln(cvec, w_ada, b_ada):
    depth, d, n = w_ada.shape
    tn = 1024
    return pl.pallas_call(
        _ada_kernel,
        grid=(depth, n // tn),
        in_specs=[pl.BlockSpec((16, d), lambda l, j: (0, 0)),
                  pl.BlockSpec((None, d, tn), lambda l, j: (l, 0, j)),
                  pl.BlockSpec((None, 1, tn), lambda l, j: (l, 0, j))],
        out_specs=pl.BlockSpec((None, 16, tn), lambda l, j: (l, 0, j)),
        out_shape=jax.ShapeDtypeStruct((depth, 16, n), F32),
        compiler_params=_cparams(("arbitrary", "arbitrary")),
        name="adaln",
    )(cvec, w_ada, b_ada.reshape(depth, 1, n))


def _resid_kernel(*refs, nb, ctx_first, has_f, has_proj, has_next, has_router, ia, gate_k, ib, sh_k, sc_k):
    it = iter(refs)
    x_ref = next(it)
    f_ref = next(it) if has_f else None
    wf_ref = next(it) if has_proj else None
    moda_ref = next(it) if has_f else None
    na_ref = next(it) if has_f else None
    modb_ref = next(it) if has_next else None
    nbn_ref = next(it) if has_next else None
    wr_ref = next(it) if has_router else None
    br_ref = next(it) if has_router else None
    xo_ref = next(it) if has_f else None
    h_ref = next(it) if has_next else None
    rt_ref = next(it) if has_router else None

    b = pl.program_id(0)
    j = pl.program_id(1)
    row = jnp.where(j == 0, nb, b) if ctx_first else b
    d = x_ref.shape[-1]

    def modv(ref, k):
        return ref[pl.ds(row, 1), k * d:(k + 1) * d]

    x = x_ref[...]
    if has_f:
        if has_proj:
            f = jnp.dot(f_ref[...], wf_ref[...], preferred_element_type=F32)
        else:
            f = f_ref[...].astype(F32)
        x = x + modv(moda_ref, gate_k) * (_rms(f) * na_ref[ia:ia + 1, :])
        xo_ref[...] = x
    if has_next:
        h = _rms(x) * nbn_ref[ib:ib + 1, :]
        h = h * (1.0 + modv(modb_ref, sc_k)) + modv(modb_ref, sh_k)
        if not has_router:
            h_ref[...] = h.astype(h_ref.dtype)
        if has_router:
            h_ref[...] = _pack_halves(h)
            logits = jnp.dot(h, wr_ref[...], preferred_element_type=F32,
                             precision=lax.Precision.HIGHEST) + br_ref[...]
            lane = lax.broadcasted_iota(jnp.int32, logits.shape, 1)
            lg = jnp.where(lane < N_EXPERTS, logits, -jnp.inf)
            v1 = jnp.max(lg, axis=-1, keepdims=True)
            i1 = jnp.min(jnp.where(lg == v1, lane, LANE), axis=-1, keepdims=True)
            lg2 = jnp.where(lane == i1, -jnp.inf, lg)
            v2 = jnp.max(lg2, axis=-1, keepdims=True)
            i2 = jnp.min(jnp.where(lg2 == v2, lane, LANE), axis=-1, keepdims=True)
            e2 = jnp.exp(v2 - v1)
            w1 = 1.0 / (1.0 + e2)
            w2 = e2 / (1.0 + e2)
            zero = jnp.zeros_like(logits)
            rt = (jnp.where(lane == 8, i1.astype(F32), zero) + jnp.where(lane == 9, i2.astype(F32), zero)
                  + jnp.where(lane == 10, w1, zero) + jnp.where(lane == 11, w2, zero))
            rt_ref[...] = rt


def _resid(x, f, mod_a, norms_a, mod_b, norms_b, router, *, x_off, f_off, nblk, ctx_first,
           ia=0, gate_k=0, ib=0, sh_k=0, sc_k=0, proj=None):
    nb, _, d = x.shape
    has_f = f is not None
    has_proj = proj is not None
    has_next = mod_b is not None
    has_router = router is not None
    rows = nblk * ROW_BLK
    blk = lambda off: pl.BlockSpec((None, ROW_BLK, d), lambda b, j: (b, j + off, 0))
    full = lambda a: pl.BlockSpec(a.shape, lambda b, j: (0,) * a.ndim)
    ins, specs = [x], [blk(x_off)]
    if has_f:
        ins += [f]
        specs += [blk(f_off)]
        if has_proj:
            w, li = proj
            ins += [w]
            specs += [pl.BlockSpec((None, d, d), lambda b, j: (li, 0, 0))]
        ins += [mod_a, norms_a]
        specs += [full(mod_a), full(norms_a)]
    if has_next:
        ins += [mod_b, norms_b]
        specs += [full(mod_b), full(norms_b)]
    if has_router:
        ins += list(router)
        specs += [full(router[0]), full(router[1])]
    outs, ospecs = [], []
    if has_f:
        outs.append(jax.ShapeDtypeStruct((nb, rows, d), F32))
        ospecs.append(blk(0))
    if has_next and not has_router:
        outs.append(jax.ShapeDtypeStruct((nb, rows, d), BF16))
        ospecs.append(blk(0))
    if has_next and has_router:
        outs.append(jax.ShapeDtypeStruct((nb, rows, d // 2), jnp.uint32))
        ospecs.append(pl.BlockSpec((None, ROW_BLK, d // 2), lambda b, j: (b, j, 0)))
    if has_router:
        outs.append(jax.ShapeDtypeStruct((nb, rows, LANE), F32))
        ospecs.append(pl.BlockSpec((None, ROW_BLK, LANE), lambda b, j: (b, j, 0)))
    res = pl.pallas_call(
        functools.partial(_resid_kernel, nb=nb, ctx_first=ctx_first, has_f=has_f, has_proj=has_proj, has_next=has_next,
                          has_router=has_router, ia=ia, gate_k=gate_k, ib=ib, sh_k=sh_k, sc_k=sc_k),
        grid=(nb, nblk),
        in_specs=specs, out_specs=ospecs, out_shape=outs,
        compiler_params=_cparams(("arbitrary", "arbitrary")),
        name="resid_norm",
    )(*ins)
    res = list(res)
    x_new = res.pop(0) if has_f else None
    h = res.pop(0) if has_next else None
    rt = res.pop(0) if has_router else None
    return x_new, h, rt


def _mm_kernel(x_ref, w_ref, o_ref):
    o_ref[...] = jnp.dot(x_ref[...], w_ref[...], preferred_element_type=F32).astype(o_ref.dtype)


def _matmul(x, w, li, out_dtype, n, tm=1024, tn=1024):
    m, k = x.shape
    return pl.pallas_call(
        _mm_kernel,
        grid=(m // tm, n // tn),
        in_specs=[pl.BlockSpec((tm, k), lambda i, j: (i, 0)),
                  pl.BlockSpec((None, k, tn), lambda i, j: (li, 0, j))],
        out_specs=pl.BlockSpec((tm, tn), lambda i, j: (i, j)),
        out_shape=jax.ShapeDtypeStruct((m, n), out_dtype),
        compiler_params=_cparams(("arbitrary", "arbitrary")),
        name="matmul",
    )(x, w)


def _ret_kernel(lg_ref, q_ref, k_ref, v_ref, g_ref, cos_ref, sin_ref, rot_ref, o_ref, q_s, k_s, acc_s):
    h = pl.program_id(1)
    lgf = lg_ref[0, h]
    lgb = lg_ref[1, h]
    c = ROW_BLK
    n = q_ref.shape[0] // c
    scale = RET_DK ** -0.5

    q_s[0:c, :] = (q_ref[0:c, :].astype(F32) * scale).astype(BF16)
    k_s[0:c, :] = k_ref[0:c, :]
    rot = rot_ref[...]
    for i in range(1, n):
        rows = slice(i * c, (i + 1) * c)
        trow = slice((i - 1) * c, i * c)
        cs, sn = cos_ref[trow, :], sin_ref[trow, :]
        qa, ka = q_ref[rows, :], k_ref[rows, :]
        qr = jnp.dot(qa, rot, preferred_element_type=F32)
        kr = jnp.dot(ka, rot, preferred_element_type=F32)
        q_s[rows, :] = ((qa.astype(F32) * cs + qr * sn) * scale).astype(BF16)
        k_s[rows, :] = (ka.astype(F32) * cs + kr * sn).astype(BF16)

    ti = lax.broadcasted_iota(jnp.int32, (c, c), 0)
    si = lax.broadcasted_iota(jnp.int32, (c, c), 1)
    dd = (ti - si).astype(F32)
    dmat = jnp.where(dd > 0, jnp.exp(lgf * dd), jnp.where(dd < 0, jnp.exp(-lgb * dd), 2.0))
    tcol = lax.broadcasted_iota(jnp.int32, (c, 1), 0).astype(F32)
    q_f = jnp.exp(lgf * (tcol + 1.0))
    q_b = jnp.exp(lgb * (c - tcol))
    k_f = jnp.exp(lgf * (c - 1.0 - tcol))
    k_b = jnp.exp(lgb * tcol)
    g_f = jnp.exp(lgf * c)
    g_b = jnp.exp(lgb * c)

    def kv_state(kc, vc, kdec):
        kd = (kc.astype(F32) * kdec).astype(BF16)
        return lax.dot_general(kd, vc, (((0,), (0,)), ((), ())), preferred_element_type=F32)

    s_f = jnp.zeros((RET_DK, q_ref.shape[1]), F32)
    for i in range(n):
        rows = slice(i * c, (i + 1) * c)
        qc, kc, vc = q_s[rows, :], k_s[rows, :], v_ref[rows, :]
        sc = lax.dot_general(qc, kc, (((1,), (1,)), ((), ())), preferred_element_type=F32)
        o = jnp.dot((sc * dmat).astype(BF16), vc, preferred_element_type=F32)
        if i > 0:
            qd = (qc.astype(F32) * q_f).astype(BF16)
            o = o + jnp.dot(qd, s_f.astype(BF16), preferred_element_type=F32)
        acc_s[rows, :] = o
        if i < n - 1:
            s_f = g_f * s_f + kv_state(kc, vc, k_f)

    s_b = kv_state(k_s[0:c, :], v_ref[0:c, :], k_b)
    for i in range(n - 1, 0, -1):
        rows = slice(i * c, (i + 1) * c)
        qc, kc, vc = q_s[rows, :], k_s[rows, :], v_ref[rows, :]
        qd = (qc.astype(F32) * q_b).astype(BF16)
        acc_s[rows, :] = acc_s[rows, :] + jnp.dot(qd, s_b.astype(BF16), preferred_element_type=F32)
        if i > 1:
            s_b = g_b * s_b + kv_state(kc, vc, k_b)

    for i in range(n):
        rows = slice(i * c, (i + 1) * c)
        o_ref[rows, :] = (_rms(acc_s[rows, :]) * _silu(g_ref[rows, :].astype(F32))).astype(o_ref.dtype)


def _retention(proj, lg, cos, sin, rot):
    nb, l, _ = proj.shape
    col = lambda c0: pl.BlockSpec((None, l, LANE), lambda b, h: (b, 0, c0 + h))
    const = lambda a: pl.BlockSpec(a.shape, lambda b, h: (0,) * a.ndim)
    return pl.pallas_call(
        _ret_kernel,
        grid=(nb, RET_HEADS),
        in_specs=[pl.BlockSpec(memory_space=pltpu.SMEM),
                  col(C_RET_Q), col(C_RET_K), col(C_RET_V), col(C_RET_G),
                  const(cos), const(sin), const(rot)],
        out_specs=pl.BlockSpec((None, l, LANE), lambda b, h: (b, 0, h)),
        out_shape=jax.ShapeDtypeStruct((nb, l, BRANCH_W), BF16),
        scratch_shapes=[pltpu.VMEM((l, LANE), BF16), pltpu.VMEM((l, LANE), BF16), pltpu.VMEM((l, LANE), F32)],
        compiler_params=_cparams(("arbitrary", "arbitrary")),
        name="retention",
    )(lg, proj, proj, proj, proj, cos, sin, rot)


def _gla_kernel(q_ref, k_ref, v_ref, g_ref, lr_ref, w2_ref, b2_ref, o_ref,
                accf_s, accb_s, st_s, qd_s, ki_s, ke_s, dl_s, *, n_ctx_chunks):
    c = GLA_CHUNK
    l = q_ref.shape[0]
    n = l // c
    cpb = ROW_BLK // c
    kw = GLA_HEADS * GLA_DK
    scale = GLA_DK ** -0.5

    ti = lax.broadcasted_iota(jnp.int32, (ROW_BLK, ROW_BLK), 0)
    si = lax.broadcasted_iota(jnp.int32, (ROW_BLK, ROW_BLK), 1)
    same = ((ti // c) == (si // c)).astype(F32)
    cum_lo = (same * (ti >= si).astype(F32)).astype(BF16)
    cum_up = (same * (ti <= si).astype(F32)).astype(BF16)

    def prep(bi, carry):
        rows = pl.ds(pl.multiple_of(bi * ROW_BLK, ROW_BLK), ROW_BLK)
        lr = lr_ref[rows, :]
        qf = q_ref[rows, :].astype(F32) * scale
        kf = k_ref[rows, :].astype(F32)
        for d, cum, last in ((0, cum_lo, c - 1), (1, cum_up, 0)):
            z = jnp.dot(lr, w2_ref[d], preferred_element_type=F32) + b2_ref[d]
            la = (jnp.minimum(z, 0.0) - jnp.log(1.0 + jnp.exp(-jnp.abs(z)))) * (1.0 / GLA_TAU)
            la_hi = la.astype(BF16)
            la_lo = (la - la_hi.astype(F32)).astype(BF16)
            bcum = (jnp.dot(cum, la_hi, preferred_element_type=F32)
                    + jnp.dot(cum, la_lo, preferred_element_type=F32))
            bl = jnp.concatenate([jnp.broadcast_to(bcum[cc * c + last:cc * c + last + 1, :], (c, kw))
                                  for cc in range(cpb)], axis=0)
            qd_s[d, rows, :] = (qf * jnp.exp(bcum)).astype(BF16)
            ki_s[d, rows, :] = (kf * jnp.exp(-bcum)).astype(BF16)
            ke_s[d, rows, :] = (kf * jnp.exp(bl - bcum)).astype(BF16)
            dec = jnp.exp(bl)
            for cc in range(cpb):
                dl_s[d, pl.ds(pl.multiple_of((bi * cpb + cc) * 8, 8), 8), :] = dec[cc * c:cc * c + 8, :]
        return carry

    lax.fori_loop(0, l // ROW_BLK, prep, 0)

    t4 = lax.broadcasted_iota(jnp.int32, (GLA_HEADS * c, c), 0) % c
    s4 = lax.broadcasted_iota(jnp.int32, (GLA_HEADS * c, c), 1)
    mask_lo = (t4 >= s4).astype(F32)
    mask_up = (t4 <= s4).astype(F32)
    srow = lax.broadcasted_iota(jnp.int32, (GLA_HEADS * c, kw), 0) // c
    slane = lax.broadcasted_iota(jnp.int32, (GLA_HEADS * c, kw), 1) // GLA_DK
    stack_mask = srow == slane
    lane_head = lax.broadcasted_iota(jnp.int32, (GLA_DV, kw), 1) // GLA_DK
    lanes = (((1,), (1,)), ((), ()))
    st_s[...] = jnp.zeros_like(st_s)

    def chunk(ci, d, mask, acc_ref):
        rows = pl.ds(pl.multiple_of(ci * c, c), c)
        q_dec, k_inv, k_end = qd_s[d, rows, :], ki_s[d, rows, :], ke_s[d, rows, :]
        vc = v_ref[rows, :]
        dec = dl_s[d, pl.ds(pl.multiple_of(ci * 8, 8), 1), :]
        q_stack = jnp.where(stack_mask, jnp.concatenate([q_dec] * GLA_HEADS, axis=0), jnp.zeros((), BF16))
        sc = (lax.dot_general(q_stack, k_inv, lanes, preferred_element_type=F32) * mask).astype(BF16)
        st = st_s[d]
        inter = lax.dot_general(q_stack, st.astype(BF16), lanes, preferred_element_type=F32)
        acc_ref[rows, :] = jnp.concatenate(
            [jnp.dot(sc[h * c:(h + 1) * c, :], vc[:, h * GLA_DV:(h + 1) * GLA_DV], preferred_element_type=F32)
             + inter[h * c:(h + 1) * c, :] for h in range(GLA_HEADS)], axis=1)
        full = lax.dot_general(vc, k_end, (((0,), (0,)), ((), ())), preferred_element_type=F32)
        comp = full[0:GLA_DV, :]
        for h in range(1, GLA_HEADS):
            comp = jnp.where(lane_head == h, full[h * GLA_DV:(h + 1) * GLA_DV, :], comp)
        st_s[d] = st * dec + comp

    def step(i, carry):
        chunk(i, 0, mask_lo, accf_s)
        chunk(jnp.where(i < n_ctx_chunks, n_ctx_chunks - 1 - i, n + n_ctx_chunks - 1 - i), 1, mask_up, accb_s)
        return carry

    lax.fori_loop(0, n, step, 0, unroll=2)

    for i in range(l // ROW_BLK):
        rows = slice(i * ROW_BLK, (i + 1) * ROW_BLK)
        o = accf_s[rows, :] + accb_s[rows, :]
        o = jnp.concatenate([_rms(o[:, h * GLA_DV:(h + 1) * GLA_DV]) for h in range(GLA_HEADS)], axis=1)
        o_ref[rows, :] = (o * _silu(g_ref[rows, :].astype(F32))).astype(o_ref.dtype)


def _gla(proj, w2p, b2):
    nb, l, _ = proj.shape
    kw, vw = GLA_HEADS * GLA_DK, GLA_HEADS * GLA_DV
    col = lambda c0, w: pl.BlockSpec((None, l, w), lambda b: (b, 0, c0 * LANE // w))
    const = lambda a: pl.BlockSpec(a.shape, lambda b: (0,) * a.ndim)
    return pl.pallas_call(
        functools.partial(_gla_kernel, n_ctx_chunks=CTX_LEN // GLA_CHUNK),
        grid=(nb,),
        in_specs=[col(C_GLA_Q, kw), col(C_GLA_K, kw), col(C_GLA_V, vw), col(C_GLA_G, vw), col(C_GLA_LR, LANE),
                  const(w2p), const(b2)],
        out_specs=pl.BlockSpec((None, l, vw), lambda b: (b, 0, 0)),
        out_shape=jax.ShapeDtypeStruct((nb, l, BRANCH_W), BF16),
        scratch_shapes=[pltpu.VMEM((l, vw), F32), pltpu.VMEM((l, vw), F32),
                        pltpu.VMEM((2, GLA_DV, kw), F32),
                        pltpu.VMEM((2, l, kw), BF16), pltpu.VMEM((2, l, kw), BF16), pltpu.VMEM((2, l, kw), BF16),
                        pltpu.VMEM((2, l // GLA_CHUNK * 8, kw), F32)],
        compiler_params=_cparams(("arbitrary",)),
        name="gla",
    )(proj, proj, proj, proj, proj, w2p, b2)


NA_QROWS = ROW_BLK // GRID_W
NA_SLAB = 3


def _na_kernel(q_ref, kc_ref, k0_ref, k1_ref, k2_ref, vc_ref, v0_ref, v1_ref, v2_ref, bias_ref, o_ref):
    scale = NA_DH ** -0.5
    dims = (((1,), (1,)), ((), ()))
    for h in range(NA_HEADS):
        cols = slice(h * NA_DH, (h + 1) * NA_DH)
        q = q_ref[:, cols]
        s = [lax.dot_general(q, kc_ref[:, cols], dims, preferred_element_type=F32) * scale]
        for j, kr in enumerate((k0_ref, k1_ref, k2_ref)):
            sj = lax.dot_general(q, kr[:, cols], dims, preferred_element_type=F32) * scale
            s.append(sj + bias_ref[h, :, j * ROW_BLK:(j + 1) * ROW_BLK])
        m = functools.reduce(jnp.maximum, [jnp.max(x, axis=-1, keepdims=True) for x in s])
        p = [jnp.exp(x - m) for x in s]
        den = functools.reduce(lambda a, b: a + b, [jnp.sum(x, axis=-1, keepdims=True) for x in p])
        o = jnp.zeros((q.shape[0], NA_DH), F32)
        for pj, vr in zip(p, (vc_ref, v0_ref, v1_ref, v2_ref)):
            o = o + jnp.dot(pj.astype(BF16), vr[:, cols], preferred_element_type=F32)
        o_ref[:, cols] = (o / den).astype(o_ref.dtype)


def _na_slab_start(qb, n_lat_blk):
    return jnp.clip(qb - 2, 0, n_lat_blk - NA_SLAB)


def _na(proj, bias):
    nb, l, _ = proj.shape
    nblk = l // ROW_BLK
    nlat = nblk - 1
    w = NA_HEADS * NA_DH
    blk = lambda c0, rowfn: pl.BlockSpec((None, ROW_BLK, w), lambda qb, b: (b, rowfn(qb), c0 * LANE // w))
    slab = lambda j: (lambda qb: 1 + _na_slab_start(qb, nlat) + j)
    return pl.pallas_call(
        _na_kernel,
        grid=(nblk, nb),
        in_specs=[blk(C_NA_Q, lambda qb: qb),
                  blk(C_NA_K, lambda qb: 0), blk(C_NA_K, slab(0)), blk(C_NA_K, slab(1)), blk(C_NA_K, slab(2)),
                  blk(C_NA_V, lambda qb: 0), blk(C_NA_V, slab(0)), blk(C_NA_V, slab(1)), blk(C_NA_V, slab(2)),
                  pl.BlockSpec((NA_HEADS, None, ROW_BLK, NA_SLAB * ROW_BLK), lambda qb, b: (0, qb, 0, 0))],
        out_specs=pl.BlockSpec((None, ROW_BLK, w), lambda qb, b: (b, qb, 0)),
        out_shape=jax.ShapeDtypeStruct((nb, l, BRANCH_W), BF16),
        compiler_params=_cparams(("arbitrary", "arbitrary")),
        name="neighbourhood_attention",
    )(*([proj] * 9), bias)


def _na_bias(rpb, seq):
    rows = seq // GRID_W
    kr = min(NA_WIN_R, rows)
    nlat = seq // ROW_BLK
    cq = np.arange(GRID_W)[:, None]
    ck = np.arange(GRID_W)[None, :]
    win_start = np.clip(cq - NA_WIN_C // 2, 0, GRID_W - NA_WIN_C)
    in_win = (ck >= win_start) & (ck < win_start + NA_WIN_C)
    rel_c = np.clip(ck - cq, 1 - NA_WIN_C, NA_WIN_C - 1) + NA_WIN_C - 1
    tiles = jnp.where(in_win[None, None], rpb[:, :, rel_c].astype(F32), NEG)
    masked = jnp.full((rpb.shape[0], GRID_W, GRID_W), NEG, F32)
    blocks = [jnp.full((rpb.shape[0], ROW_BLK, NA_SLAB * ROW_BLK), NEG, F32)]
    for qb in range(nlat):
        s0 = int(np.clip(qb - 1, 0, nlat - NA_SLAB)) * NA_QROWS
        qrows = []
        for qr in range(NA_QROWS):
            r = qb * NA_QROWS + qr
            k0 = int(np.clip(r - kr // 2, 0, rows - kr))
            assert s0 <= k0 and k0 + kr <= s0 + NA_SLAB * NA_QROWS
            krows = []
            for kk in range(NA_SLAB * NA_QROWS):
                krow = s0 + kk
                if k0 <= krow < k0 + kr:
                    krows.append(tiles[:, krow - r + NA_WIN_R - 1])
                else:
                    krows.append(masked)
            qrows.append(jnp.concatenate(krows, axis=2))
        blocks.append(jnp.concatenate(qrows, axis=1))
    return jnp.stack(blocks, axis=1)


def _fourier_kernel(u_ref, cg_ref, sg_ref, dft_ref, o_ref, ab_s):
    l = u_ref.shape[0]

    @pl.when(pl.program_id(1) == 0)
    def _():
        for g in range(FNET_GROUPS):
            cols = slice(g * FNET_GW, (g + 1) * FNET_GW)
            u = u_ref[:, cols]
            ab_s[0:l, cols] = jnp.dot(u, cg_ref[...], preferred_element_type=F32).astype(BF16)
            ab_s[l:2 * l, cols] = jnp.dot(u, sg_ref[...], preferred_element_type=F32).astype(BF16)

    o_ref[...] = jnp.dot(dft_ref[...], ab_s[...], preferred_element_type=F32).astype(o_ref.dtype)


def _fourier(proj, cg, sg, dft):
    nb, l, _ = proj.shape
    tr = 3 * ROW_BLK
    w = FNET_GROUPS * FNET_GW
    return pl.pallas_call(
        _fourier_kernel,
        grid=(nb, l // tr),
        in_specs=[pl.BlockSpec((None, l, w), lambda b, i: (b, 0, C_FU * LANE // w)),
                  pl.BlockSpec(cg.shape, lambda b, i: (0, 0)),
                  pl.BlockSpec(sg.shape, lambda b, i: (0, 0)),
                  pl.BlockSpec((tr, 2 * l), lambda b, i: (i, 0))],
        out_specs=pl.BlockSpec((None, tr, w), lambda b, i: (b, i, 0)),
        out_shape=jax.ShapeDtypeStruct((nb, l, BRANCH_W), BF16),
        scratch_shapes=[pltpu.VMEM((2 * l, w), BF16)],
        compiler_params=_cparams(("arbitrary", "arbitrary")),
        name="fourier_mix",
    )(proj, cg, sg, dft)


def _dft_tables(n_ctx, seq):
    def cs(n):
        jk = (np.arange(n)[:, None] * np.arange(n)[None, :]) % n
        ang = 2.0 * np.pi * jk / n
        return np.cos(ang) / np.sqrt(n), np.sin(ang) / np.sqrt(n)

    cg, sg = cs(FNET_GW)
    l = n_ctx + seq
    cl = np.zeros((l, l))
    sl = np.zeros((l, l))
    cc, sc = cs(n_ctx)
    cs_, ss_ = cs(seq)
    cl[:n_ctx, :n_ctx], sl[:n_ctx, :n_ctx] = cc, sc
    cl[n_ctx:, n_ctx:], sl[n_ctx:, n_ctx:] = cs_, ss_
    dft = np.concatenate([cl, -sl], axis=1)
    return (jnp.asarray(cg, BF16), jnp.asarray(sg, BF16), jnp.asarray(dft, BF16))


def _merge_kernel(h_ref, y0_ref, y1_ref, y2_ref, y3_ref, w_ref, g0_ref, g1_ref, g2_ref, g3_ref, o_ref):
    h = h_ref[...]
    acc = None
    for i, (y_ref, g_ref) in enumerate(zip((y0_ref, y1_ref, y2_ref, y3_ref), (g0_ref, g1_ref, g2_ref, g3_ref))):
        gate = _sigmoid(jnp.dot(h, g_ref[...], preferred_element_type=F32))
        t = gate * jnp.dot(y_ref[...], w_ref[i], preferred_element_type=F32)
        acc = t if acc is None else acc + t
    o_ref[...] = acc.astype(o_ref.dtype)


def _merge(h2d, ys, w_up, w_in_b, li, tm=1024, tn=512):
    t, k = h2d.shape
    d = w_up.shape[3]
    gate0 = C_GATE * LANE // tn
    yspec = pl.BlockSpec((tm, BRANCH_W), lambda i, j: (i, 0))
    gspec = lambda br: pl.BlockSpec((None, k, tn), lambda i, j: (li, 0, gate0 + br * (d // tn) + j))
    return pl.pallas_call(
        _merge_kernel,
        grid=(t // tm, d // tn),
        in_specs=[pl.BlockSpec((tm, k), lambda i, j: (i, 0))] + [yspec] * 4
        + [pl.BlockSpec((None, 4, BRANCH_W, tn), lambda i, j: (li, 0, 0, j))] + [gspec(br) for br in range(4)],
        out_specs=pl.BlockSpec((tm, tn), lambda i, j: (i, j)),
        out_shape=jax.ShapeDtypeStruct((t, d), BF16),
        compiler_params=_cparams(("arbitrary", "arbitrary")),
        name="gated_merge",
    )(h2d, *ys, w_up, w_in_b, w_in_b, w_in_b, w_in_b)


def _ffn_kernel(x_ref, w1_ref, w3_ref, w2_ref, o_ref, acc_s):
    j = pl.program_id(1)
    x = x_ref[...]
    a = jnp.dot(x, w1_ref[...], preferred_element_type=F32)
    b = jnp.dot(x, w3_ref[...], preferred_element_type=F32)
    g = (_silu(a) * b).astype(BF16)
    part = jnp.dot(g, w2_ref[...], preferred_element_type=F32)

    @pl.when(j == 0)
    def _():
        acc_s[...] = part

    @pl.when(j > 0)
    def _():
        acc_s[...] += part

    @pl.when(j == pl.num_programs(1) - 1)
    def _():
        o_ref[...] = acc_s[...].astype(o_ref.dtype)


def _ffn(x, w1, w3, w2, li, tm=1024, tf=512):
    t, d = x.shape
    ff = w1.shape[2]
    return pl.pallas_call(
        _ffn_kernel,
        grid=(t // tm, ff // tf),
        in_specs=[pl.BlockSpec((tm, d), lambda i, j: (i, 0)),
                  pl.BlockSpec((None, d, tf), lambda i, j: (li, 0, j)),
                  pl.BlockSpec((None, d, tf), lambda i, j: (li, 0, j)),
                  pl.BlockSpec((None, tf, d), lambda i, j: (li, j, 0))],
        out_specs=pl.BlockSpec((tm, d), lambda i, j: (i, 0)),
        out_shape=jax.ShapeDtypeStruct((t, d), BF16),
        scratch_shapes=[pltpu.VMEM((tm, d), F32)],
        compiler_params=_cparams(("arbitrary", "arbitrary")),
        name="swiglu",
    )(x, w1, w3, w2)


MOE_TM = 512
MOE_TF = 1024
MOE_TN = 256


def _dispatch_kernel(nu_ref, tok_ref, h_hbm, o_ref, xg_s, sem):
    i = pl.program_id(0)
    tm = xg_s.shape[0]
    used = i < nu_ref[0]

    @pl.when(used)
    def _():
        def issue(r, carry):
            tok = tok_ref[i * tm + r]
            pltpu.make_async_copy(h_hbm.at[pl.ds(tok, 1), :], xg_s.at[pl.ds(r, 1), :], sem).start()
            return carry

        lax.fori_loop(0, tm, issue, 0, unroll=8)
        pltpu.make_async_copy(h_hbm.at[pl.ds(0, tm), :], xg_s, sem).wait()
        lo, hi = _unpack_halves(xg_s[...])
        half = xg_s.shape[1]
        o_ref[:, :half] = lo.astype(o_ref.dtype)
        o_ref[:, half:] = hi.astype(o_ref.dtype)

    @pl.when(jnp.logical_not(used))
    def _():
        o_ref[...] = jnp.zeros_like(o_ref)


def _moe_dispatch(hp, n_used, src_tok):
    p = src_tok.shape[0]
    half = hp.shape[1]
    d = 2 * half
    tm = MOE_TM
    grid_spec = pltpu.PrefetchScalarGridSpec(
        num_scalar_prefetch=2,
        grid=(p // tm,),
        in_specs=[pl.BlockSpec(memory_space=pl.ANY)],
        out_specs=pl.BlockSpec((tm, d), lambda i, nu, tok: (i, 0)),
        scratch_shapes=[pltpu.VMEM((tm, half), jnp.uint32), pltpu.SemaphoreType.DMA(())],
    )
    return pl.pallas_call(
        _dispatch_kernel,
        grid_spec=grid_spec,
        out_shape=jax.ShapeDtypeStruct((p, d), BF16),
        compiler_params=_cparams(("arbitrary",), row_dma=True),
        name="moe_dispatch",
    )(n_used, src_tok, hp)


def _expert_changed(te_ref, i):
    return jnp.logical_or(i == 0, te_ref[i] != te_ref[jnp.maximum(i - 1, 0)])


def _moe_up_kernel(te_ref, nu_ref, x_ref, w1_ref, w3_ref, o_ref, w1b_s, w3b_s):
    i = pl.program_id(1)

    @pl.when(_expert_changed(te_ref, i))
    def _():
        w1b_s[...] = w1_ref[...].astype(BF16)
        w3b_s[...] = w3_ref[...].astype(BF16)

    @pl.when(i < nu_ref[0])
    def _():
        x = x_ref[...]
        a = jnp.dot(x, w1b_s[...], preferred_element_type=F32)
        b = jnp.dot(x, w3b_s[...], preferred_element_type=F32)
        o_ref[...] = (_silu(a) * b).astype(o_ref.dtype)

    @pl.when(i >= nu_ref[0])
    def _():
        o_ref[...] = jnp.zeros_like(o_ref)


def _moe_down_kernel(te_ref, nu_ref, g_ref, w2a_ref, w2b_ref, o_ref, w2a_s, w2b_s):
    i = pl.program_id(1)

    @pl.when(_expert_changed(te_ref, i))
    def _():
        w2a_s[...] = w2a_ref[...].astype(BF16)
        w2b_s[...] = w2b_ref[...].astype(BF16)

    @pl.when(i < nu_ref[0])
    def _():
        g = g_ref[...]
        ya = jnp.dot(g, w2a_s[...], preferred_element_type=F32)
        yb = jnp.dot(g, w2b_s[...], preferred_element_type=F32)
        o_ref[...] = _pack_halves(jnp.concatenate([ya, yb], axis=1))

    @pl.when(i >= nu_ref[0])
    def _():
        o_ref[...] = jnp.zeros_like(o_ref)


def _moe_ffn(xs, tile_e, n_used, w1, w3, w2, li):
    p, d = xs.shape
    ff = w1.shape[3]
    tm, tf, tn = MOE_TM, MOE_TF, MOE_TN
    up_spec = pltpu.PrefetchScalarGridSpec(
        num_scalar_prefetch=2,
        grid=(ff // tf, p // tm),
        in_specs=[pl.BlockSpec((tm, d), lambda j, i, te, nu: (i, 0)),
                  pl.BlockSpec((None, None, d, tf), lambda j, i, te, nu: (li, te[i], 0, j)),
                  pl.BlockSpec((None, None, d, tf), lambda j, i, te, nu: (li, te[i], 0, j))],
        out_specs=pl.BlockSpec((tm, tf), lambda j, i, te, nu: (i, j)),
        scratch_shapes=[pltpu.VMEM((d, tf), BF16), pltpu.VMEM((d, tf), BF16)],
    )
    g = pl.pallas_call(
        _moe_up_kernel,
        grid_spec=up_spec,
        out_shape=jax.ShapeDtypeStruct((p, ff), BF16),
        compiler_params=_cparams(("arbitrary", "arbitrary")),
        name="moe_up",
    )(tile_e, n_used, xs, w1, w3)
    half_blks = d // 2 // tn
    down_spec = pltpu.PrefetchScalarGridSpec(
        num_scalar_prefetch=2,
        grid=(half_blks, p // tm),
        in_specs=[pl.BlockSpec((tm, ff), lambda j, i, te, nu: (i, 0)),
                  pl.BlockSpec((None, None, ff, tn), lambda j, i, te, nu: (li, te[i], 0, j)),
                  pl.BlockSpec((None, None, ff, tn), lambda j, i, te, nu: (li, te[i], 0, half_blks + j))],
        out_specs=pl.BlockSpec((tm, tn), lambda j, i, te, nu: (i, j)),
        scratch_shapes=[pltpu.VMEM((ff, tn), BF16), pltpu.VMEM((ff, tn), BF16)],
    )
    return pl.pallas_call(
        _moe_down_kernel,
        grid_spec=down_spec,
        out_shape=jax.ShapeDtypeStruct((p, d // 2), jnp.uint32),
        compiler_params=_cparams(("arbitrary", "arbitrary")),
        name="moe_down",
    )(tile_e, n_used, g, w2, w2)


def _combine_kernel(pos_ref, y_hbm, rt_ref, o_ref, g0_s, g1_s, sem):
    i = pl.program_id(0)
    tc = g0_s.shape[0]

    def issue(r, carry):
        t = i * tc + r
        pltpu.make_async_copy(y_hbm.at[pl.ds(pos_ref[2 * t], 1), :], g0_s.at[pl.ds(r, 1), :], sem.at[0]).start()
        pltpu.make_async_copy(y_hbm.at[pl.ds(pos_ref[2 * t + 1], 1), :], g1_s.at[pl.ds(r, 1), :], sem.at[1]).start()
        return carry

    lax.fori_loop(0, tc, issue, 0, unroll=8)
    pltpu.make_async_copy(y_hbm.at[pl.ds(0, tc), :], g0_s, sem.at[0]).wait()
    pltpu.make_async_copy(y_hbm.at[pl.ds(0, tc), :], g1_s, sem.at[1]).wait()
    w0 = rt_ref[:, 10:11]
    w1 = rt_ref[:, 11:12]
    lo0, hi0 = _unpack_halves(g0_s[...])
    lo1, hi1 = _unpack_halves(g1_s[...])
    half = g0_s.shape[1]
    o_ref[:, :half] = (w0 * lo0 + w1 * lo1).astype(o_ref.dtype)
    o_ref[:, half:] = (w0 * hi0 + w1 * hi1).astype(o_ref.dtype)


def _moe_combine(y, pos, route2d):
    t = route2d.shape[0]
    half = y.shape[1]
    d = 2 * half
    tc = ROW_BLK
    grid_spec = pltpu.PrefetchScalarGridSpec(
        num_scalar_prefetch=1,
        grid=(t // tc,),
        in_specs=[pl.BlockSpec(memory_space=pl.ANY),
                  pl.BlockSpec((tc, LANE), lambda i, pos: (i, 0))],
        out_specs=pl.BlockSpec((tc, d), lambda i, pos: (i, 0)),
        scratch_shapes=[pltpu.VMEM((tc, half), jnp.uint32), pltpu.VMEM((tc, half), jnp.uint32),
                        pltpu.SemaphoreType.DMA((2,))],
    )
    return pl.pallas_call(
        _combine_kernel,
        grid_spec=grid_spec,
        out_shape=jax.ShapeDtypeStruct((t, d), BF16),
        compiler_params=_cparams(("arbitrary",), row_dma=True),
        name="moe_combine",
    )(pos, y, route2d)


def _moe_plan(route, tm):
    t = route.shape[0]
    e_flat = route[:, 8:10].astype(jnp.int32).reshape(-1)
    na = 2 * t
    n_tiles = -(-(na + N_EXPERTS * (tm - 1)) // tm)
    p = n_tiles * tm
    onehot = (e_flat[:, None] == jnp.arange(N_EXPERTS)[None, :]).astype(jnp.int32)
    csum = jnp.cumsum(onehot, axis=0)
    counts = csum[-1]
    rank = jnp.sum((csum - onehot) * onehot, axis=1)
    tiles_e = (counts + tm - 1) // tm
    tile_end = jnp.cumsum(tiles_e)
    tile_start = tile_end - tiles_e
    n_used = tile_end[-1]
    cstart = jnp.cumsum(counts) - counts
    pos_of = (jnp.sum(onehot * tile_start[None, :], axis=1) * tm + rank).astype(jnp.int32)
    _, tok_sorted = lax.sort((pos_of, jnp.arange(na, dtype=jnp.int32) // 2), num_keys=1)
    tid = jnp.arange(n_tiles, dtype=jnp.int32)
    e_raw = jnp.sum((tid[:, None] >= tile_end[None, :]).astype(jnp.int32), axis=1)
    tile_e = jnp.minimum(jnp.sum((jnp.minimum(tid, n_used - 1)[:, None] >= tile_end[None, :]).astype(jnp.int32),
                                 axis=1), N_EXPERTS - 1)
    oh_t = (jnp.minimum(e_raw, N_EXPERTS - 1)[:, None] == jnp.arange(N_EXPERTS)[None, :]).astype(jnp.int32)
    t_start = jnp.sum(oh_t * tile_start[None, :], axis=1)
    t_count = jnp.where(tid < n_used, jnp.sum(oh_t * counts[None, :], axis=1), 0)
    t_cstart = jnp.sum(oh_t * cstart[None, :], axis=1)
    local = (tid - t_start)[:, None] * tm + jnp.arange(tm, dtype=jnp.int32)[None, :]
    valid = (local < t_count[:, None]).reshape(p)
    src = jnp.clip(t_cstart[:, None] + local, 0, na - 1).reshape(p)
    src_tok = jnp.where(valid, tok_sorted[src], 0)
    return tile_e.astype(jnp.int32), n_used.reshape(1).astype(jnp.int32), src_tok, pos_of


def _moe(hp, route, w1, w3, w2, li):
    nb, rows, half = hp.shape
    t = nb * rows
    route2d = route.reshape(t, LANE)
    tile_e, n_used, src_tok, pos_of = _moe_plan(route2d, MOE_TM)
    xs = _moe_dispatch(hp.reshape(t, half), n_used, src_tok)
    y = _moe_ffn(xs, tile_e, n_used, w1, w3, w2, li)
    return _moe_combine(y, pos_of, route2d).reshape(nb, rows, 2 * half)


def _rope_tables(seq):
    quarter = RET_DK // 4
    inv_freq = ROPE_BASE ** (-jnp.arange(quarter, dtype=F32) / quarter)
    t = jnp.arange(seq, dtype=jnp.int32)
    ang_r = (t // GRID_W).astype(F32)[:, None] * inv_freq[None]
    ang_c = (t % GRID_W).astype(F32)[:, None] * inv_freq[None]
    ang = jnp.concatenate([ang_r, ang_r, ang_c, ang_c], axis=1)
    rot = np.zeros((RET_DK, RET_DK), np.float32)
    for j in range(RET_DK):
        if (j % (2 * quarter)) < quarter:
            rot[j + quarter, j] = -1.0
        else:
            rot[j - quarter, j] = 1.0
    return jnp.cos(ang), jnp.sin(ang), jnp.asarray(rot, BF16)


PACK_TN = 512


def _pack_kernel(a_ref, b_ref, o_ref):
    blk = pl.program_id(1)
    main_blks = W_IN_SPLIT // PACK_TN
    sh = 2 * GLA_RANK

    @pl.when(blk < main_blks)
    def _():
        o_ref[...] = a_ref[...].T.astype(BF16)

    @pl.when(blk == main_blks)
    def _():
        row = lax.broadcasted_iota(jnp.int32, a_ref.shape, 0)
        o_ref[...] = jnp.where(row < sh, a_ref[...], 0.0).T.astype(BF16)

    @pl.when(blk > main_blks)
    def _():
        o_ref[...] = jnp.concatenate([a_ref[sh:, :], b_ref[:sh, :]], axis=0).T.astype(BF16)


def _pack_w_in(w):
    depth, d, n_src = w.shape
    wt = jnp.swapaxes(w, 1, 2)
    main_blks = W_IN_SPLIT // PACK_TN
    a_idx = lambda j: jnp.where(j <= main_blks, j, j - 1)
    last_b = (n_src - 1) // LANE
    return pl.pallas_call(
        _pack_kernel,
        grid=(depth, N_PROJ // PACK_TN),
        in_specs=[pl.BlockSpec((None, PACK_TN, d), lambda l, j: (l, a_idx(j), 0)),
                  pl.BlockSpec((None, LANE, d),
                               lambda l, j: (l, jnp.minimum((a_idx(j) + 1) * (PACK_TN // LANE), last_b), 0))],
        out_specs=pl.BlockSpec((None, d, PACK_TN), lambda l, j: (l, 0, j)),
        out_shape=jax.ShapeDtypeStruct((depth, d, N_PROJ), BF16),
        compiler_params=_cparams(("arbitrary", "arbitrary")),
        name="pack_w_in",
    )(wt, wt)


def kernel(x, c, ctx, c_ctx, w_ada, b_ada, norms, w_in, ret_decay, gla_w2, gla_b2, na_rpb, w_up, w_out,
           ffn_w1, ffn_w3, ffn_w2, moe_router, moe_router_b, moe_w1, moe_w3, moe_w2):
    nb, seq, d = x.shape
    n_ctx = ctx.shape[1]
    l = n_ctx + seq
    nblk = l // ROW_BLK
    assert n_ctx == ROW_BLK and seq % ROW_BLK == 0 and seq // ROW_BLK >= NA_SLAB and d == D_MODEL and nb < 16

    cvec = jnp.zeros((16, d), F32).at[:nb].set(c).at[nb].set(c_ctx)
    mod = _adaln(cvec, w_ada, b_ada)
    cos, sin, rot = _rope_tables(seq)
    cg, sg, dft = _dft_tables(n_ctx, seq)
    log_gamma = jnp.log1p(-jnp.exp(ret_decay.astype(F32)))

    w_in_b = _pack_w_in(w_in)
    w_up_b, w_out_b = w_up.astype(BF16), w_out.astype(BF16)
    ffn_b = (ffn_w1.astype(BF16), ffn_w3.astype(BF16), ffn_w2.astype(BF16))

    xs = jnp.concatenate([ctx, x], axis=1)
    _, h, _ = _resid(xs, None, None, None, mod[0], norms[0], None, x_off=0, f_off=0, nblk=nblk,
                     ctx_first=True, ib=0, sh_k=0, sc_k=1)
    for li in range(DEPTH):
        last = li == DEPTH - 1
        is_moe = li % 2 == 1
        h2d = h.reshape(nb * l, d)
        n_mix = C_GATE * LANE
        proj = _matmul(h2d, w_in_b, li, BF16, n_mix).reshape(nb, l, n_mix)
        w2p = jnp.zeros((2, LANE, GLA_HEADS * GLA_DK), F32)
        w2p = w2p.at[0, :GLA_RANK].set(gla_w2[li, 0]).at[1, GLA_RANK:2 * GLA_RANK].set(gla_w2[li, 1]).astype(BF16)
        y_ret = _retention(proj, log_gamma[li], cos, sin, rot)
        y_na = _na(proj, _na_bias(na_rpb[li], seq))
        y_fn = _fourier(proj, cg, sg, dft)
        y_gla = _gla(proj, w2p, gla_b2[li].reshape(2, 1, -1).astype(F32))
        ys = [y.reshape(nb * l, BRANCH_W) for y in (y_ret, y_na, y_fn, y_gla)]
        merged = _merge(h2d, ys, w_up_b, w_in_b, li).reshape(nb, l, d)
        out_proj = (w_out_b, li)

        router = None
        if is_moe:
            wr = jnp.zeros((d, LANE), F32).at[:, :N_EXPERTS].set(moe_router[li // 2])
            br = jnp.zeros((1, LANE), F32).at[0, :N_EXPERTS].set(moe_router_b[li // 2])
            router = (wr, br)
        if last:
            xs, h2, route = _resid(xs, merged, mod[li], norms[li], mod[li], norms[li], router, x_off=1, f_off=1,
                                   nblk=nblk - 1, ctx_first=False, ia=1, gate_k=2, ib=2, sh_k=3, sc_k=4,
                                   proj=out_proj)
        else:
            xs, h2, route = _resid(xs, merged, mod[li], norms[li], mod[li], norms[li], router, x_off=0, f_off=0,
                                   nblk=nblk, ctx_first=True, ia=1, gate_k=2, ib=2, sh_k=3, sc_k=4,
                                   proj=out_proj)
        rows = h2.shape[1]
        if is_moe:
            f = _moe(h2, route, moe_w1, moe_w3, moe_w2, li // 2)
        else:
            f = _ffn(h2.reshape(nb * rows, d), *ffn_b, li // 2).reshape(nb, rows, d)
        if last:
            xs, _, _ = _resid(xs, f, mod[li], norms[li], None, None, None, x_off=0, f_off=0, nblk=nblk - 1,
                              ctx_first=False, ia=3, gate_k=5)
        else:
            xs, h, _ = _resid(xs, f, mod[li], norms[li], mod[li + 1], norms[li + 1], None, x_off=0, f_off=0,
                              nblk=nblk, ctx_first=True, ia=3, gate_k=5, ib=0, sh_k=0, sc_k=1)
    return xs
```

```python
import functools
import math

import numpy as np
import jax
import jax.numpy as jnp
from jax import lax
from jax.experimental import pallas as pl
from jax.experimental.pallas import tpu as pltpu

F32 = jnp.float32
BF16 = jnp.bfloat16

D_MODEL = 2048
DEPTH = 4
GRID_W = 64
CTX_LEN = 256
EPS = 1e-6
ROPE_BASE = 10000.0
RET_HEADS, RET_DK = 4, 128
NA_HEADS, NA_DH = 4, 128
NA_WIN_R, NA_WIN_C = 8, 16
FNET_GROUPS, FNET_GW = 4, 128
GLA_HEADS, GLA_DK, GLA_DV, GLA_RANK, GLA_TAU = 4, 64, 128, 16, 16.0
GLA_CHUNK = 64
BRANCH_W = 512
N_EXPERTS = 8

LANE = 128
ROW_BLK = 256
VMEM_LIMIT = 56 * 1024 * 1024

C_RET_Q, C_RET_K, C_RET_V, C_RET_G = 0, 4, 8, 12
C_NA_Q, C_NA_K, C_NA_V = 16, 20, 24
C_FU = 28
C_GLA_Q, C_GLA_K, C_GLA_V, C_GLA_G, C_GLA_LR = 32, 34, 36, 40, 44
C_GATE = 48
N_PROJ = (C_GATE + 4 * D_MODEL // LANE) * LANE
W_IN_SPLIT = 5632

NEG = -1e30


def _cparams(sem, row_dma=False):
    return pltpu.CompilerParams(dimension_semantics=sem, vmem_limit_bytes=VMEM_LIMIT,
                                disable_bounds_checks=row_dma)


def _silu(x):
    return x * (1.0 / (1.0 + jnp.exp(-x)))


def _sigmoid(x):
    return 0.5 * jnp.tanh(0.5 * x) + 0.5


def _rms(x):
    return x * lax.rsqrt(jnp.mean(x * x, axis=-1, keepdims=True) + EPS)


HI_MASK = 0xFFFF0000


def _pack_halves(x):
    n = x.shape[1] // 2
    lo = pltpu.bitcast(x[:, :n].astype(BF16).astype(F32), jnp.uint32)
    hi = pltpu.bitcast(x[:, n:].astype(BF16).astype(F32), jnp.uint32)
    return (lo >> 16) | (hi & jnp.uint32(HI_MASK))


def _unpack_halves(p):
    return pltpu.bitcast(p << 16, F32), pltpu.bitcast(p & jnp.uint32(HI_MASK), F32)


def _ada_kernel(c_ref, w_ref, b_ref, o_ref):
    s = _silu(c_ref[...]).astype(BF16)
    o_ref[...] = jnp.dot(s, w_ref[...].astype(BF16), preferred_element_type=F32) + b_ref[...]


def _adaln(cvec, w_ada, b_ada):
    depth, d, n = w_ada.shape
    tn = 1024
    return pl.pallas_call(
        _ada_kernel,
        grid=(depth, n // tn),
        in_specs=[pl.BlockSpec((16, d), lambda l, j: (0, 0)),
                  pl.BlockSpec((None, d, tn), lambda l, j: (l, 0, j)),
                  pl.BlockSpec((None, 1, tn), lambda l, j: (l, 0, j))],
        out_specs=pl.BlockSpec((None, 16, tn), lambda l, j: (l, 0, j)),
        out_shape=jax.ShapeDtypeStruct((depth, 16, n), F32),
        compiler_params=_cparams(("arbitrary", "arbitrary")),
        name="adaln",
    )(cvec, w_ada, b_ada.reshape(depth, 1, n))


def _resid_kernel(*refs, nb, ctx_first, has_f, has_proj, has_next, has_router, ia, gate_k, ib, sh_k, sc_k):
    it = iter(refs)
    x_ref = next(it)
    f_ref = next(it) if has_f else None
    wf_ref = next(it) if has_proj else None
    moda_ref = next(it) if has_f else None
    na_ref = next(it) if has_f else None
    modb_ref = next(it) if has_next else None
    nbn_ref = next(it) if has_next else None
    wr_ref = next(it) if has_router else None
    br_ref = next(it) if has_router else None
    xo_ref = next(it) if has_f else None
    h_ref = next(it) if has_next else None
    rt_ref = next(it) if has_router else None

    b = pl.program_id(0)
    j = pl.program_id(1)
    row = jnp.where(j == 0, nb, b) if ctx_first else b
    d = x_ref.shape[-1]

    def modv(ref, k):
        return ref[pl.ds(row, 1), k * d:(k + 1) * d]

    x = x_ref[...]
    if has_f:
        if has_proj:
            f = jnp.dot(f_ref[...], wf_ref[...], preferred_element_type=F32)
        else:
            f = f_ref[...].astype(F32)
        x = x + modv(moda_ref, gate_k) * (_rms(f) * na_ref[ia:ia + 1, :])
        xo_ref[...] = x
    if has_next:
        h = _rms(x) * nbn_ref[ib:ib + 1, :]
        h = h * (1.0 + modv(modb_ref, sc_k)) + modv(modb_ref, sh_k)
        if not has_router:
            h_ref[...] = h.astype(h_ref.dtype)
        if has_router:
            h_ref[...] = _pack_halves(h)
            h_hi = h.astype(BF16)
            h_lo = (h - h_hi.astype(F32)).astype(BF16)
            w_hi, w_lo = wr_ref[0], wr_ref[1]
            logits = (jnp.dot(h_hi, w_hi, preferred_element_type=F32)
                      + jnp.dot(h_lo, w_hi, preferred_element_type=F32)
                      + jnp.dot(h_hi, w_lo, preferred_element_type=F32)) + br_ref[...]
            lane = lax.broadcasted_iota(jnp.int32, logits.shape, 1)
            lg = jnp.where(lane < N_EXPERTS, logits, -jnp.inf)
            v1 = jnp.max(lg, axis=-1, keepdims=True)
            i1 = jnp.min(jnp.where(lg == v1, lane, LANE), axis=-1, keepdims=True)
            lg2 = jnp.where(lane == i1, -jnp.inf, lg)
            v2 = jnp.max(lg2, axis=-1, keepdims=True)
            i2 = jnp.min(jnp.where(lg2 == v2, lane, LANE), axis=-1, keepdims=True)
            e2 = jnp.exp(v2 - v1)
            w1 = 1.0 / (1.0 + e2)
            w2 = e2 / (1.0 + e2)
            zero = jnp.zeros_like(logits)
            rt = (jnp.where(lane == 8, i1.astype(F32), zero) + jnp.where(lane == 9, i2.astype(F32), zero)
                  + jnp.where(lane == 10, w1, zero) + jnp.where(lane == 11, w2, zero))
            rt_ref[...] = rt


def _resid(x, f, mod_a, norms_a, mod_b, norms_b, router, *, x_off, f_off, nblk, ctx_first,
           ia=0, gate_k=0, ib=0, sh_k=0, sc_k=0, proj=None):
    nb, _, d = x.shape
    has_f = f is not None
    has_proj = proj is not None
    has_next = mod_b is not None
    has_router = router is not None
    rows = nblk * ROW_BLK
    blk = lambda off: pl.BlockSpec((None, ROW_BLK, d), lambda b, j: (b, j + off, 0))
    full = lambda a: pl.BlockSpec(a.shape, lambda b, j: (0,) * a.ndim)
    ins, specs = [x], [blk(x_off)]
    if has_f:
        ins += [f]
        specs += [blk(f_off)]
        if has_proj:
            w, li = proj
            ins += [w]
            specs += [pl.BlockSpec((None, d, d), lambda b, j: (li, 0, 0))]
        ins += [mod_a, norms_a]
        specs += [full(mod_a), full(norms_a)]
    if has_next:
        ins += [mod_b, norms_b]
        specs += [full(mod_b), full(norms_b)]
    if has_router:
        ins += list(router)
        specs += [full(router[0]), full(router[1])]
    outs, ospecs = [], []
    if has_f:
        outs.append(jax.ShapeDtypeStruct((nb, rows, d), F32))
        ospecs.append(blk(0))
    if has_next and not has_router:
        outs.append(jax.ShapeDtypeStruct((nb, rows, d), BF16))
        ospecs.append(blk(0))
    if has_next and has_router:
        outs.append(jax.ShapeDtypeStruct((nb, rows, d // 2), jnp.uint32))
        ospecs.append(pl.BlockSpec((None, ROW_BLK, d // 2), lambda b, j: (b, j, 0)))
    if has_router:
        outs.append(jax.ShapeDtypeStruct((nb, rows, LANE), F32))
        ospecs.append(pl.BlockSpec((None, ROW_BLK, LANE), lambda b, j: (b, j, 0)))
    res = pl.pallas_call(
        functools.partial(_resid_kernel, nb=nb, ctx_first=ctx_first, has_f=has_f, has_proj=has_proj, has_next=has_next,
                          has_router=has_router, ia=ia, gate_k=gate_k, ib=ib, sh_k=sh_k, sc_k=sc_k),
        grid=(nb, nblk),
        in_specs=specs, out_specs=ospecs, out_shape=outs,
        compiler_params=_cparams(("arbitrary", "arbitrary")),
        name="resid_norm",
    )(*ins)
    res = list(res)
    x_new = res.pop(0) if has_f else None
    h = res.pop(0) if has_next else None
    rt = res.pop(0) if has_router else None
    return x_new, h, rt


def _mm_kernel(x_ref, w_ref, o_ref):
    o_ref[...] = jnp.dot(x_ref[...], w_ref[...], preferred_element_type=F32).astype(o_ref.dtype)


def _matmul(x, w, li, out_dtype, n, tm=1024, tn=1024):
    m, k = x.shape
    return pl.pallas_call(
        _mm_kernel,
        grid=(m // tm, n // tn),
        in_specs=[pl.BlockSpec((tm, k), lambda i, j: (i, 0)),
                  pl.BlockSpec((None, k, tn), lambda i, j: (li, 0, j))],
        out_specs=pl.BlockSpec((tm, tn), lambda i, j: (i, j)),
        out_shape=jax.ShapeDtypeStruct((m, n), out_dtype),
        compiler_params=_cparams(("arbitrary", "arbitrary")),
        name="matmul",
    )(x, w)


def _ret_kernel(lg_ref, q_ref, k_ref, v_ref, g_ref, cos_ref, sin_ref, rot_ref, o_ref, q_s, k_s, acc_s):
    h = pl.program_id(1)
    lgf = lg_ref[0, h]
    lgb = lg_ref[1, h]
    c = ROW_BLK
    n = q_ref.shape[0] // c
    scale = RET_DK ** -0.5

    q_s[0:c, :] = (q_ref[0:c, :].astype(F32) * scale).astype(BF16)
    k_s[0:c, :] = k_ref[0:c, :]
    rot = rot_ref[...]
    for i in range(1, n):
        rows = slice(i * c, (i + 1) * c)
        trow = slice((i - 1) * c, i * c)
        cs, sn = cos_ref[trow, :], sin_ref[trow, :]
        qa, ka = q_ref[rows, :], k_ref[rows, :]
        qr = jnp.dot(qa, rot, preferred_element_type=F32)
        kr = jnp.dot(ka, rot, preferred_element_type=F32)
        q_s[rows, :] = ((qa.astype(F32) * cs + qr * sn) * scale).astype(BF16)
        k_s[rows, :] = (ka.astype(F32) * cs + kr * sn).astype(BF16)

    ti = lax.broadcasted_iota(jnp.int32, (c, c), 0)
    si = lax.broadcasted_iota(jnp.int32, (c, c), 1)
    dd = (ti - si).astype(F32)
    dmat = jnp.where(dd > 0, jnp.exp(lgf * dd), jnp.where(dd < 0, jnp.exp(-lgb * dd), 2.0))
    tcol = lax.broadcasted_iota(jnp.int32, (c, 1), 0).astype(F32)
    q_f = jnp.exp(lgf * (tcol + 1.0))
    q_b = jnp.exp(lgb * (c - tcol))
    k_f = jnp.exp(lgf * (c - 1.0 - tcol))
    k_b = jnp.exp(lgb * tcol)
    g_f = jnp.exp(lgf * c)
    g_b = jnp.exp(lgb * c)

    def kv_state(kc, vc, kdec):
        kd = (kc.astype(F32) * kdec).astype(BF16)
        return lax.dot_general(kd, vc, (((0,), (0,)), ((), ())), preferred_element_type=F32)

    s_f = jnp.zeros((RET_DK, q_ref.shape[1]), F32)
    for i in range(n):
        rows = slice(i * c, (i + 1) * c)
        qc, kc, vc = q_s[rows, :], k_s[rows, :], v_ref[rows, :]
        sc = lax.dot_general(qc, kc, (((1,), (1,)), ((), ())), preferred_element_type=F32)
        o = jnp.dot((sc * dmat).astype(BF16), vc, preferred_element_type=F32)
        if i > 0:
            qd = (qc.astype(F32) * q_f).astype(BF16)
            o = o + jnp.dot(qd, s_f.astype(BF16), preferred_element_type=F32)
        acc_s[rows, :] = o
        if i < n - 1:
            s_f = g_f * s_f + kv_state(kc, vc, k_f)

    s_b = kv_state(k_s[0:c, :], v_ref[0:c, :], k_b)
    for i in range(n - 1, 0, -1):
        rows = slice(i * c, (i + 1) * c)
        qc, kc, vc = q_s[rows, :], k_s[rows, :], v_ref[rows, :]
        qd = (qc.astype(F32) * q_b).astype(BF16)
        acc_s[rows, :] = acc_s[rows, :] + jnp.dot(qd, s_b.astype(BF16), preferred_element_type=F32)
        if i > 1:
            s_b = g_b * s_b + kv_state(kc, vc, k_b)

    for i in range(n):
        rows = slice(i * c, (i + 1) * c)
        o_ref[rows, :] = (_rms(acc_s[rows, :]) * _silu(g_ref[rows, :].astype(F32))).astype(o_ref.dtype)


def _retention(proj, lg, cos, sin, rot):
    nb, l, _ = proj.shape
    col = lambda c0: pl.BlockSpec((None, l, LANE), lambda b, h: (b, 0, c0 + h))
    const = lambda a: pl.BlockSpec(a.shape, lambda b, h: (0,) * a.ndim)
    return pl.pallas_call(
        _ret_kernel,
        grid=(nb, RET_HEADS),
        in_specs=[pl.BlockSpec(memory_space=pltpu.SMEM),
                  col(C_RET_Q), col(C_RET_K), col(C_RET_V), col(C_RET_G),
                  const(cos), const(sin), const(rot)],
        out_specs=pl.BlockSpec((None, l, LANE), lambda b, h: (b, 0, h)),
        out_shape=jax.ShapeDtypeStruct((nb, l, BRANCH_W), BF16),
        scratch_shapes=[pltpu.VMEM((l, LANE), BF16), pltpu.VMEM((l, LANE), BF16), pltpu.VMEM((l, LANE), F32)],
        compiler_params=_cparams(("arbitrary", "arbitrary")),
        name="retention",
    )(lg, proj, proj, proj, proj, cos, sin, rot)


def _gla_kernel(q_ref, k_ref, v_ref, g_ref, lr_ref, w2_ref, b2_ref, o_ref,
                accf_s, accb_s, st_s, qd_s, ki_s, ke_s, dl_s, *, n_ctx_chunks):
    c = GLA_CHUNK
    l = q_ref.shape[0]
    n = l // c
    cpb = ROW_BLK // c
    kw = GLA_HEADS * GLA_DK
    scale = GLA_DK ** -0.5

    ti = lax.broadcasted_iota(jnp.int32, (ROW_BLK, ROW_BLK), 0)
    si = lax.broadcasted_iota(jnp.int32, (ROW_BLK, ROW_BLK), 1)
    same = ((ti // c) == (si // c)).astype(F32)
    cum_lo = (same * (ti >= si).astype(F32)).astype(BF16)
    cum_up = (same * (ti <= si).astype(F32)).astype(BF16)

    def prep(bi, carry):
        rows = pl.ds(pl.multiple_of(bi * ROW_BLK, ROW_BLK), ROW_BLK)
        lr = lr_ref[rows, :]
        qf = q_ref[rows, :].astype(F32) * scale
        kf = k_ref[rows, :].astype(F32)
        for d, cum, last in ((0, cum_lo, c - 1), (1, cum_up, 0)):
            z = jnp.dot(lr, w2_ref[d], preferred_element_type=F32) + b2_ref[d]
            la = (jnp.minimum(z, 0.0) - jnp.log(1.0 + jnp.exp(-jnp.abs(z)))) * (1.0 / GLA_TAU)
            la_hi = la.astype(BF16)
            la_lo = (la - la_hi.astype(F32)).astype(BF16)
            bcum = (jnp.dot(cum, la_hi, preferred_element_type=F32)
                    + jnp.dot(cum, la_lo, preferred_element_type=F32))
            bl = jnp.concatenate([jnp.broadcast_to(bcum[cc * c + last:cc * c + last + 1, :], (c, kw))
                                  for cc in range(cpb)], axis=0)
            qd_s[d, rows, :] = (qf * jnp.exp(bcum)).astype(BF16)
            ki_s[d, rows, :] = (kf * jnp.exp(-bcum)).astype(BF16)
            ke_s[d, rows, :] = (kf * jnp.exp(bl - bcum)).astype(BF16)
            dec = jnp.exp(bl)
            for cc in range(cpb):
                dl_s[d, pl.ds(pl.multiple_of((bi * cpb + cc) * 8, 8), 8), :] = dec[cc * c:cc * c + 8, :]
        return carry

    lax.fori_loop(0, l // ROW_BLK, prep, 0)

    t4 = lax.broadcasted_iota(jnp.int32, (GLA_HEADS * c, c), 0) % c
    s4 = lax.broadcasted_iota(jnp.int32, (GLA_HEADS * c, c), 1)
    mask_lo = (t4 >= s4).astype(F32)
    mask_up = (t4 <= s4).astype(F32)
    srow = lax.broadcasted_iota(jnp.int32, (GLA_HEADS * c, kw), 0) // c
    slane = lax.broadcasted_iota(jnp.int32, (GLA_HEADS * c, kw), 1) // GLA_DK
    stack_mask = srow == slane
    lane_head = lax.broadcasted_iota(jnp.int32, (GLA_DV, kw), 1) // GLA_DK
    lanes = (((1,), (1,)), ((), ()))
    st_s[...] = jnp.zeros_like(st_s)

    def chunk(ci, d, mask, acc_ref):
        rows = pl.ds(pl.multiple_of(ci * c, c), c)
        q_dec, k_inv, k_end = qd_s[d, rows, :], ki_s[d, rows, :], ke_s[d, rows, :]
        vc = v_ref[rows, :]
        dec = dl_s[d, pl.ds(pl.multiple_of(ci * 8, 8), 1), :]
        q_stack = jnp.where(stack_mask, jnp.concatenate([q_dec] * GLA_HEADS, axis=0), jnp.zeros((), BF16))
        sc = (lax.dot_general(q_stack, k_inv, lanes, preferred_element_type=F32) * mask).astype(BF16)
        st = st_s[d]
        inter = lax.dot_general(q_stack, st.astype(BF16), lanes, preferred_element_type=F32)
        acc_ref[rows, :] = jnp.concatenate(
            [jnp.dot(sc[h * c:(h + 1) * c, :], vc[:, h * GLA_DV:(h + 1) * GLA_DV], preferred_element_type=F32)
             + inter[h * c:(h + 1) * c, :] for h in range(GLA_HEADS)], axis=1)
        full = lax.dot_general(vc, k_end, (((0,), (0,)), ((), ())), preferred_element_type=F32)
        comp = full[0:GLA_DV, :]
        for h in range(1, GLA_HEADS):
            comp = jnp.where(lane_head == h, full[h * GLA_DV:(h + 1) * GLA_DV, :], comp)
        st_s[d] = st * dec + comp

    def step(i, carry):
        chunk(i, 0, mask_lo, accf_s)
        chunk(jnp.where(i < n_ctx_chunks, n_ctx_chunks - 1 - i, n + n_ctx_chunks - 1 - i), 1, mask_up, accb_s)
        return carry

    lax.fori_loop(0, n, step, 0, unroll=4)

    for i in range(l // ROW_BLK):
        rows = slice(i * ROW_BLK, (i + 1) * ROW_BLK)
        o = accf_s[rows, :] + accb_s[rows, :]
        o = jnp.concatenate([_rms(o[:, h * GLA_DV:(h + 1) * GLA_DV]) for h in range(GLA_HEADS)], axis=1)
        o_ref[rows, :] = (o * _silu(g_ref[rows, :].astype(F32))).astype(o_ref.dtype)


def _gla(proj, w2p, b2):
    nb, l, _ = proj.shape
    kw, vw = GLA_HEADS * GLA_DK, GLA_HEADS * GLA_DV
    col = lambda c0, w: pl.BlockSpec((None, l, w), lambda b: (b, 0, c0 * LANE // w))
    const = lambda a: pl.BlockSpec(a.shape, lambda b: (0,) * a.ndim)
    return pl.pallas_call(
        functools.partial(_gla_kernel, n_ctx_chunks=CTX_LEN // GLA_CHUNK),
        grid=(nb,),
        in_specs=[col(C_GLA_Q, kw), col(C_GLA_K, kw), col(C_GLA_V, vw), col(C_GLA_G, vw), col(C_GLA_LR, LANE),
                  const(w2p), const(b2)],
        out_specs=pl.BlockSpec((None, l, vw), lambda b: (b, 0, 0)),
        out_shape=jax.ShapeDtypeStruct((nb, l, BRANCH_W), BF16),
        scratch_shapes=[pltpu.VMEM((l, vw), F32), pltpu.VMEM((l, vw), F32),
                        pltpu.VMEM((2, GLA_DV, kw), F32),
                        pltpu.VMEM((2, l, kw), BF16), pltpu.VMEM((2, l, kw), BF16), pltpu.VMEM((2, l, kw), BF16),
                        pltpu.VMEM((2, l // GLA_CHUNK * 8, kw), F32)],
        compiler_params=_cparams(("arbitrary",)),
        name="gla",
    )(proj, proj, proj, proj, proj, w2p, b2)


NA_QROWS = ROW_BLK // GRID_W
NA_SLAB = 3


def _na_kernel(q_ref, kc_ref, k0_ref, k1_ref, k2_ref, vc_ref, v0_ref, v1_ref, v2_ref, bias_ref, o_ref):
    scale = NA_DH ** -0.5
    dims = (((1,), (1,)), ((), ()))
    for h in range(NA_HEADS):
        cols = slice(h * NA_DH, (h + 1) * NA_DH)
        q = q_ref[:, cols]
        s = [lax.dot_general(q, kc_ref[:, cols], dims, preferred_element_type=F32) * scale]
        for j, kr in enumerate((k0_ref, k1_ref, k2_ref)):
            sj = lax.dot_general(q, kr[:, cols], dims, preferred_element_type=F32) * scale
            s.append(sj + bias_ref[h, :, j * ROW_BLK:(j + 1) * ROW_BLK])
        m = functools.reduce(jnp.maximum, [jnp.max(x, axis=-1, keepdims=True) for x in s])
        p = [jnp.exp(x - m) for x in s]
        den = functools.reduce(lambda a, b: a + b, [jnp.sum(x, axis=-1, keepdims=True) for x in p])
        o = jnp.zeros((q.shape[0], NA_DH), F32)
        for pj, vr in zip(p, (vc_ref, v0_ref, v1_ref, v2_ref)):
            o = o + jnp.dot(pj.astype(BF16), vr[:, cols], preferred_element_type=F32)
        o_ref[:, cols] = (o / den).astype(o_ref.dtype)


def _na_slab_start(qb, n_lat_blk):
    return jnp.clip(qb - 2, 0, n_lat_blk - NA_SLAB)


def _na(proj, bias):
    nb, l, _ = proj.shape
    nblk = l // ROW_BLK
    nlat = nblk - 1
    w = NA_HEADS * NA_DH
    blk = lambda c0, rowfn: pl.BlockSpec((None, ROW_BLK, w), lambda qb, b: (b, rowfn(qb), c0 * LANE // w))
    slab = lambda j: (lambda qb: 1 + _na_slab_start(qb, nlat) + j)
    return pl.pallas_call(
        _na_kernel,
        grid=(nblk, nb),
        in_specs=[blk(C_NA_Q, lambda qb: qb),
                  blk(C_NA_K, lambda qb: 0), blk(C_NA_K, slab(0)), blk(C_NA_K, slab(1)), blk(C_NA_K, slab(2)),
                  blk(C_NA_V, lambda qb: 0), blk(C_NA_V, slab(0)), blk(C_NA_V, slab(1)), blk(C_NA_V, slab(2)),
                  pl.BlockSpec((NA_HEADS, None, ROW_BLK, NA_SLAB * ROW_BLK), lambda qb, b: (0, qb, 0, 0))],
        out_specs=pl.BlockSpec((None, ROW_BLK, w), lambda qb, b: (b, qb, 0)),
        out_shape=jax.ShapeDtypeStruct((nb, l, BRANCH_W), BF16),
        compiler_params=_cparams(("arbitrary", "arbitrary")),
        name="neighbourhood_attention",
    )(*([proj] * 9), bias)


def _na_bias(rpb, seq):
    rows = seq // GRID_W
    kr = min(NA_WIN_R, rows)
    nlat = seq // ROW_BLK
    cq = np.arange(GRID_W)[:, None]
    ck = np.arange(GRID_W)[None, :]
    win_start = np.clip(cq - NA_WIN_C // 2, 0, GRID_W - NA_WIN_C)
    in_win = (ck >= win_start) & (ck < win_start + NA_WIN_C)
    rel_c = np.clip(ck - cq, 1 - NA_WIN_C, NA_WIN_C - 1) + NA_WIN_C - 1
    tiles = jnp.where(in_win[None, None], rpb[:, :, rel_c].astype(F32), NEG)
    masked = jnp.full((rpb.shape[0], GRID_W, GRID_W), NEG, F32)
    blocks = [jnp.full((rpb.shape[0], ROW_BLK, NA_SLAB * ROW_BLK), NEG, F32)]
    for qb in range(nlat):
        s0 = int(np.clip(qb - 1, 0, nlat - NA_SLAB)) * NA_QROWS
        qrows = []
        for qr in range(NA_QROWS):
            r = qb * NA_QROWS + qr
            k0 = int(np.clip(r - kr // 2, 0, rows - kr))
            assert s0 <= k0 and k0 + kr <= s0 + NA_SLAB * NA_QROWS
            krows = []
            for kk in range(NA_SLAB * NA_QROWS):
                krow = s0 + kk
                if k0 <= krow < k0 + kr:
                    krows.append(tiles[:, krow - r + NA_WIN_R - 1])
                else:
                    krows.append(masked)
            qrows.append(jnp.concatenate(krows, axis=2))
        blocks.append(jnp.concatenate(qrows, axis=1))
    return jnp.stack(blocks, axis=1)


def _fourier_kernel(u_ref, cg_ref, sg_ref, dft_ref, o_ref, ab_s):
    l = u_ref.shape[0]

    @pl.when(pl.program_id(1) == 0)
    def _():
        for g in range(FNET_GROUPS):
            cols = slice(g * FNET_GW, (g + 1) * FNET_GW)
            u = u_ref[:, cols]
            ab_s[0:l, cols] = jnp.dot(u, cg_ref[...], preferred_element_type=F32).astype(BF16)
            ab_s[l:2 * l, cols] = jnp.dot(u, sg_ref[...], preferred_element_type=F32).astype(BF16)

    o_ref[...] = jnp.dot(dft_ref[...], ab_s[...], preferred_element_type=F32).astype(o_ref.dtype)


def _fourier(proj, cg, sg, dft):
    nb, l, _ = proj.shape
    tr = 3 * ROW_BLK
    w = FNET_GROUPS * FNET_GW
    return pl.pallas_call(
        _fourier_kernel,
        grid=(nb, l // tr),
        in_specs=[pl.BlockSpec((None, l, w), lambda b, i: (b, 0, C_FU * LANE // w)),
                  pl.BlockSpec(cg.shape, lambda b, i: (0, 0)),
                  pl.BlockSpec(sg.shape, lambda b, i: (0, 0)),
                  pl.BlockSpec((tr, 2 * l), lambda b, i: (i, 0))],
        out_specs=pl.BlockSpec((None, tr, w), lambda b, i: (b, i, 0)),
        out_shape=jax.ShapeDtypeStruct((nb, l, BRANCH_W), BF16),
        scratch_shapes=[pltpu.VMEM((2 * l, w), BF16)],
        compiler_params=_cparams(("arbitrary", "arbitrary")),
        name="fourier_mix",
    )(proj, cg, sg, dft)


def _dft_tables(n_ctx, seq):
    def cs(n):
        jk = (np.arange(n)[:, None] * np.arange(n)[None, :]) % n
        ang = 2.0 * np.pi * jk / n
        return np.cos(ang) / np.sqrt(n), np.sin(ang) / np.sqrt(n)

    cg, sg = cs(FNET_GW)
    l = n_ctx + seq
    cl = np.zeros((l, l))
    sl = np.zeros((l, l))
    cc, sc = cs(n_ctx)
    cs_, ss_ = cs(seq)
    cl[:n_ctx, :n_ctx], sl[:n_ctx, :n_ctx] = cc, sc
    cl[n_ctx:, n_ctx:], sl[n_ctx:, n_ctx:] = cs_, ss_
    dft = np.concatenate([cl, -sl], axis=1)
    return (jnp.asarray(cg, BF16), jnp.asarray(sg, BF16), jnp.asarray(dft, BF16))


def _merge_kernel(h_ref, y0_ref, y1_ref, y2_ref, y3_ref, w_ref, g0_ref, g1_ref, g2_ref, g3_ref, o_ref):
    h = h_ref[...]
    acc = None
    for i, (y_ref, g_ref) in enumerate(zip((y0_ref, y1_ref, y2_ref, y3_ref), (g0_ref, g1_ref, g2_ref, g3_ref))):
        gate = _sigmoid(jnp.dot(h, g_ref[...], preferred_element_type=F32))
        t = gate * jnp.dot(y_ref[...], w_ref[i], preferred_element_type=F32)
        acc = t if acc is None else acc + t
    o_ref[...] = acc.astype(o_ref.dtype)


def _merge(h2d, ys, w_up, w_in_b, li, tm=1024, tn=512):
    t, k = h2d.shape
    d = w_up.shape[3]
    gate0 = C_GATE * LANE // tn
    yspec = pl.BlockSpec((tm, BRANCH_W), lambda i, j: (i, 0))
    gspec = lambda br: pl.BlockSpec((None, k, tn), lambda i, j: (li, 0, gate0 + br * (d // tn) + j))
    return pl.pallas_call(
        _merge_kernel,
        grid=(t // tm, d // tn),
        in_specs=[pl.BlockSpec((tm, k), lambda i, j: (i, 0))] + [yspec] * 4
        + [pl.BlockSpec((None, 4, BRANCH_W, tn), lambda i, j: (li, 0, 0, j))] + [gspec(br) for br in range(4)],
        out_specs=pl.BlockSpec((tm, tn), lambda i, j: (i, j)),
        out_shape=jax.ShapeDtypeStruct((t, d), BF16),
        compiler_params=_cparams(("arbitrary", "arbitrary")),
        name="gated_merge",
    )(h2d, *ys, w_up, w_in_b, w_in_b, w_in_b, w_in_b)


def _ffn_kernel(x_ref, w1_ref, w3_ref, w2_ref, o_ref, acc_s):
    j = pl.program_id(1)
    x = x_ref[...]
    a = jnp.dot(x, w1_ref[...], preferred_element_type=F32)
    b = jnp.dot(x, w3_ref[...], preferred_element_type=F32)
    g = (_silu(a) * b).astype(BF16)
    part = jnp.dot(g, w2_ref[...], preferred_element_type=F32)

    @pl.when(j == 0)
    def _():
        acc_s[...] = part

    @pl.when(j > 0)
    def _():
        acc_s[...] += part

    @pl.when(j == pl.num_programs(1) - 1)
    def _():
        o_ref[...] = acc_s[...].astype(o_ref.dtype)


def _ffn(x, w1, w3, w2, li, tm=1024, tf=512):
    t, d = x.shape
    ff = w1.shape[2]
    return pl.pallas_call(
        _ffn_kernel,
        grid=(t // tm, ff // tf),
        in_specs=[pl.BlockSpec((tm, d), lambda i, j: (i, 0)),
                  pl.BlockSpec((None, d, tf), lambda i, j: (li, 0, j)),
                  pl.BlockSpec((None, d, tf), lambda i, j: (li, 0, j)),
                  pl.BlockSpec((None, tf, d), lambda i, j: (li, j, 0))],
        out_specs=pl.BlockSpec((tm, d), lambda i, j: (i, 0)),
        out_shape=jax.ShapeDtypeStruct((t, d), BF16),
        scratch_shapes=[pltpu.VMEM((tm, d), F32)],
        compiler_params=_cparams(("arbitrary", "arbitrary")),
        name="swiglu",
    )(x, w1, w3, w2)


MOE_TM = 512
MOE_TF = 1024
MOE_TN = 512


def _dispatch_kernel(nu_ref, tok_ref, h_hbm, o_ref, xg_s, sem):
    i = pl.program_id(0)
    tm = xg_s.shape[0]
    used = i < nu_ref[0]

    @pl.when(used)
    def _():
        def issue(r, carry):
            tok = tok_ref[i * tm + r]
            pltpu.make_async_copy(h_hbm.at[pl.ds(tok, 1), :], xg_s.at[pl.ds(r, 1), :], sem).start()
            return carry

        lax.fori_loop(0, tm, issue, 0, unroll=8)
        pltpu.make_async_copy(h_hbm.at[pl.ds(0, tm), :], xg_s, sem).wait()
        lo, hi = _unpack_halves(xg_s[...])
        half = xg_s.shape[1]
        o_ref[:, :half] = lo.astype(o_ref.dtype)
        o_ref[:, half:] = hi.astype(o_ref.dtype)

    @pl.when(jnp.logical_not(used))
    def _():
        o_ref[...] = jnp.zeros_like(o_ref)


def _moe_dispatch(hp, n_used, src_tok):
    p = src_tok.shape[0]
    half = hp.shape[1]
    d = 2 * half
    tm = MOE_TM
    grid_spec = pltpu.PrefetchScalarGridSpec(
        num_scalar_prefetch=2,
        grid=(p // tm,),
        in_specs=[pl.BlockSpec(memory_space=pl.ANY)],
        out_specs=pl.BlockSpec((tm, d), lambda i, nu, tok: (i, 0)),
        scratch_shapes=[pltpu.VMEM((tm, half), jnp.uint32), pltpu.SemaphoreType.DMA(())],
    )
    return pl.pallas_call(
        _dispatch_kernel,
        grid_spec=grid_spec,
        out_shape=jax.ShapeDtypeStruct((p, d), BF16),
        compiler_params=_cparams(("arbitrary",), row_dma=True),
        name="moe_dispatch",
    )(n_used, src_tok, hp)


def _expert_changed(te_ref, i):
    return jnp.logical_or(i == 0, te_ref[i] != te_ref[jnp.maximum(i - 1, 0)])


def _moe_up_kernel(te_ref, nu_ref, x_ref, w1_ref, w3_ref, o_ref, w1b_s, w3b_s):
    i = pl.program_id(1)

    @pl.when(_expert_changed(te_ref, i))
    def _():
        w1b_s[...] = w1_ref[...].astype(BF16)
        w3b_s[...] = w3_ref[...].astype(BF16)

    @pl.when(i < nu_ref[0])
    def _():
        x = x_ref[...]
        a = jnp.dot(x, w1b_s[...], preferred_element_type=F32)
        b = jnp.dot(x, w3b_s[...], preferred_element_type=F32)
        o_ref[...] = (_silu(a) * b).astype(o_ref.dtype)

    @pl.when(i >= nu_ref[0])
    def _():
        o_ref[...] = jnp.zeros_like(o_ref)


def _moe_down_kernel(te_ref, nu_ref, g_ref, w2a_ref, w2b_ref, o_ref, w2a_s, w2b_s):
    i = pl.program_id(1)

    @pl.when(_expert_changed(te_ref, i))
    def _():
        w2a_s[...] = w2a_ref[...].astype(BF16)
        w2b_s[...] = w2b_ref[...].astype(BF16)

    @pl.when(i < nu_ref[0])
    def _():
        g = g_ref[...]
        ya = jnp.dot(g, w2a_s[...], preferred_element_type=F32)
        yb = jnp.dot(g, w2b_s[...], preferred_element_type=F32)
        o_ref[...] = _pack_halves(jnp.concatenate([ya, yb], axis=1))

    @pl.when(i >= nu_ref[0])
    def _():
        o_ref[...] = jnp.zeros_like(o_ref)


def _moe_ffn(xs, tile_e, n_used, w1, w3, w2, li):
    p, d = xs.shape
    ff = w1.shape[3]
    tm, tf, tn = MOE_TM, MOE_TF, MOE_TN
    up_spec = pltpu.PrefetchScalarGridSpec(
        num_scalar_prefetch=2,
        grid=(ff // tf, p // tm),
        in_specs=[pl.BlockSpec((tm, d), lambda j, i, te, nu: (i, 0)),
                  pl.BlockSpec((None, None, d, tf), lambda j, i, te, nu: (li, te[i], 0, j)),
                  pl.BlockSpec((None, None, d, tf), lambda j, i, te, nu: (li, te[i], 0, j))],
        out_specs=pl.BlockSpec((tm, tf), lambda j, i, te, nu: (i, j)),
        scratch_shapes=[pltpu.VMEM((d, tf), BF16), pltpu.VMEM((d, tf), BF16)],
    )
    g = pl.pallas_call(
        _moe_up_kernel,
        grid_spec=up_spec,
        out_shape=jax.ShapeDtypeStruct((p, ff), BF16),
        compiler_params=_cparams(("arbitrary", "arbitrary")),
        name="moe_up",
    )(tile_e, n_used, xs, w1, w3)
    half_blks = d // 2 // tn
    down_spec = pltpu.PrefetchScalarGridSpec(
        num_scalar_prefetch=2,
        grid=(half_blks, p // tm),
        in_specs=[pl.BlockSpec((tm, ff), lambda j, i, te, nu: (i, 0)),
                  pl.BlockSpec((None, None, ff, tn), lambda j, i, te, nu: (li, te[i], 0, j)),
                  pl.BlockSpec((None, None, ff, tn), lambda j, i, te, nu: (li, te[i], 0, half_blks + j))],
        out_specs=pl.BlockSpec((tm, tn), lambda j, i, te, nu: (i, j)),
        scratch_shapes=[pltpu.VMEM((ff, tn), BF16), pltpu.VMEM((ff, tn), BF16)],
    )
    return pl.pallas_call(
        _moe_down_kernel,
        grid_spec=down_spec,
        out_shape=jax.ShapeDtypeStruct((p, d // 2), jnp.uint32),
        compiler_params=_cparams(("arbitrary", "arbitrary")),
        name="moe_down",
    )(tile_e, n_used, g, w2, w2)


def _combine_kernel(pos_ref, y_hbm, rt_ref, o_ref, g0_s, g1_s, sem):
    i = pl.program_id(0)
    tc = g0_s.shape[0]

    def issue(r, carry):
        t = i * tc + r
        pltpu.make_async_copy(y_hbm.at[pl.ds(pos_ref[2 * t], 1), :], g0_s.at[pl.ds(r, 1), :], sem.at[0]).start()
        pltpu.make_async_copy(y_hbm.at[pl.ds(pos_ref[2 * t + 1], 1), :], g1_s.at[pl.ds(r, 1), :], sem.at[1]).start()
        return carry

    lax.fori_loop(0, tc, issue, 0, unroll=8)
    pltpu.make_async_copy(y_hbm.at[pl.ds(0, tc), :], g0_s, sem.at[0]).wait()
    pltpu.make_async_copy(y_hbm.at[pl.ds(0, tc), :], g1_s, sem.at[1]).wait()
    w0 = rt_ref[:, 10:11]
    w1 = rt_ref[:, 11:12]
    lo0, hi0 = _unpack_halves(g0_s[...])
    lo1, hi1 = _unpack_halves(g1_s[...])
    half = g0_s.shape[1]
    o_ref[:, :half] = (w0 * lo0 + w1 * lo1).astype(o_ref.dtype)
    o_ref[:, half:] = (w0 * hi0 + w1 * hi1).astype(o_ref.dtype)


def _moe_combine(y, pos, route2d):
    t = route2d.shape[0]
    half = y.shape[1]
    d = 2 * half
    tc = ROW_BLK
    grid_spec = pltpu.PrefetchScalarGridSpec(
        num_scalar_prefetch=1,
        grid=(t // tc,),
        in_specs=[pl.BlockSpec(memory_space=pl.ANY),
                  pl.BlockSpec((tc, LANE), lambda i, pos: (i, 0))],
        out_specs=pl.BlockSpec((tc, d), lambda i, pos: (i, 0)),
        scratch_shapes=[pltpu.VMEM((tc, half), jnp.uint32), pltpu.VMEM((tc, half), jnp.uint32),
                        pltpu.SemaphoreType.DMA((2,))],
    )
    return pl.pallas_call(
        _combine_kernel,
        grid_spec=grid_spec,
        out_shape=jax.ShapeDtypeStruct((t, d), BF16),
        compiler_params=_cparams(("arbitrary",), row_dma=True),
        name="moe_combine",
    )(pos, y, route2d)


def _moe_plan(route, tm):
    t = route.shape[0]
    e_flat = route[:, 8:10].astype(jnp.int32).reshape(-1)
    na = 2 * t
    n_tiles = -(-(na + N_EXPERTS * (tm - 1)) // tm)
    p = n_tiles * tm
    onehot = (e_flat[:, None] == jnp.arange(N_EXPERTS)[None, :]).astype(jnp.int32)
    csum = jnp.cumsum(onehot, axis=0)
    counts = csum[-1]
    rank = jnp.sum((csum - onehot) * onehot, axis=1)
    tiles_e = (counts + tm - 1) // tm
    tile_end = jnp.cumsum(tiles_e)
    tile_start = tile_end - tiles_e
    n_used = tile_end[-1]
    cstart = jnp.cumsum(counts) - counts
    pos_of = (jnp.sum(onehot * tile_start[None, :], axis=1) * tm + rank).astype(jnp.int32)
    _, tok_sorted = lax.sort((pos_of, jnp.arange(na, dtype=jnp.int32) // 2), num_keys=1)
    tid = jnp.arange(n_tiles, dtype=jnp.int32)
    e_raw = jnp.sum((tid[:, None] >= tile_end[None, :]).astype(jnp.int32), axis=1)
    tile_e = jnp.minimum(jnp.sum((jnp.minimum(tid, n_used - 1)[:, None] >= tile_end[None, :]).astype(jnp.int32),
                                 axis=1), N_EXPERTS - 1)
    oh_t = (jnp.minimum(e_raw, N_EXPERTS - 1)[:, None] == jnp.arange(N_EXPERTS)[None, :]).astype(jnp.int32)
    t_start = jnp.sum(oh_t * tile_start[None, :], axis=1)
    t_count = jnp.where(tid < n_used, jnp.sum(oh_t * counts[None, :], axis=1), 0)
    t_cstart = jnp.sum(oh_t * cstart[None, :], axis=1)
    local = (tid - t_start)[:, None] * tm + jnp.arange(tm, dtype=jnp.int32)[None, :]
    valid = (local < t_count[:, None]).reshape(p)
    src = jnp.clip(t_cstart[:, None] + local, 0, na - 1).reshape(p)
    src_tok = jnp.where(valid, tok_sorted[src], 0)
    return tile_e.astype(jnp.int32), n_used.reshape(1).astype(jnp.int32), src_tok, pos_of


def _moe(hp, route, w1, w3, w2, li):
    nb, rows, half = hp.shape
    t = nb * rows
    route2d = route.reshape(t, LANE)
    tile_e, n_used, src_tok, pos_of = _moe_plan(route2d, MOE_TM)
    xs = _moe_dispatch(hp.reshape(t, half), n_used, src_tok)
    y = _moe_ffn(xs, tile_e, n_used, w1, w3, w2, li)
    return _moe_combine(y, pos_of, route2d).reshape(nb, rows, 2 * half)


def _rope_tables(seq):
    quarter = RET_DK // 4
    inv_freq = ROPE_BASE ** (-jnp.arange(quarter, dtype=F32) / quarter)
    t = jnp.arange(seq, dtype=jnp.int32)
    ang_r = (t // GRID_W).astype(F32)[:, None] * inv_freq[None]
    ang_c = (t % GRID_W).astype(F32)[:, None] * inv_freq[None]
    ang = jnp.concatenate([ang_r, ang_r, ang_c, ang_c], axis=1)
    rot = np.zeros((RET_DK, RET_DK), np.float32)
    for j in range(RET_DK):
        if (j % (2 * quarter)) < quarter:
            rot[j + quarter, j] = -1.0
        else:
            rot[j - quarter, j] = 1.0
    return jnp.cos(ang), jnp.sin(ang), jnp.asarray(rot, BF16)


PACK_TN = 512


def _pack_kernel(a_ref, b_ref, o_ref):
    blk = pl.program_id(1)
    main_blks = W_IN_SPLIT // PACK_TN
    sh = 2 * GLA_RANK

    @pl.when(blk < main_blks)
    def _():
        o_ref[...] = a_ref[...].T.astype(BF16)

    @pl.when(blk == main_blks)
    def _():
        row = lax.broadcasted_iota(jnp.int32, a_ref.shape, 0)
        o_ref[...] = jnp.where(row < sh, a_ref[...], 0.0).T.astype(BF16)

    @pl.when(blk > main_blks)
    def _():
        o_ref[...] = jnp.concatenate([a_ref[sh:, :], b_ref[:sh, :]], axis=0).T.astype(BF16)


def _pack_w_in(w):
    depth, d, n_src = w.shape
    wt = jnp.swapaxes(w, 1, 2)
    main_blks = W_IN_SPLIT // PACK_TN
    a_idx = lambda j: jnp.where(j <= main_blks, j, j - 1)
    last_b = (n_src - 1) // LANE
    return pl.pallas_call(
        _pack_kernel,
        grid=(depth, N_PROJ // PACK_TN),
        in_specs=[pl.BlockSpec((None, PACK_TN, d), lambda l, j: (l, a_idx(j), 0)),
                  pl.BlockSpec((None, LANE, d),
                               lambda l, j: (l, jnp.minimum((a_idx(j) + 1) * (PACK_TN // LANE), last_b), 0))],
        out_specs=pl.BlockSpec((None, d, PACK_TN), lambda l, j: (l, 0, j)),
        out_shape=jax.ShapeDtypeStruct((depth, d, N_PROJ), BF16),
        compiler_params=_cparams(("arbitrary", "arbitrary")),
        name="pack_w_in",
    )(wt, wt)


def kernel(x, c, ctx, c_ctx, w_ada, b_ada, norms, w_in, ret_decay, gla_w2, gla_b2, na_rpb, w_up, w_out,
           ffn_w1, ffn_w3, ffn_w2, moe_router, moe_router_b, moe_w1, moe_w3, moe_w2):
    nb, seq, d = x.shape
    n_ctx = ctx.shape[1]
    l = n_ctx + seq
    nblk = l // ROW_BLK
    assert n_ctx == ROW_BLK and seq % ROW_BLK == 0 and seq // ROW_BLK >= NA_SLAB and d == D_MODEL and nb < 16

    cvec = jnp.zeros((16, d), F32).at[:nb].set(c).at[nb].set(c_ctx)
    mod = _adaln(cvec, w_ada, b_ada)
    cos, sin, rot = _rope_tables(seq)
    cg, sg, dft = _dft_tables(n_ctx, seq)
    log_gamma = jnp.log1p(-jnp.exp(ret_decay.astype(F32)))

    w_in_b = _pack_w_in(w_in)
    w_up_b, w_out_b = w_up.astype(BF16), w_out.astype(BF16)
    ffn_b = (ffn_w1.astype(BF16), ffn_w3.astype(BF16), ffn_w2.astype(BF16))

    xs = jnp.concatenate([ctx, x], axis=1)
    _, h, _ = _resid(xs, None, None, None, mod[0], norms[0], None, x_off=0, f_off=0, nblk=nblk,
                     ctx_first=True, ib=0, sh_k=0, sc_k=1)
    for li in range(DEPTH):
        last = li == DEPTH - 1
        is_moe = li % 2 == 1
        h2d = h.reshape(nb * l, d)
        n_mix = C_GATE * LANE
        proj = _matmul(h2d, w_in_b, li, BF16, n_mix).reshape(nb, l, n_mix)
        w2p = jnp.zeros((2, LANE, GLA_HEADS * GLA_DK), F32)
        w2p = w2p.at[0, :GLA_RANK].set(gla_w2[li, 0]).at[1, GLA_RANK:2 * GLA_RANK].set(gla_w2[li, 1]).astype(BF16)
        y_ret = _retention(proj, log_gamma[li], cos, sin, rot)
        y_na = _na(proj, _na_bias(na_rpb[li], seq))
        y_fn = _fourier(proj, cg, sg, dft)
        y_gla = _gla(proj, w2p, gla_b2[li].reshape(2, 1, -1).astype(F32))
        ys = [y.reshape(nb * l, BRANCH_W) for y in (y_ret, y_na, y_fn, y_gla)]
        merged = _merge(h2d, ys, w_up_b, w_in_b, li).reshape(nb, l, d)
        out_proj = (w_out_b, li)

        router = None
        if is_moe:
            wr = jnp.zeros((d, LANE), F32).at[:, :N_EXPERTS].set(moe_router[li // 2])
            wr_hi = wr.astype(BF16)
            wr_lo = (wr - wr_hi.astype(F32)).astype(BF16)
            br = jnp.zeros((1, LANE), F32).at[0, :N_EXPERTS].set(moe_router_b[li // 2])
            router = (jnp.stack([wr_hi, wr_lo]), br)
        if last:
            xs, h2, route = _resid(xs, merged, mod[li], norms[li], mod[li], norms[li], router, x_off=1, f_off=1,
                                   nblk=nblk - 1, ctx_first=False, ia=1, gate_k=2, ib=2, sh_k=3, sc_k=4,
                                   proj=out_proj)
        else:
            xs, h2, route = _resid(xs, merged, mod[li], norms[li], mod[li], norms[li], router, x_off=0, f_off=0,
                                   nblk=nblk, ctx_first=True, ia=1, gate_k=2, ib=2, sh_k=3, sc_k=4,
                                   proj=out_proj)
        rows = h2.shape[1]
        if is_moe:
            f = _moe(h2, route, moe_w1, moe_w3, moe_w2, li // 2)
        else:
            f = _ffn(h2.reshape(nb * rows, d), *ffn_b, li // 2).reshape(nb, rows, d)
        if last:
            xs, _, _ = _resid(xs, f, mod[li], norms[li], None, None, None, x_off=0, f_off=0, nblk=nblk - 1,
                              ctx_first=False, ia=3, gate_k=5)
        else:
            xs, h, _ = _resid(xs, f, mod[li], norms[li], mod[li + 1], norms[li + 1], None, x_off=0, f_off=0,
                              nblk=nblk, ctx_first=True, ia=3, gate_k=5, ib=0, sh_k=0, sc_k=1)
    return xs
```

```python
import functools
import math

import numpy as np
import jax
import jax.numpy as jnp
from jax import lax
from jax.experimental import pallas as pl
from jax.experimental.pallas import tpu as pltpu

F32 = jnp.float32
BF16 = jnp.bfloat16

D_MODEL = 2048
DEPTH = 4
GRID_W = 64
CTX_LEN = 256
EPS = 1e-6
ROPE_BASE = 10000.0
RET_HEADS, RET_DK = 4, 128
NA_HEADS, NA_DH = 4, 128
NA_WIN_R, NA_WIN_C = 8, 16
FNET_GROUPS, FNET_GW = 4, 128
GLA_HEADS, GLA_DK, GLA_DV, GLA_RANK, GLA_TAU = 4, 64, 128, 16, 16.0
GLA_CHUNK = 64
BRANCH_W = 512
N_EXPERTS = 8

LANE = 128
ROW_BLK = 256
VMEM_LIMIT = 56 * 1024 * 1024

C_RET_Q, C_RET_K, C_RET_V, C_RET_G = 0, 4, 8, 12
C_NA_Q, C_NA_K, C_NA_V = 16, 20, 24
C_FU = 28
C_GLA_Q, C_GLA_K, C_GLA_V, C_GLA_G, C_GLA_LR = 32, 34, 36, 40, 44
C_GATE = 48
N_PROJ = (C_GATE + 4 * D_MODEL // LANE) * LANE
W_IN_SPLIT = 5632

NEG = -1e30


def _cparams(sem, row_dma=False):
    return pltpu.CompilerParams(dimension_semantics=sem, vmem_limit_bytes=VMEM_LIMIT,
                                disable_bounds_checks=row_dma)


def _silu(x):
    return x * (1.0 / (1.0 + jnp.exp(-x)))


def _sigmoid(x):
    return 0.5 * jnp.tanh(0.5 * x) + 0.5


def _rms(x):
    return x * lax.rsqrt(jnp.mean(x * x, axis=-1, keepdims=True) + EPS)


HI_MASK = 0xFFFF0000


def _pack_halves(x):
    n = x.shape[1] // 2
    lo = pltpu.bitcast(x[:, :n].astype(BF16).astype(F32), jnp.uint32)
    hi = pltpu.bitcast(x[:, n:].astype(BF16).astype(F32), jnp.uint32)
    return (lo >> 16) | (hi & jnp.uint32(HI_MASK))


def _unpack_halves(p):
    return pltpu.bitcast(p << 16, F32), pltpu.bitcast(p & jnp.uint32(HI_MASK), F32)


def _ada_kernel(c_ref, w_ref, b_ref, o_ref):
    s = _silu(c_ref[...]).astype(BF16)
    o_ref[...] = jnp.dot(s, w_ref[...].astype(BF16), preferred_element_type=F32) + b_ref[...]


def _adaln(cvec, w_ada, b_ada):
    depth, d, n = w_ada.shape
    tn = 1024
    return pl.pallas_call(
        _ada_kernel,
        grid=(depth, n // tn),
        in_specs=[pl.BlockSpec((16, d), lambda l, j: (0, 0)),
                  pl.BlockSpec((None, d, tn), lambda l, j: (l, 0, j)),
                  pl.BlockSpec((None, 1, tn), lambda l, j: (l, 0, j))],
        out_specs=pl.BlockSpec((None, 16, tn), lambda l, j: (l, 0, j)),
        out_shape=jax.ShapeDtypeStruct((depth, 16, n), F32),
        compiler_params=_cparams(("arbitrary", "arbitrary")),
        name="adaln",
    )(cvec, w_ada, b_ada.reshape(depth, 1, n))


def _resid_kernel(*refs, nb, ctx_first, has_f, has_proj, has_next, has_router, ia, gate_k, ib, sh_k, sc_k):
    it = iter(refs)
    x_ref = next(it)
    f_ref = next(it) if has_f else None
    wf_ref = next(it) if has_proj else None
    moda_ref = next(it) if has_f else None
    na_ref = next(it) if has_f else None
    modb_ref = next(it) if has_next else None
    nbn_ref = next(it) if has_next else None
    wr_ref = next(it) if has_router else None
    br_ref = next(it) if has_router else None
    xo_ref = next(it) if has_f else None
    h_ref = next(it) if has_next else None
    rt_ref = next(it) if has_router else None

    b = pl.program_id(0)
    j = pl.program_id(1)
    row = jnp.where(j == 0, nb, b) if ctx_first else b
    d = x_ref.shape[-1]

    def modv(ref, k):
        return ref[pl.ds(row, 1), k * d:(k + 1) * d]

    x = x_ref[...]
    if has_f:
        if has_proj:
            f = jnp.dot(f_ref[...], wf_ref[...], preferred_element_type=F32)
        else:
            f = f_ref[...].astype(F32)
        x = x + modv(moda_ref, gate_k) * (_rms(f) * na_ref[ia:ia + 1, :])
        xo_ref[...] = x
    if has_next:
        h = _rms(x) * nbn_ref[ib:ib + 1, :]
        h = h * (1.0 + modv(modb_ref, sc_k)) + modv(modb_ref, sh_k)
        if not has_router:
            h_ref[...] = h.astype(h_ref.dtype)
        if has_router:
            h_ref[...] = _pack_halves(h)
            h_hi = h.astype(BF16)
            h_lo = (h - h_hi.astype(F32)).astype(BF16)
            w_hi, w_lo = wr_ref[0], wr_ref[1]
            logits = (jnp.dot(h_hi, w_hi, preferred_element_type=F32)
                      + jnp.dot(h_lo, w_hi, preferred_element_type=F32)
                      + jnp.dot(h_hi, w_lo, preferred_element_type=F32)) + br_ref[...]
            lane = lax.broadcasted_iota(jnp.int32, logits.shape, 1)
            lg = jnp.where(lane < N_EXPERTS, logits, -jnp.inf)
            v1 = jnp.max(lg, axis=-1, keepdims=True)
            i1 = jnp.min(jnp.where(lg == v1, lane, LANE), axis=-1, keepdims=True)
            lg2 = jnp.where(lane == i1, -jnp.inf, lg)
            v2 = jnp.max(lg2, axis=-1, keepdims=True)
            i2 = jnp.min(jnp.where(lg2 == v2, lane, LANE), axis=-1, keepdims=True)
            e2 = jnp.exp(v2 - v1)
            w1 = 1.0 / (1.0 + e2)
            w2 = e2 / (1.0 + e2)
            zero = jnp.zeros_like(logits)
            rt = (jnp.where(lane == 8, i1.astype(F32), zero) + jnp.where(lane == 9, i2.astype(F32), zero)
                  + jnp.where(lane == 10, w1, zero) + jnp.where(lane == 11, w2, zero))
            rt_ref[...] = rt


def _resid(x, f, mod_a, norms_a, mod_b, norms_b, router, *, x_off, f_off, nblk, ctx_first,
           ia=0, gate_k=0, ib=0, sh_k=0, sc_k=0, proj=None):
    nb, _, d = x.shape
    has_f = f is not None
    has_proj = proj is not None
    has_next = mod_b is not None
    has_router = router is not None
    rows = nblk * ROW_BLK
    blk = lambda off: pl.BlockSpec((None, ROW_BLK, d), lambda b, j: (b, j + off, 0))
    full = lambda a: pl.BlockSpec(a.shape, lambda b, j: (0,) * a.ndim)
    ins, specs = [x], [blk(x_off)]
    if has_f:
        ins += [f]
        specs += [blk(f_off)]
        if has_proj:
            w, li = proj
            ins += [w]
            specs += [pl.BlockSpec((None, d, d), lambda b, j: (li, 0, 0))]
        ins += [mod_a, norms_a]
        specs += [full(mod_a), full(norms_a)]
    if has_next:
        ins += [mod_b, norms_b]
        specs += [full(mod_b), full(norms_b)]
    if has_router:
        ins += list(router)
        specs += [full(router[0]), full(router[1])]
    outs, ospecs = [], []
    if has_f:
        outs.append(jax.ShapeDtypeStruct((nb, rows, d), F32))
        ospecs.append(blk(0))
    if has_next and not has_router:
        outs.append(jax.ShapeDtypeStruct((nb, rows, d), BF16))
        ospecs.append(blk(0))
    if has_next and has_router:
        outs.append(jax.ShapeDtypeStruct((nb, rows, d // 2), jnp.uint32))
        ospecs.append(pl.BlockSpec((None, ROW_BLK, d // 2), lambda b, j: (b, j, 0)))
    if has_router:
        outs.append(jax.ShapeDtypeStruct((nb, rows, LANE), F32))
        ospecs.append(pl.BlockSpec((None, ROW_BLK, LANE), lambda b, j: (b, j, 0)))
    res = pl.pallas_call(
        functools.partial(_resid_kernel, nb=nb, ctx_first=ctx_first, has_f=has_f, has_proj=has_proj, has_next=has_next,
                          has_router=has_router, ia=ia, gate_k=gate_k, ib=ib, sh_k=sh_k, sc_k=sc_k),
        grid=(nb, nblk),
        in_specs=specs, out_specs=ospecs, out_shape=outs,
        compiler_params=_cparams(("arbitrary", "arbitrary")),
        name="resid_norm",
    )(*ins)
    res = list(res)
    x_new = res.pop(0) if has_f else None
    h = res.pop(0) if has_next else None
    rt = res.pop(0) if has_router else None
    return x_new, h, rt


def _mm_kernel(x_ref, w_ref, o_ref):
    o_ref[...] = jnp.dot(x_ref[...], w_ref[...], preferred_element_type=F32).astype(o_ref.dtype)


def _matmul(x, w, li, out_dtype, n, tm=1024, tn=1024):
    m, k = x.shape
    return pl.pallas_call(
        _mm_kernel,
        grid=(m // tm, n // tn),
        in_specs=[pl.BlockSpec((tm, k), lambda i, j: (i, 0)),
                  pl.BlockSpec((None, k, tn), lambda i, j: (li, 0, j))],
        out_specs=pl.BlockSpec((tm, tn), lambda i, j: (i, j)),
        out_shape=jax.ShapeDtypeStruct((m, n), out_dtype),
        compiler_params=_cparams(("arbitrary", "arbitrary")),
        name="matmul",
    )(x, w)


def _ret_kernel(lg_ref, q_ref, k_ref, v_ref, g_ref, cos_ref, sin_ref, rot_ref, o_ref, q_s, k_s, acc_s):
    h = pl.program_id(1)
    lgf = lg_ref[0, h]
    lgb = lg_ref[1, h]
    c = ROW_BLK
    n = q_ref.shape[0] // c
    scale = RET_DK ** -0.5

    q_s[0:c, :] = (q_ref[0:c, :].astype(F32) * scale).astype(BF16)
    k_s[0:c, :] = k_ref[0:c, :]
    rot = rot_ref[...]
    for i in range(1, n):
        rows = slice(i * c, (i + 1) * c)
        trow = slice((i - 1) * c, i * c)
        cs, sn = cos_ref[trow, :], sin_ref[trow, :]
        qa, ka = q_ref[rows, :], k_ref[rows, :]
        qr = jnp.dot(qa, rot, preferred_element_type=F32)
        kr = jnp.dot(ka, rot, preferred_element_type=F32)
        q_s[rows, :] = ((qa.astype(F32) * cs + qr * sn) * scale).astype(BF16)
        k_s[rows, :] = (ka.astype(F32) * cs + kr * sn).astype(BF16)

    ti = lax.broadcasted_iota(jnp.int32, (c, c), 0)
    si = lax.broadcasted_iota(jnp.int32, (c, c), 1)
    dd = (ti - si).astype(F32)
    dmat = jnp.where(dd > 0, jnp.exp(lgf * dd), jnp.where(dd < 0, jnp.exp(-lgb * dd), 2.0))
    tcol = lax.broadcasted_iota(jnp.int32, (c, 1), 0).astype(F32)
    q_f = jnp.exp(lgf * (tcol + 1.0))
    q_b = jnp.exp(lgb * (c - tcol))
    k_f = jnp.exp(lgf * (c - 1.0 - tcol))
    k_b = jnp.exp(lgb * tcol)
    g_f = jnp.exp(lgf * c)
    g_b = jnp.exp(lgb * c)

    def kv_state(kc, vc, kdec):
        kd = (kc.astype(F32) * kdec).astype(BF16)
        return lax.dot_general(kd, vc, (((0,), (0,)), ((), ())), preferred_element_type=F32)

    s_f = jnp.zeros((RET_DK, q_ref.shape[1]), F32)
    for i in range(n):
        rows = slice(i * c, (i + 1) * c)
        qc, kc, vc = q_s[rows, :], k_s[rows, :], v_ref[rows, :]
        sc = lax.dot_general(qc, kc, (((1,), (1,)), ((), ())), preferred_element_type=F32)
        o = jnp.dot((sc * dmat).astype(BF16), vc, preferred_element_type=F32)
        if i > 0:
            qd = (qc.astype(F32) * q_f).astype(BF16)
            o = o + jnp.dot(qd, s_f.astype(BF16), preferred_element_type=F32)
        acc_s[rows, :] = o
        if i < n - 1:
            s_f = g_f * s_f + kv_state(kc, vc, k_f)

    s_b = kv_state(k_s[0:c, :], v_ref[0:c, :], k_b)
    for i in range(n - 1, 0, -1):
        rows = slice(i * c, (i + 1) * c)
        qc, kc, vc = q_s[rows, :], k_s[rows, :], v_ref[rows, :]
        qd = (qc.astype(F32) * q_b).astype(BF16)
        acc_s[rows, :] = acc_s[rows, :] + jnp.dot(qd, s_b.astype(BF16), preferred_element_type=F32)
        if i > 1:
            s_b = g_b * s_b + kv_state(kc, vc, k_b)

    for i in range(n):
        rows = slice(i * c, (i + 1) * c)
        o_ref[rows, :] = (_rms(acc_s[rows, :]) * _silu(g_ref[rows, :].astype(F32))).astype(o_ref.dtype)


def _retention(proj, lg, cos, sin, rot):
    nb, l, _ = proj.shape
    col = lambda c0: pl.BlockSpec((None, l, LANE), lambda b, h: (b, 0, c0 + h))
    const = lambda a: pl.BlockSpec(a.shape, lambda b, h: (0,) * a.ndim)
    return pl.pallas_call(
        _ret_kernel,
        grid=(nb, RET_HEADS),
        in_specs=[pl.BlockSpec(memory_space=pltpu.SMEM),
                  col(C_RET_Q), col(C_RET_K), col(C_RET_V), col(C_RET_G),
                  const(cos), const(sin), const(rot)],
        out_specs=pl.BlockSpec((None, l, LANE), lambda b, h: (b, 0, h)),
        out_shape=jax.ShapeDtypeStruct((nb, l, BRANCH_W), BF16),
        scratch_shapes=[pltpu.VMEM((l, LANE), BF16), pltpu.VMEM((l, LANE), BF16), pltpu.VMEM((l, LANE), F32)],
        compiler_params=_cparams(("arbitrary", "arbitrary")),
        name="retention",
    )(lg, proj, proj, proj, proj, cos, sin, rot)


def _gla_kernel(q_ref, k_ref, v_ref, g_ref, lr_ref, w2_ref, b2_ref, o_ref,
                accf_s, accb_s, st_s, qd_s, ki_s, ke_s, dl_s, *, n_ctx_chunks):
    c = GLA_CHUNK
    l = q_ref.shape[0]
    n = l // c
    cpb = ROW_BLK // c
    kw = GLA_HEADS * GLA_DK
    scale = GLA_DK ** -0.5

    ti = lax.broadcasted_iota(jnp.int32, (ROW_BLK, ROW_BLK), 0)
    si = lax.broadcasted_iota(jnp.int32, (ROW_BLK, ROW_BLK), 1)
    same = ((ti // c) == (si // c)).astype(F32)
    cum_lo = (same * (ti >= si).astype(F32)).astype(BF16)
    cum_up = (same * (ti <= si).astype(F32)).astype(BF16)

    def prep(bi, carry):
        rows = pl.ds(pl.multiple_of(bi * ROW_BLK, ROW_BLK), ROW_BLK)
        lr = lr_ref[rows, :]
        qf = q_ref[rows, :].astype(F32) * scale
        kf = k_ref[rows, :].astype(F32)
        for d, cum, last in ((0, cum_lo, c - 1), (1, cum_up, 0)):
            z = jnp.dot(lr, w2_ref[d], preferred_element_type=F32) + b2_ref[d]
            la = (jnp.minimum(z, 0.0) - jnp.log(1.0 + jnp.exp(-jnp.abs(z)))) * (1.0 / GLA_TAU)
            la_hi = la.astype(BF16)
            la_lo = (la - la_hi.astype(F32)).astype(BF16)
            bcum = (jnp.dot(cum, la_hi, preferred_element_type=F32)
                    + jnp.dot(cum, la_lo, preferred_element_type=F32))
            bl = jnp.concatenate([jnp.broadcast_to(bcum[cc * c + last:cc * c + last + 1, :], (c, kw))
                                  for cc in range(cpb)], axis=0)
            qd_s[d, rows, :] = (qf * jnp.exp(bcum)).astype(BF16)
            ki_s[d, rows, :] = (kf * jnp.exp(-bcum)).astype(BF16)
            ke_s[d, rows, :] = (kf * jnp.exp(bl - bcum)).astype(BF16)
            dec = jnp.exp(bl)
            for cc in range(cpb):
                dl_s[d, pl.ds(pl.multiple_of((bi * cpb + cc) * 8, 8), 8), :] = dec[cc * c:cc * c + 8, :]
        return carry

    lax.fori_loop(0, l // ROW_BLK, prep, 0)

    t4 = lax.broadcasted_iota(jnp.int32, (GLA_HEADS * c, c), 0) % c
    s4 = lax.broadcasted_iota(jnp.int32, (GLA_HEADS * c, c), 1)
    mask_lo = (t4 >= s4).astype(F32)
    mask_up = (t4 <= s4).astype(F32)
    srow = lax.broadcasted_iota(jnp.int32, (GLA_HEADS * c, kw), 0) // c
    slane = lax.broadcasted_iota(jnp.int32, (GLA_HEADS * c, kw), 1) // GLA_DK
    stack_mask = srow == slane
    lane_head = lax.broadcasted_iota(jnp.int32, (GLA_DV, kw), 1) // GLA_DK
    lanes = (((1,), (1,)), ((), ()))
    st_s[...] = jnp.zeros_like(st_s)

    def chunk(ci, d, mask, acc_ref):
        rows = pl.ds(pl.multiple_of(ci * c, c), c)
        q_dec, k_inv, k_end = qd_s[d, rows, :], ki_s[d, rows, :], ke_s[d, rows, :]
        vc = v_ref[rows, :]
        dec = dl_s[d, pl.ds(pl.multiple_of(ci * 8, 8), 1), :]
        q_stack = jnp.where(stack_mask, jnp.concatenate([q_dec] * GLA_HEADS, axis=0), jnp.zeros((), BF16))
        sc = (lax.dot_general(q_stack, k_inv, lanes, preferred_element_type=F32) * mask).astype(BF16)
        st = st_s[d]
        inter = lax.dot_general(q_stack, st.astype(BF16), lanes, preferred_element_type=F32)
        acc_ref[rows, :] = jnp.concatenate(
            [jnp.dot(sc[h * c:(h + 1) * c, :], vc[:, h * GLA_DV:(h + 1) * GLA_DV], preferred_element_type=F32)
             + inter[h * c:(h + 1) * c, :] for h in range(GLA_HEADS)], axis=1)
        full = lax.dot_general(vc, k_end, (((0,), (0,)), ((), ())), preferred_element_type=F32)
        comp = full[0:GLA_DV, :]
        for h in range(1, GLA_HEADS):
            comp = jnp.where(lane_head == h, full[h * GLA_DV:(h + 1) * GLA_DV, :], comp)
        st_s[d] = st * dec + comp

    def step(i, carry):
        chunk(i, 0, mask_lo, accf_s)
        chunk(jnp.where(i < n_ctx_chunks, n_ctx_chunks - 1 - i, n + n_ctx_chunks - 1 - i), 1, mask_up, accb_s)
        return carry

    lax.fori_loop(0, n, step, 0, unroll=4)

    for i in range(l // ROW_BLK):
        rows = slice(i * ROW_BLK, (i + 1) * ROW_BLK)
        o = accf_s[rows, :] + accb_s[rows, :]
        o = jnp.concatenate([_rms(o[:, h * GLA_DV:(h + 1) * GLA_DV]) for h in range(GLA_HEADS)], axis=1)
        o_ref[rows, :] = (o * _silu(g_ref[rows, :].astype(F32))).astype(o_ref.dtype)


def _gla(proj, w2p, b2):
    nb, l, _ = proj.shape
    kw, vw = GLA_HEADS * GLA_DK, GLA_HEADS * GLA_DV
    col = lambda c0, w: pl.BlockSpec((None, l, w), lambda b: (b, 0, c0 * LANE // w))
    const = lambda a: pl.BlockSpec(a.shape, lambda b: (0,) * a.ndim)
    return pl.pallas_call(
        functools.partial(_gla_kernel, n_ctx_chunks=CTX_LEN // GLA_CHUNK),
        grid=(nb,),
        in_specs=[col(C_GLA_Q, kw), col(C_GLA_K, kw), col(C_GLA_V, vw), col(C_GLA_G, vw), col(C_GLA_LR, LANE),
                  const(w2p), const(b2)],
        out_specs=pl.BlockSpec((None, l, vw), lambda b: (b, 0, 0)),
        out_shape=jax.ShapeDtypeStruct((nb, l, BRANCH_W), BF16),
        scratch_shapes=[pltpu.VMEM((l, vw), F32), pltpu.VMEM((l, vw), F32),
                        pltpu.VMEM((2, GLA_DV, kw), F32),
                        pltpu.VMEM((2, l, kw), BF16), pltpu.VMEM((2, l, kw), BF16), pltpu.VMEM((2, l, kw), BF16),
                        pltpu.VMEM((2, l // GLA_CHUNK * 8, kw), F32)],
        compiler_params=_cparams(("arbitrary",)),
        name="gla",
    )(proj, proj, proj, proj, proj, w2p, b2)


NA_QROWS = ROW_BLK // GRID_W
NA_SLAB = 3


def _na_kernel(q_ref, kc_ref, k0_ref, k1_ref, k2_ref, vc_ref, v0_ref, v1_ref, v2_ref, bias_ref, o_ref):
    scale = NA_DH ** -0.5
    dims = (((1,), (1,)), ((), ()))
    for h in range(NA_HEADS):
        cols = slice(h * NA_DH, (h + 1) * NA_DH)
        q = q_ref[:, cols]
        s = [lax.dot_general(q, kc_ref[:, cols], dims, preferred_element_type=F32) * scale]
        for j, kr in enumerate((k0_ref, k1_ref, k2_ref)):
            sj = lax.dot_general(q, kr[:, cols], dims, preferred_element_type=F32) * scale
            s.append(sj + bias_ref[h, :, j * ROW_BLK:(j + 1) * ROW_BLK])
        m = functools.reduce(jnp.maximum, [jnp.max(x, axis=-1, keepdims=True) for x in s])
        p = [jnp.exp(x - m) for x in s]
        den = functools.reduce(lambda a, b: a + b, [jnp.sum(x, axis=-1, keepdims=True) for x in p])
        o = jnp.zeros((q.shape[0], NA_DH), F32)
        for pj, vr in zip(p, (vc_ref, v0_ref, v1_ref, v2_ref)):
            o = o + jnp.dot(pj.astype(BF16), vr[:, cols], preferred_element_type=F32)
        o_ref[:, cols] = (o / den).astype(o_ref.dtype)


def _na_slab_start(qb, n_lat_blk):
    return jnp.clip(qb - 2, 0, n_lat_blk - NA_SLAB)


def _na(proj, bias):
    nb, l, _ = proj.shape
    nblk = l // ROW_BLK
    nlat = nblk - 1
    w = NA_HEADS * NA_DH
    blk = lambda c0, rowfn: pl.BlockSpec((None, ROW_BLK, w), lambda qb, b: (b, rowfn(qb), c0 * LANE // w))
    slab = lambda j: (lambda qb: 1 + _na_slab_start(qb, nlat) + j)
    return pl.pallas_call(
        _na_kernel,
        grid=(nblk, nb),
        in_specs=[blk(C_NA_Q, lambda qb: qb),
                  blk(C_NA_K, lambda qb: 0), blk(C_NA_K, slab(0)), blk(C_NA_K, slab(1)), blk(C_NA_K, slab(2)),
                  blk(C_NA_V, lambda qb: 0), blk(C_NA_V, slab(0)), blk(C_NA_V, slab(1)), blk(C_NA_V, slab(2)),
                  pl.BlockSpec((NA_HEADS, None, ROW_BLK, NA_SLAB * ROW_BLK), lambda qb, b: (0, qb, 0, 0))],
        out_specs=pl.BlockSpec((None, ROW_BLK, w), lambda qb, b: (b, qb, 0)),
        out_shape=jax.ShapeDtypeStruct((nb, l, BRANCH_W), BF16),
        compiler_params=_cparams(("arbitrary", "arbitrary")),
        name="neighbourhood_attention",
    )(*([proj] * 9), bias)


def _na_bias(rpb, seq):
    rows = seq // GRID_W
    kr = min(NA_WIN_R, rows)
    nlat = seq // ROW_BLK
    cq = np.arange(GRID_W)[:, None]
    ck = np.arange(GRID_W)[None, :]
    win_start = np.clip(cq - NA_WIN_C // 2, 0, GRID_W - NA_WIN_C)
    in_win = (ck >= win_start) & (ck < win_start + NA_WIN_C)
    rel_c = np.clip(ck - cq, 1 - NA_WIN_C, NA_WIN_C - 1) + NA_WIN_C - 1
    tiles = jnp.where(in_win[None, None], rpb[:, :, rel_c].astype(F32), NEG)
    masked = jnp.full((rpb.shape[0], GRID_W, GRID_W), NEG, F32)
    blocks = [jnp.full((rpb.shape[0], ROW_BLK, NA_SLAB * ROW_BLK), NEG, F32)]
    for qb in range(nlat):
        s0 = int(np.clip(qb - 1, 0, nlat - NA_SLAB)) * NA_QROWS
        qrows = []
        for qr in range(NA_QROWS):
            r = qb * NA_QROWS + qr
            k0 = int(np.clip(r - kr // 2, 0, rows - kr))
            assert s0 <= k0 and k0 + kr <= s0 + NA_SLAB * NA_QROWS
            krows = []
            for kk in range(NA_SLAB * NA_QROWS):
                krow = s0 + kk
                if k0 <= krow < k0 + kr:
                    krows.append(tiles[:, krow - r + NA_WIN_R - 1])
                else:
                    krows.append(masked)
            qrows.append(jnp.concatenate(krows, axis=2))
        blocks.append(jnp.concatenate(qrows, axis=1))
    return jnp.stack(blocks, axis=1)


def _fourier_kernel(u_ref, cg_ref, sg_ref, dft_ref, o_ref, ab_s):
    l = u_ref.shape[0]

    @pl.when(pl.program_id(1) == 0)
    def _():
        for g in range(FNET_GROUPS):
            cols = slice(g * FNET_GW, (g + 1) * FNET_GW)
            u = u_ref[:, cols]
            ab_s[0:l, cols] = jnp.dot(u, cg_ref[...], preferred_element_type=F32).astype(BF16)
            ab_s[l:2 * l, cols] = jnp.dot(u, sg_ref[...], preferred_element_type=F32).astype(BF16)

    o_ref[...] = jnp.dot(dft_ref[...], ab_s[...], preferred_element_type=F32).astype(o_ref.dtype)


def _fourier(proj, cg, sg, dft):
    nb, l, _ = proj.shape
    tr = 3 * ROW_BLK
    w = FNET_GROUPS * FNET_GW
    return pl.pallas_call(
        _fourier_kernel,
        grid=(nb, l // tr),
        in_specs=[pl.BlockSpec((None, l, w), lambda b, i: (b, 0, C_FU * LANE // w)),
                  pl.BlockSpec(cg.shape, lambda b, i: (0, 0)),
                  pl.BlockSpec(sg.shape, lambda b, i: (0, 0)),
                  pl.BlockSpec((tr, 2 * l), lambda b, i: (i, 0))],
        out_specs=pl.BlockSpec((None, tr, w), lambda b, i: (b, i, 0)),
        out_shape=jax.ShapeDtypeStruct((nb, l, BRANCH_W), BF16),
        scratch_shapes=[pltpu.VMEM((2 * l, w), BF16)],
        compiler_params=_cparams(("arbitrary", "arbitrary")),
        name="fourier_mix",
    )(proj, cg, sg, dft)


def _dft_tables(n_ctx, seq):
    def cs(n):
        jk = (np.arange(n)[:, None] * np.arange(n)[None, :]) % n
        ang = 2.0 * np.pi * jk / n
        return np.cos(ang) / np.sqrt(n), np.sin(ang) / np.sqrt(n)

    cg, sg = cs(FNET_GW)
    l = n_ctx + seq
    cl = np.zeros((l, l))
    sl = np.zeros((l, l))
    cc, sc = cs(n_ctx)
    cs_, ss_ = cs(seq)
    cl[:n_ctx, :n_ctx], sl[:n_ctx, :n_ctx] = cc, sc
    cl[n_ctx:, n_ctx:], sl[n_ctx:, n_ctx:] = cs_, ss_
    dft = np.concatenate([cl, -sl], axis=1)
    return (jnp.asarray(cg, BF16), jnp.asarray(sg, BF16), jnp.asarray(dft, BF16))


def _merge_kernel(h_ref, y0_ref, y1_ref, y2_ref, y3_ref, w_ref, g0_ref, g1_ref, g2_ref, g3_ref, o_ref):
    h = h_ref[...]
    acc = None
    for i, (y_ref, g_ref) in enumerate(zip((y0_ref, y1_ref, y2_ref, y3_ref), (g0_ref, g1_ref, g2_ref, g3_ref))):
        gate = _sigmoid(jnp.dot(h, g_ref[...], preferred_element_type=F32))
        t = gate * jnp.dot(y_ref[...], w_ref[i], preferred_element_type=F32)
        acc = t if acc is None else acc + t
    o_ref[...] = acc.astype(o_ref.dtype)


def _merge(h2d, ys, w_up, w_in_b, li, tm=1024, tn=512):
    t, k = h2d.shape
    d = w_up.shape[3]
    gate0 = C_GATE * LANE // tn
    yspec = pl.BlockSpec((tm, BRANCH_W), lambda i, j: (i, 0))
    gspec = lambda br: pl.BlockSpec((None, k, tn), lambda i, j: (li, 0, gate0 + br * (d // tn) + j))
    return pl.pallas_call(
        _merge_kernel,
        grid=(t // tm, d // tn),
        in_specs=[pl.BlockSpec((tm, k), lambda i, j: (i, 0))] + [yspec] * 4
        + [pl.BlockSpec((None, 4, BRANCH_W, tn), lambda i, j: (li, 0, 0, j))] + [gspec(br) for br in range(4)],
        out_specs=pl.BlockSpec((tm, tn), lambda i, j: (i, j)),
        out_shape=jax.ShapeDtypeStruct((t, d), BF16),
        compiler_params=_cparams(("arbitrary", "arbitrary")),
        name="gated_merge",
    )(h2d, *ys, w_up, w_in_b, w_in_b, w_in_b, w_in_b)


def _ffn_up_kernel(x_ref, w1_ref, w3_ref, o_ref, w1b_s, w3b_s):
    @pl.when(pl.program_id(1) == 0)
    def _():
        w1b_s[...] = w1_ref[...].astype(BF16)
        w3b_s[...] = w3_ref[...].astype(BF16)

    x = x_ref[...]
    a = jnp.dot(x, w1b_s[...], preferred_element_type=F32)
    b = jnp.dot(x, w3b_s[...], preferred_element_type=F32)
    o_ref[...] = (_silu(a) * b).astype(o_ref.dtype)


def _ffn_down_kernel(g_ref, w2_ref, o_ref, w2b_s):
    @pl.when(pl.program_id(1) == 0)
    def _():
        w2b_s[...] = w2_ref[...].astype(BF16)

    o_ref[...] = jnp.dot(g_ref[...], w2b_s[...], preferred_element_type=F32).astype(o_ref.dtype)


def _ffn(x, w1, w3, w2, li, tm=1024, tf=512, tm2=512, tn=512):
    t, d = x.shape
    ff = w1.shape[2]
    g = pl.pallas_call(
        _ffn_up_kernel,
        grid=(ff // tf, t // tm),
        in_specs=[pl.BlockSpec((tm, d), lambda j, i: (i, 0)),
                  pl.BlockSpec((None, d, tf), lambda j, i: (li, 0, j)),
                  pl.BlockSpec((None, d, tf), lambda j, i: (li, 0, j))],
        out_specs=pl.BlockSpec((tm, tf), lambda j, i: (i, j)),
        out_shape=jax.ShapeDtypeStruct((t, ff), BF16),
        scratch_shapes=[pltpu.VMEM((d, tf), BF16), pltpu.VMEM((d, tf), BF16)],
        compiler_params=_cparams(("arbitrary", "arbitrary")),
        name="swiglu_up",
    )(x, w1, w3)
    return pl.pallas_call(
        _ffn_down_kernel,
        grid=(d // tn, t // tm2),
        in_specs=[pl.BlockSpec((tm2, ff), lambda j, i: (i, 0)),
                  pl.BlockSpec((None, ff, tn), lambda j, i: (li, 0, j))],
        out_specs=pl.BlockSpec((tm2, tn), lambda j, i: (i, j)),
        out_shape=jax.ShapeDtypeStruct((t, d), BF16),
        scratch_shapes=[pltpu.VMEM((ff, tn), BF16)],
        compiler_params=_cparams(("arbitrary", "arbitrary")),
        name="swiglu_down",
    )(g, w2)


MOE_TM = 512
MOE_TF = 1024
MOE_TN = 512


def _dispatch_kernel(nu_ref, tok_ref, h_hbm, o_ref, xg_s, sem):
    i = pl.program_id(0)
    tm = xg_s.shape[0]
    used = i < nu_ref[0]

    @pl.when(used)
    def _():
        def issue(r, carry):
            tok = tok_ref[i * tm + r]
            pltpu.make_async_copy(h_hbm.at[pl.ds(tok, 1), :], xg_s.at[pl.ds(r, 1), :], sem).start()
            return carry

        lax.fori_loop(0, tm, issue, 0, unroll=8)
        pltpu.make_async_copy(h_hbm.at[pl.ds(0, tm), :], xg_s, sem).wait()
        lo, hi = _unpack_halves(xg_s[...])
        half = xg_s.shape[1]
        o_ref[:, :half] = lo.astype(o_ref.dtype)
        o_ref[:, half:] = hi.astype(o_ref.dtype)

    @pl.when(jnp.logical_not(used))
    def _():
        o_ref[...] = jnp.zeros_like(o_ref)


def _moe_dispatch(hp, n_used, src_tok):
    p = src_tok.shape[0]
    half = hp.shape[1]
    d = 2 * half
    tm = MOE_TM
    grid_spec = pltpu.PrefetchScalarGridSpec(
        num_scalar_prefetch=2,
        grid=(p // tm,),
        in_specs=[pl.BlockSpec(memory_space=pl.ANY)],
        out_specs=pl.BlockSpec((tm, d), lambda i, nu, tok: (i, 0)),
        scratch_shapes=[pltpu.VMEM((tm, half), jnp.uint32), pltpu.SemaphoreType.DMA(())],
    )
    return pl.pallas_call(
        _dispatch_kernel,
        grid_spec=grid_spec,
        out_shape=jax.ShapeDtypeStruct((p, d), BF16),
        compiler_params=_cparams(("arbitrary",), row_dma=True),
        name="moe_dispatch",
    )(n_used, src_tok, hp)


def _expert_changed(te_ref, i):
    return jnp.logical_or(i == 0, te_ref[i] != te_ref[jnp.maximum(i - 1, 0)])


def _moe_up_kernel(te_ref, nu_ref, x_ref, w1_ref, w3_ref, o_ref, w1b_s, w3b_s):
    i = pl.program_id(1)

    @pl.when(_expert_changed(te_ref, i))
    def _():
        w1b_s[...] = w1_ref[...].astype(BF16)
        w3b_s[...] = w3_ref[...].astype(BF16)

    @pl.when(i < nu_ref[0])
    def _():
        x = x_ref[...]
        a = jnp.dot(x, w1b_s[...], preferred_element_type=F32)
        b = jnp.dot(x, w3b_s[...], preferred_element_type=F32)
        o_ref[...] = (_silu(a) * b).astype(o_ref.dtype)

    @pl.when(i >= nu_ref[0])
    def _():
        o_ref[...] = jnp.zeros_like(o_ref)


def _moe_down_kernel(te_ref, nu_ref, g_ref, w2a_ref, w2b_ref, o_ref, w2a_s, w2b_s):
    i = pl.program_id(1)

    @pl.when(_expert_changed(te_ref, i))
    def _():
        w2a_s[...] = w2a_ref[...].astype(BF16)
        w2b_s[...] = w2b_ref[...].astype(BF16)

    @pl.when(i < nu_ref[0])
    def _():
        g = g_ref[...]
        ya = jnp.dot(g, w2a_s[...], preferred_element_type=F32)
        yb = jnp.dot(g, w2b_s[...], preferred_element_type=F32)
        o_ref[...] = _pack_halves(jnp.concatenate([ya, yb], axis=1))

    @pl.when(i >= nu_ref[0])
    def _():
        o_ref[...] = jnp.zeros_like(o_ref)


def _moe_ffn(xs, tile_e, n_used, w1, w3, w2, li):
    p, d = xs.shape
    ff = w1.shape[3]
    tm, tf, tn = MOE_TM, MOE_TF, MOE_TN
    up_spec = pltpu.PrefetchScalarGridSpec(
        num_scalar_prefetch=2,
        grid=(ff // tf, p // tm),
        in_specs=[pl.BlockSpec((tm, d), lambda j, i, te, nu: (i, 0)),
                  pl.BlockSpec((None, None, d, tf), lambda j, i, te, nu: (li, te[i], 0, j)),
                  pl.BlockSpec((None, None, d, tf), lambda j, i, te, nu: (li, te[i], 0, j))],
        out_specs=pl.BlockSpec((tm, tf), lambda j, i, te, nu: (i, j)),
        scratch_shapes=[pltpu.VMEM((d, tf), BF16), pltpu.VMEM((d, tf), BF16)],
    )
    g = pl.pallas_call(
        _moe_up_kernel,
        grid_spec=up_spec,
        out_shape=jax.ShapeDtypeStruct((p, ff), BF16),
        compiler_params=_cparams(("arbitrary", "arbitrary")),
        name="moe_up",
    )(tile_e, n_used, xs, w1, w3)
    half_blks = d // 2 // tn
    down_spec = pltpu.PrefetchScalarGridSpec(
        num_scalar_prefetch=2,
        grid=(half_blks, p // tm),
        in_specs=[pl.BlockSpec((tm, ff), lambda j, i, te, nu: (i, 0)),
                  pl.BlockSpec((None, None, ff, tn), lambda j, i, te, nu: (li, te[i], 0, j)),
                  pl.BlockSpec((None, None, ff, tn), lambda j, i, te, nu: (li, te[i], 0, half_blks + j))],
        out_specs=pl.BlockSpec((tm, tn), lambda j, i, te, nu: (i, j)),
        scratch_shapes=[pltpu.VMEM((ff, tn), BF16), pltpu.VMEM((ff, tn), BF16)],
    )
    return pl.pallas_call(
        _moe_down_kernel,
        grid_spec=down_spec,
        out_shape=jax.ShapeDtypeStruct((p, d // 2), jnp.uint32),
        compiler_params=_cparams(("arbitrary", "arbitrary")),
        name="moe_down",
    )(tile_e, n_used, g, w2, w2)


def _combine_kernel(pos_ref, y_hbm, rt_ref, o_ref, g0_s, g1_s, sem):
    i = pl.program_id(0)
    tc = g0_s.shape[0]

    def issue(r, carry):
        t = i * tc + r
        pltpu.make_async_copy(y_hbm.at[pl.ds(pos_ref[2 * t], 1), :], g0_s.at[pl.ds(r, 1), :], sem.at[0]).start()
        pltpu.make_async_copy(y_hbm.at[pl.ds(pos_ref[2 * t + 1], 1), :], g1_s.at[pl.ds(r, 1), :], sem.at[1]).start()
        return carry

    lax.fori_loop(0, tc, issue, 0, unroll=8)
    pltpu.make_async_copy(y_hbm.at[pl.ds(0, tc), :], g0_s, sem.at[0]).wait()
    pltpu.make_async_copy(y_hbm.at[pl.ds(0, tc), :], g1_s, sem.at[1]).wait()
    w0 = rt_ref[:, 10:11]
    w1 = rt_ref[:, 11:12]
    lo0, hi0 = _unpack_halves(g0_s[...])
    lo1, hi1 = _unpack_halves(g1_s[...])
    half = g0_s.shape[1]
    o_ref[:, :half] = (w0 * lo0 + w1 * lo1).astype(o_ref.dtype)
    o_ref[:, half:] = (w0 * hi0 + w1 * hi1).astype(o_ref.dtype)


def _moe_combine(y, pos, route2d):
    t = route2d.shape[0]
    half = y.shape[1]
    d = 2 * half
    tc = ROW_BLK
    grid_spec = pltpu.PrefetchScalarGridSpec(
        num_scalar_prefetch=1,
        grid=(t // tc,),
        in_specs=[pl.BlockSpec(memory_space=pl.ANY),
                  pl.BlockSpec((tc, LANE), lambda i, pos: (i, 0))],
        out_specs=pl.BlockSpec((tc, d), lambda i, pos: (i, 0)),
        scratch_shapes=[pltpu.VMEM((tc, half), jnp.uint32), pltpu.VMEM((tc, half), jnp.uint32),
                        pltpu.SemaphoreType.DMA((2,))],
    )
    return pl.pallas_call(
        _combine_kernel,
        grid_spec=grid_spec,
        out_shape=jax.ShapeDtypeStruct((t, d), BF16),
        compiler_params=_cparams(("arbitrary",), row_dma=True),
        name="moe_combine",
    )(pos, y, route2d)


def _moe_plan(route, tm):
    t = route.shape[0]
    e_flat = route[:, 8:10].astype(jnp.int32).reshape(-1)
    na = 2 * t
    n_tiles = -(-(na + N_EXPERTS * (tm - 1)) // tm)
    p = n_tiles * tm
    onehot = (e_flat[:, None] == jnp.arange(N_EXPERTS)[None, :]).astype(jnp.int32)
    csum = jnp.cumsum(onehot, axis=0)
    counts = csum[-1]
    rank = jnp.sum((csum - onehot) * onehot, axis=1)
    tiles_e = (counts + tm - 1) // tm
    tile_end = jnp.cumsum(tiles_e)
    tile_start = tile_end - tiles_e
    n_used = tile_end[-1]
    cstart = jnp.cumsum(counts) - counts
    pos_of = (jnp.sum(onehot * tile_start[None, :], axis=1) * tm + rank).astype(jnp.int32)
    _, tok_sorted = lax.sort((pos_of, jnp.arange(na, dtype=jnp.int32) // 2), num_keys=1)
    tid = jnp.arange(n_tiles, dtype=jnp.int32)
    e_raw = jnp.sum((tid[:, None] >= tile_end[None, :]).astype(jnp.int32), axis=1)
    tile_e = jnp.minimum(jnp.sum((jnp.minimum(tid, n_used - 1)[:, None] >= tile_end[None, :]).astype(jnp.int32),
                                 axis=1), N_EXPERTS - 1)
    oh_t = (jnp.minimum(e_raw, N_EXPERTS - 1)[:, None] == jnp.arange(N_EXPERTS)[None, :]).astype(jnp.int32)
    t_start = jnp.sum(oh_t * tile_start[None, :], axis=1)
    t_count = jnp.where(tid < n_used, jnp.sum(oh_t * counts[None, :], axis=1), 0)
    t_cstart = jnp.sum(oh_t * cstart[None, :], axis=1)
    local = (tid - t_start)[:, None] * tm + jnp.arange(tm, dtype=jnp.int32)[None, :]
    valid = (local < t_count[:, None]).reshape(p)
    src = jnp.clip(t_cstart[:, None] + local, 0, na - 1).reshape(p)
    src_tok = jnp.where(valid, tok_sorted[src], 0)
    return tile_e.astype(jnp.int32), n_used.reshape(1).astype(jnp.int32), src_tok, pos_of


def _moe(hp, route, w1, w3, w2, li):
    nb, rows, half = hp.shape
    t = nb * rows
    route2d = route.reshape(t, LANE)
    tile_e, n_used, src_tok, pos_of = _moe_plan(route2d, MOE_TM)
    xs = _moe_dispatch(hp.reshape(t, half), n_used, src_tok)
    y = _moe_ffn(xs, tile_e, n_used, w1, w3, w2, li)
    return _moe_combine(y, pos_of, route2d).reshape(nb, rows, 2 * half)


def _rope_tables(seq):
    quarter = RET_DK // 4
    inv_freq = ROPE_BASE ** (-jnp.arange(quarter, dtype=F32) / quarter)
    t = jnp.arange(seq, dtype=jnp.int32)
    ang_r = (t // GRID_W).astype(F32)[:, None] * inv_freq[None]
    ang_c = (t % GRID_W).astype(F32)[:, None] * inv_freq[None]
    ang = jnp.concatenate([ang_r, ang_r, ang_c, ang_c], axis=1)
    rot = np.zeros((RET_DK, RET_DK), np.float32)
    for j in range(RET_DK):
        if (j % (2 * quarter)) < quarter:
            rot[j + quarter, j] = -1.0
        else:
            rot[j - quarter, j] = 1.0
    return jnp.cos(ang), jnp.sin(ang), jnp.asarray(rot, BF16)


PACK_TN = 512


def _pack_kernel(a_ref, b_ref, o_ref):
    blk = pl.program_id(1)
    main_blks = W_IN_SPLIT // PACK_TN
    sh = 2 * GLA_RANK

    @pl.when(blk < main_blks)
    def _():
        o_ref[...] = a_ref[...].T.astype(BF16)

    @pl.when(blk == main_blks)
    def _():
        row = lax.broadcasted_iota(jnp.int32, a_ref.shape, 0)
        o_ref[...] = jnp.where(row < sh, a_ref[...], 0.0).T.astype(BF16)

    @pl.when(blk > main_blks)
    def _():
        o_ref[...] = jnp.concatenate([a_ref[sh:, :], b_ref[:sh, :]], axis=0).T.astype(BF16)


def _pack_w_in(w):
    depth, d, n_src = w.shape
    wt = jnp.swapaxes(w, 1, 2)
    main_blks = W_IN_SPLIT // PACK_TN
    a_idx = lambda j: jnp.where(j <= main_blks, j, j - 1)
    last_b = (n_src - 1) // LANE
    return pl.pallas_call(
        _pack_kernel,
        grid=(depth, N_PROJ // PACK_TN),
        in_specs=[pl.BlockSpec((None, PACK_TN, d), lambda l, j: (l, a_idx(j), 0)),
                  pl.BlockSpec((None, LANE, d),
                               lambda l, j: (l, jnp.minimum((a_idx(j) + 1) * (PACK_TN // LANE), last_b), 0))],
        out_specs=pl.BlockSpec((None, d, PACK_TN), lambda l, j: (l, 0, j)),
        out_shape=jax.ShapeDtypeStruct((depth, d, N_PROJ), BF16),
        compiler_params=_cparams(("arbitrary", "arbitrary")),
        name="pack_w_in",
    )(wt, wt)


def kernel(x, c, ctx, c_ctx, w_ada, b_ada, norms, w_in, ret_decay, gla_w2, gla_b2, na_rpb, w_up, w_out,
           ffn_w1, ffn_w3, ffn_w2, moe_router, moe_router_b, moe_w1, moe_w3, moe_w2):
    nb, seq, d = x.shape
    n_ctx = ctx.shape[1]
    l = n_ctx + seq
    nblk = l // ROW_BLK
    assert n_ctx == ROW_BLK and seq % ROW_BLK == 0 and seq // ROW_BLK >= NA_SLAB and d == D_MODEL and nb < 16

    cvec = jnp.zeros((16, d), F32).at[:nb].set(c).at[nb].set(c_ctx)
    mod = _adaln(cvec, w_ada, b_ada)
    cos, sin, rot = _rope_tables(seq)
    cg, sg, dft = _dft_tables(n_ctx, seq)
    log_gamma = jnp.log1p(-jnp.exp(ret_decay.astype(F32)))

    w_in_b = _pack_w_in(w_in)
    w_up_b, w_out_b = w_up.astype(BF16), w_out.astype(BF16)

    xs = jnp.concatenate([ctx, x], axis=1)
    _, h, _ = _resid(xs, None, None, None, mod[0], norms[0], None, x_off=0, f_off=0, nblk=nblk,
                     ctx_first=True, ib=0, sh_k=0, sc_k=1)
    for li in range(DEPTH):
        last = li == DEPTH - 1
        is_moe = li % 2 == 1
        h2d = h.reshape(nb * l, d)
        n_mix = C_GATE * LANE
        proj = _matmul(h2d, w_in_b, li, BF16, n_mix).reshape(nb, l, n_mix)
        w2p = jnp.zeros((2, LANE, GLA_HEADS * GLA_DK), F32)
        w2p = w2p.at[0, :GLA_RANK].set(gla_w2[li, 0]).at[1, GLA_RANK:2 * GLA_RANK].set(gla_w2[li, 1]).astype(BF16)
        y_ret = _retention(proj, log_gamma[li], cos, sin, rot)
        y_na = _na(proj, _na_bias(na_rpb[li], seq))
        y_fn = _fourier(proj, cg, sg, dft)
        y_gla = _gla(proj, w2p, gla_b2[li].reshape(2, 1, -1).astype(F32))
        ys = [y.reshape(nb * l, BRANCH_W) for y in (y_ret, y_na, y_fn, y_gla)]
        merged = _merge(h2d, ys, w_up_b, w_in_b, li).reshape(nb, l, d)
        out_proj = (w_out_b, li)

        router = None
        if is_moe:
            wr = jnp.zeros((d, LANE), F32).at[:, :N_EXPERTS].set(moe_router[li // 2])
            wr_hi = wr.astype(BF16)
            wr_lo = (wr - wr_hi.astype(F32)).astype(BF16)
            br = jnp.zeros((1, LANE), F32).at[0, :N_EXPERTS].set(moe_router_b[li // 2])
            router = (jnp.stack([wr_hi, wr_lo]), br)
        if last:
            xs, h2, route = _resid(xs, merged, mod[li], norms[li], mod[li], norms[li], router, x_off=1, f_off=1,
                                   nblk=nblk - 1, ctx_first=False, ia=1, gate_k=2, ib=2, sh_k=3, sc_k=4,
                                   proj=out_proj)
        else:
            xs, h2, route = _resid(xs, merged, mod[li], norms[li], mod[li], norms[li], router, x_off=0, f_off=0,
                                   nblk=nblk, ctx_first=True, ia=1, gate_k=2, ib=2, sh_k=3, sc_k=4,
                                   proj=out_proj)
        rows = h2.shape[1]
        if is_moe:
            f = _moe(h2, route, moe_w1, moe_w3, moe_w2, li // 2)
        else:
            f = _ffn(h2.reshape(nb * rows, d), ffn_w1, ffn_w3, ffn_w2, li // 2).reshape(nb, rows, d)
        if last:
            xs, _, _ = _resid(xs, f, mod[li], norms[li], None, None, None, x_off=0, f_off=0, nblk=nblk - 1,
                              ctx_first=False, ia=3, gate_k=5)
        else:
            xs, h, _ = _resid(xs, f, mod[li], norms[li], mod[li + 1], norms[li + 1], None, x_off=0, f_off=0,
                              nblk=nblk, ctx_first=True, ia=3, gate_k=5, ib=0, sh_k=0, sc_k=1)
    return xs
```

```python
import functools
import math

import numpy as np
import jax
import jax.numpy as jnp
from jax import lax
from jax.experimental import pallas as pl
from jax.experimental.pallas import tpu as pltpu

F32 = jnp.float32
BF16 = jnp.bfloat16

D_MODEL = 2048
DEPTH = 4
GRID_W = 64
CTX_LEN = 256
EPS = 1e-6
ROPE_BASE = 10000.0
RET_HEADS, RET_DK = 4, 128
NA_HEADS, NA_DH = 4, 128
NA_WIN_R, NA_WIN_C = 8, 16
FNET_GROUPS, FNET_GW = 4, 128
GLA_HEADS, GLA_DK, GLA_DV, GLA_RANK, GLA_TAU = 4, 64, 128, 16, 16.0
GLA_CHUNK = 64
BRANCH_W = 512
N_EXPERTS = 8

LANE = 128
ROW_BLK = 256
VMEM_LIMIT = 56 * 1024 * 1024

C_RET_Q, C_RET_K, C_RET_V, C_RET_G = 0, 4, 8, 12
C_NA_Q, C_NA_K, C_NA_V = 16, 20, 24
C_FU = 28
C_GLA_Q, C_GLA_K, C_GLA_V, C_GLA_G, C_GLA_LR = 32, 34, 36, 40, 44
C_GATE = 48
N_PROJ = (C_GATE + 4 * D_MODEL // LANE) * LANE
W_IN_SPLIT = 5632

NEG = -1e30


def _cparams(sem, row_dma=False):
    return pltpu.CompilerParams(dimension_semantics=sem, vmem_limit_bytes=VMEM_LIMIT,
                                disable_bounds_checks=row_dma)


def _silu(x):
    return x * (1.0 / (1.0 + jnp.exp(-x)))


def _sigmoid(x):
    return 0.5 * jnp.tanh(0.5 * x) + 0.5


def _rms(x):
    return x * lax.rsqrt(jnp.mean(x * x, axis=-1, keepdims=True) + EPS)


HI_MASK = 0xFFFF0000


def _pack_halves(x):
    n = x.shape[1] // 2
    lo = pltpu.bitcast(x[:, :n].astype(BF16).astype(F32), jnp.uint32)
    hi = pltpu.bitcast(x[:, n:].astype(BF16).astype(F32), jnp.uint32)
    return (lo >> 16) | (hi & jnp.uint32(HI_MASK))


def _unpack_halves(p):
    return pltpu.bitcast(p << 16, F32), pltpu.bitcast(p & jnp.uint32(HI_MASK), F32)


def _ada_kernel(c_ref, w_ref, b_ref, o_ref):
    s = _silu(c_ref[...]).astype(BF16)
    o_ref[...] = jnp.dot(s, w_ref[...].astype(BF16), preferred_element_type=F32) + b_ref[...]


def _adaln(cvec, w_ada, b_ada):
    depth, d, n = w_ada.shape
    tn = 1024
    return pl.pallas_call(
        _ada_kernel,
        grid=(depth, n // tn),
        in_specs=[pl.BlockSpec((16, d), lambda l, j: (0, 0)),
                  pl.BlockSpec((None, d, tn), lambda l, j: (l, 0, j)),
                  pl.BlockSpec((None, 1, tn), lambda l, j: (l, 0, j))],
        out_specs=pl.BlockSpec((None, 16, tn), lambda l, j: (l, 0, j)),
        out_shape=jax.ShapeDtypeStruct((depth, 16, n), F32),
        compiler_params=_cparams(("arbitrary", "arbitrary")),
        name="adaln",
    )(cvec, w_ada, b_ada.reshape(depth, 1, n))


def _resid_kernel(*refs, nb, ctx_first, has_f, has_proj, has_next, has_router, ia, gate_k, ib, sh_k, sc_k):
    it = iter(refs)
    x_ref = next(it)
    f_ref = next(it) if has_f else None
    wf_ref = next(it) if has_proj else None
    moda_ref = next(it) if has_f else None
    na_ref = next(it) if has_f else None
    modb_ref = next(it) if has_next else None
    nbn_ref = next(it) if has_next else None
    wr_ref = next(it) if has_router else None
    br_ref = next(it) if has_router else None
    xo_ref = next(it) if has_f else None
    h_ref = next(it) if has_next else None
    rt_ref = next(it) if has_router else None

    b = pl.program_id(0)
    j = pl.program_id(1)
    row = jnp.where(j == 0, nb, b) if ctx_first else b
    d = x_ref.shape[-1]

    def modv(ref, k):
        return ref[pl.ds(row, 1), k * d:(k + 1) * d]

    x = x_ref[...]
    if has_f:
        if has_proj:
            f = jnp.dot(f_ref[...], wf_ref[...], preferred_element_type=F32)
        else:
            f = f_ref[...].astype(F32)
        x = x + modv(moda_ref, gate_k) * (_rms(f) * na_ref[ia:ia + 1, :])
        xo_ref[...] = x
    if has_next:
        h = _rms(x) * nbn_ref[ib:ib + 1, :]
        h = h * (1.0 + modv(modb_ref, sc_k)) + modv(modb_ref, sh_k)
        if not has_router:
            h_ref[...] = h.astype(h_ref.dtype)
        if has_router:
            h_ref[...] = _pack_halves(h)
            h_hi = h.astype(BF16)
            h_lo = (h - h_hi.astype(F32)).astype(BF16)
            w_hi, w_lo = wr_ref[0], wr_ref[1]
            logits = (jnp.dot(h_hi, w_hi, preferred_element_type=F32)
                      + jnp.dot(h_lo, w_hi, preferred_element_type=F32)
                      + jnp.dot(h_hi, w_lo, preferred_element_type=F32)) + br_ref[...]
            lane = lax.broadcasted_iota(jnp.int32, logits.shape, 1)
            lg = jnp.where(lane < N_EXPERTS, logits, -jnp.inf)
            v1 = jnp.max(lg, axis=-1, keepdims=True)
            i1 = jnp.min(jnp.where(lg == v1, lane, LANE), axis=-1, keepdims=True)
            lg2 = jnp.where(lane == i1, -jnp.inf, lg)
            v2 = jnp.max(lg2, axis=-1, keepdims=True)
            i2 = jnp.min(jnp.where(lg2 == v2, lane, LANE), axis=-1, keepdims=True)
            e2 = jnp.exp(v2 - v1)
            w1 = 1.0 / (1.0 + e2)
            w2 = e2 / (1.0 + e2)
            zero = jnp.zeros_like(logits)
            rt = (jnp.where(lane == 8, i1.astype(F32), zero) + jnp.where(lane == 9, i2.astype(F32), zero)
                  + jnp.where(lane == 10, w1, zero) + jnp.where(lane == 11, w2, zero))
            rt_ref[...] = rt


def _resid(x, f, mod_a, norms_a, mod_b, norms_b, router, *, x_off, f_off, nblk, ctx_first,
           ia=0, gate_k=0, ib=0, sh_k=0, sc_k=0, proj=None):
    nb, _, d = x.shape
    has_f = f is not None
    has_proj = proj is not None
    has_next = mod_b is not None
    has_router = router is not None
    rows = nblk * ROW_BLK
    blk = lambda off: pl.BlockSpec((None, ROW_BLK, d), lambda b, j: (b, j + off, 0))
    full = lambda a: pl.BlockSpec(a.shape, lambda b, j: (0,) * a.ndim)
    ins, specs = [x], [blk(x_off)]
    if has_f:
        ins += [f]
        specs += [blk(f_off)]
        if has_proj:
            w, li = proj
            ins += [w]
            specs += [pl.BlockSpec((None, d, d), lambda b, j: (li, 0, 0))]
        ins += [mod_a, norms_a]
        specs += [full(mod_a), full(norms_a)]
    if has_next:
        ins += [mod_b, norms_b]
        specs += [full(mod_b), full(norms_b)]
    if has_router:
        ins += list(router)
        specs += [full(router[0]), full(router[1])]
    outs, ospecs = [], []
    if has_f:
        outs.append(jax.ShapeDtypeStruct((nb, rows, d), F32))
        ospecs.append(blk(0))
    if has_next and not has_router:
        outs.append(jax.ShapeDtypeStruct((nb, rows, d), BF16))
        ospecs.append(blk(0))
    if has_next and has_router:
        outs.append(jax.ShapeDtypeStruct((nb, rows, d // 2), jnp.uint32))
        ospecs.append(pl.BlockSpec((None, ROW_BLK, d // 2), lambda b, j: (b, j, 0)))
    if has_router:
        outs.append(jax.ShapeDtypeStruct((nb, rows, LANE), F32))
        ospecs.append(pl.BlockSpec((None, ROW_BLK, LANE), lambda b, j: (b, j, 0)))
    res = pl.pallas_call(
        functools.partial(_resid_kernel, nb=nb, ctx_first=ctx_first, has_f=has_f, has_proj=has_proj, has_next=has_next,
                          has_router=has_router, ia=ia, gate_k=gate_k, ib=ib, sh_k=sh_k, sc_k=sc_k),
        grid=(nb, nblk),
        in_specs=specs, out_specs=ospecs, out_shape=outs,
        compiler_params=_cparams(("arbitrary", "arbitrary")),
        name="resid_norm",
    )(*ins)
    res = list(res)
    x_new = res.pop(0) if has_f else None
    h = res.pop(0) if has_next else None
    rt = res.pop(0) if has_router else None
    return x_new, h, rt


def _mm_kernel(x_ref, w_ref, o_ref):
    o_ref[...] = jnp.dot(x_ref[...], w_ref[...], preferred_element_type=F32).astype(o_ref.dtype)


def _matmul(x, w, li, out_dtype, n, tm=1024, tn=1024):
    m, k = x.shape
    return pl.pallas_call(
        _mm_kernel,
        grid=(m // tm, n // tn),
        in_specs=[pl.BlockSpec((tm, k), lambda i, j: (i, 0)),
                  pl.BlockSpec((None, k, tn), lambda i, j: (li, 0, j))],
        out_specs=pl.BlockSpec((tm, tn), lambda i, j: (i, j)),
        out_shape=jax.ShapeDtypeStruct((m, n), out_dtype),
        compiler_params=_cparams(("arbitrary", "arbitrary")),
        name="matmul",
    )(x, w)


def _ret_kernel(lg_ref, q_ref, k_ref, v_ref, g_ref, cos_ref, sin_ref, rot_ref, o_ref, q_s, k_s, acc_s):
    h = pl.program_id(1)
    lgf = lg_ref[0, h]
    lgb = lg_ref[1, h]
    c = ROW_BLK
    n = q_ref.shape[0] // c
    scale = RET_DK ** -0.5

    q_s[0:c, :] = (q_ref[0:c, :].astype(F32) * scale).astype(BF16)
    k_s[0:c, :] = k_ref[0:c, :]
    rot = rot_ref[...]
    for i in range(1, n):
        rows = slice(i * c, (i + 1) * c)
        trow = slice((i - 1) * c, i * c)
        cs, sn = cos_ref[trow, :], sin_ref[trow, :]
        qa, ka = q_ref[rows, :], k_ref[rows, :]
        qr = jnp.dot(qa, rot, preferred_element_type=F32)
        kr = jnp.dot(ka, rot, preferred_element_type=F32)
        q_s[rows, :] = ((qa.astype(F32) * cs + qr * sn) * scale).astype(BF16)
        k_s[rows, :] = (ka.astype(F32) * cs + kr * sn).astype(BF16)

    ti = lax.broadcasted_iota(jnp.int32, (c, c), 0)
    si = lax.broadcasted_iota(jnp.int32, (c, c), 1)
    dd = (ti - si).astype(F32)
    dmat = jnp.where(dd > 0, jnp.exp(lgf * dd), jnp.where(dd < 0, jnp.exp(-lgb * dd), 2.0))
    tcol = lax.broadcasted_iota(jnp.int32, (c, 1), 0).astype(F32)
    q_f = jnp.exp(lgf * (tcol + 1.0))
    q_b = jnp.exp(lgb * (c - tcol))
    k_f = jnp.exp(lgf * (c - 1.0 - tcol))
    k_b = jnp.exp(lgb * tcol)
    g_f = jnp.exp(lgf * c)
    g_b = jnp.exp(lgb * c)

    def kv_state(kc, vc, kdec):
        kd = (kc.astype(F32) * kdec).astype(BF16)
        return lax.dot_general(kd, vc, (((0,), (0,)), ((), ())), preferred_element_type=F32)

    s_f = jnp.zeros((RET_DK, q_ref.shape[1]), F32)
    for i in range(n):
        rows = slice(i * c, (i + 1) * c)
        qc, kc, vc = q_s[rows, :], k_s[rows, :], v_ref[rows, :]
        sc = lax.dot_general(qc, kc, (((1,), (1,)), ((), ())), preferred_element_type=F32)
        o = jnp.dot((sc * dmat).astype(BF16), vc, preferred_element_type=F32)
        if i > 0:
            qd = (qc.astype(F32) * q_f).astype(BF16)
            o = o + jnp.dot(qd, s_f.astype(BF16), preferred_element_type=F32)
        acc_s[rows, :] = o
        if i < n - 1:
            s_f = g_f * s_f + kv_state(kc, vc, k_f)

    s_b = kv_state(k_s[0:c, :], v_ref[0:c, :], k_b)
    for i in range(n - 1, 0, -1):
        rows = slice(i * c, (i + 1) * c)
        qc, kc, vc = q_s[rows, :], k_s[rows, :], v_ref[rows, :]
        qd = (qc.astype(F32) * q_b).astype(BF16)
        acc_s[rows, :] = acc_s[rows, :] + jnp.dot(qd, s_b.astype(BF16), preferred_element_type=F32)
        if i > 1:
            s_b = g_b * s_b + kv_state(kc, vc, k_b)

    for i in range(n):
        rows = slice(i * c, (i + 1) * c)
        o_ref[rows, :] = (_rms(acc_s[rows, :]) * _silu(g_ref[rows, :].astype(F32))).astype(o_ref.dtype)


def _retention(proj, lg, cos, sin, rot):
    nb, l, _ = proj.shape
    col = lambda c0: pl.BlockSpec((None, l, LANE), lambda b, h: (b, 0, c0 + h))
    const = lambda a: pl.BlockSpec(a.shape, lambda b, h: (0,) * a.ndim)
    return pl.pallas_call(
        _ret_kernel,
        grid=(nb, RET_HEADS),
        in_specs=[pl.BlockSpec(memory_space=pltpu.SMEM),
                  col(C_RET_Q), col(C_RET_K), col(C_RET_V), col(C_RET_G),
                  const(cos), const(sin), const(rot)],
        out_specs=pl.BlockSpec((None, l, LANE), lambda b, h: (b, 0, h)),
        out_shape=jax.ShapeDtypeStruct((nb, l, BRANCH_W), BF16),
        scratch_shapes=[pltpu.VMEM((l, LANE), BF16), pltpu.VMEM((l, LANE), BF16), pltpu.VMEM((l, LANE), F32)],
        compiler_params=_cparams(("arbitrary", "arbitrary")),
        name="retention",
    )(lg, proj, proj, proj, proj, cos, sin, rot)


def _gla_kernel(q_ref, k_ref, v_ref, g_ref, lr_ref, w2_ref, b2_ref, o_ref,
                accf_s, accb_s, st_s, qd_s, ki_s, ke_s, dl_s, *, n_ctx_chunks):
    c = GLA_CHUNK
    l = q_ref.shape[0]
    n = l // c
    cpb = ROW_BLK // c
    kw = GLA_HEADS * GLA_DK
    scale = GLA_DK ** -0.5

    ti = lax.broadcasted_iota(jnp.int32, (ROW_BLK, ROW_BLK), 0)
    si = lax.broadcasted_iota(jnp.int32, (ROW_BLK, ROW_BLK), 1)
    same = ((ti // c) == (si // c)).astype(F32)
    cum_lo = (same * (ti >= si).astype(F32)).astype(BF16)
    cum_up = (same * (ti <= si).astype(F32)).astype(BF16)

    def prep(bi, carry):
        rows = pl.ds(pl.multiple_of(bi * ROW_BLK, ROW_BLK), ROW_BLK)
        lr = lr_ref[rows, :]
        qf = q_ref[rows, :].astype(F32) * scale
        kf = k_ref[rows, :].astype(F32)
        for d, cum, last in ((0, cum_lo, c - 1), (1, cum_up, 0)):
            z = jnp.dot(lr, w2_ref[d], preferred_element_type=F32) + b2_ref[d]
            la = (jnp.minimum(z, 0.0) - jnp.log(1.0 + jnp.exp(-jnp.abs(z)))) * (1.0 / GLA_TAU)
            la_hi = la.astype(BF16)
            la_lo = (la - la_hi.astype(F32)).astype(BF16)
            bcum = (jnp.dot(cum, la_hi, preferred_element_type=F32)
                    + jnp.dot(cum, la_lo, preferred_element_type=F32))
            bl = jnp.concatenate([jnp.broadcast_to(bcum[cc * c + last:cc * c + last + 1, :], (c, kw))
                                  for cc in range(cpb)], axis=0)
            qd_s[d, rows, :] = (qf * jnp.exp(bcum)).astype(BF16)
            ki_s[d, rows, :] = (kf * jnp.exp(-bcum)).astype(BF16)
            ke_s[d, rows, :] = (kf * jnp.exp(bl - bcum)).astype(BF16)
            dec = jnp.exp(bl)
            for cc in range(cpb):
                dl_s[d, pl.ds(pl.multiple_of((bi * cpb + cc) * 8, 8), 8), :] = dec[cc * c:cc * c + 8, :]
        return carry

    lax.fori_loop(0, l // ROW_BLK, prep, 0)

    t4 = lax.broadcasted_iota(jnp.int32, (GLA_HEADS * c, c), 0) % c
    s4 = lax.broadcasted_iota(jnp.int32, (GLA_HEADS * c, c), 1)
    mask_lo = (t4 >= s4).astype(F32)
    mask_up = (t4 <= s4).astype(F32)
    srow = lax.broadcasted_iota(jnp.int32, (GLA_HEADS * c, kw), 0) // c
    slane = lax.broadcasted_iota(jnp.int32, (GLA_HEADS * c, kw), 1) // GLA_DK
    stack_mask = srow == slane
    lane_head = lax.broadcasted_iota(jnp.int32, (GLA_DV, kw), 1) // GLA_DK
    lanes = (((1,), (1,)), ((), ()))
    st_s[...] = jnp.zeros_like(st_s)

    def chunk(ci, d, mask, acc_ref):
        rows = pl.ds(pl.multiple_of(ci * c, c), c)
        q_dec, k_inv, k_end = qd_s[d, rows, :], ki_s[d, rows, :], ke_s[d, rows, :]
        vc = v_ref[rows, :]
        dec = dl_s[d, pl.ds(pl.multiple_of(ci * 8, 8), 1), :]
        q_stack = jnp.where(stack_mask, jnp.concatenate([q_dec] * GLA_HEADS, axis=0), jnp.zeros((), BF16))
        sc = (lax.dot_general(q_stack, k_inv, lanes, preferred_element_type=F32) * mask).astype(BF16)
        st = st_s[d]
        inter = lax.dot_general(q_stack, st.astype(BF16), lanes, preferred_element_type=F32)
        acc_ref[rows, :] = jnp.concatenate(
            [jnp.dot(sc[h * c:(h + 1) * c, :], vc[:, h * GLA_DV:(h + 1) * GLA_DV], preferred_element_type=F32)
             + inter[h * c:(h + 1) * c, :] for h in range(GLA_HEADS)], axis=1)
        full = lax.dot_general(vc, k_end, (((0,), (0,)), ((), ())), preferred_element_type=F32)
        comp = full[0:GLA_DV, :]
        for h in range(1, GLA_HEADS):
            comp = jnp.where(lane_head == h, full[h * GLA_DV:(h + 1) * GLA_DV, :], comp)
        st_s[d] = st * dec + comp

    def step(i, carry):
        chunk(i, 0, mask_lo, accf_s)
        chunk(jnp.where(i < n_ctx_chunks, n_ctx_chunks - 1 - i, n + n_ctx_chunks - 1 - i), 1, mask_up, accb_s)
        return carry

    lax.fori_loop(0, n, step, 0, unroll=4)

    for i in range(l // ROW_BLK):
        rows = slice(i * ROW_BLK, (i + 1) * ROW_BLK)
        o = accf_s[rows, :] + accb_s[rows, :]
        o = jnp.concatenate([_rms(o[:, h * GLA_DV:(h + 1) * GLA_DV]) for h in range(GLA_HEADS)], axis=1)
        o_ref[rows, :] = (o * _silu(g_ref[rows, :].astype(F32))).astype(o_ref.dtype)


def _gla(proj, w2p, b2):
    nb, l, _ = proj.shape
    kw, vw = GLA_HEADS * GLA_DK, GLA_HEADS * GLA_DV
    col = lambda c0, w: pl.BlockSpec((None, l, w), lambda b: (b, 0, c0 * LANE // w))
    const = lambda a: pl.BlockSpec(a.shape, lambda b: (0,) * a.ndim)
    return pl.pallas_call(
        functools.partial(_gla_kernel, n_ctx_chunks=CTX_LEN // GLA_CHUNK),
        grid=(nb,),
        in_specs=[col(C_GLA_Q, kw), col(C_GLA_K, kw), col(C_GLA_V, vw), col(C_GLA_G, vw), col(C_GLA_LR, LANE),
                  const(w2p), const(b2)],
        out_specs=pl.BlockSpec((None, l, vw), lambda b: (b, 0, 0)),
        out_shape=jax.ShapeDtypeStruct((nb, l, BRANCH_W), BF16),
        scratch_shapes=[pltpu.VMEM((l, vw), F32), pltpu.VMEM((l, vw), F32),
                        pltpu.VMEM((2, GLA_DV, kw), F32),
                        pltpu.VMEM((2, l, kw), BF16), pltpu.VMEM((2, l, kw), BF16), pltpu.VMEM((2, l, kw), BF16),
                        pltpu.VMEM((2, l // GLA_CHUNK * 8, kw), F32)],
        compiler_params=_cparams(("arbitrary",)),
        name="gla",
    )(proj, proj, proj, proj, proj, w2p, b2)


NA_QROWS = ROW_BLK // GRID_W
NA_SLAB = 3


def _na_kernel(q_ref, kc_ref, k0_ref, k1_ref, k2_ref, vc_ref, v0_ref, v1_ref, v2_ref, bias_ref, o_ref):
    scale = NA_DH ** -0.5
    dims = (((1,), (1,)), ((), ()))
    for h in range(NA_HEADS):
        cols = slice(h * NA_DH, (h + 1) * NA_DH)
        q = q_ref[:, cols]
        s = [lax.dot_general(q, kc_ref[:, cols], dims, preferred_element_type=F32) * scale]
        for j, kr in enumerate((k0_ref, k1_ref, k2_ref)):
            sj = lax.dot_general(q, kr[:, cols], dims, preferred_element_type=F32) * scale
            s.append(sj + bias_ref[h, :, j * ROW_BLK:(j + 1) * ROW_BLK])
        m = functools.reduce(jnp.maximum, [jnp.max(x, axis=-1, keepdims=True) for x in s])
        p = [jnp.exp(x - m) for x in s]
        den = functools.reduce(lambda a, b: a + b, [jnp.sum(x, axis=-1, keepdims=True) for x in p])
        o = jnp.zeros((q.shape[0], NA_DH), F32)
        for pj, vr in zip(p, (vc_ref, v0_ref, v1_ref, v2_ref)):
            o = o + jnp.dot(pj.astype(BF16), vr[:, cols], preferred_element_type=F32)
        o_ref[:, cols] = (o / den).astype(o_ref.dtype)


def _na_slab_start(qb, n_lat_blk):
    return jnp.clip(qb - 2, 0, n_lat_blk - NA_SLAB)


def _na(proj, bias):
    nb, l, _ = proj.shape
    nblk = l // ROW_BLK
    nlat = nblk - 1
    w = NA_HEADS * NA_DH
    blk = lambda c0, rowfn: pl.BlockSpec((None, ROW_BLK, w), lambda qb, b: (b, rowfn(qb), c0 * LANE // w))
    slab = lambda j: (lambda qb: 1 + _na_slab_start(qb, nlat) + j)
    return pl.pallas_call(
        _na_kernel,
        grid=(nblk, nb),
        in_specs=[blk(C_NA_Q, lambda qb: qb),
                  blk(C_NA_K, lambda qb: 0), blk(C_NA_K, slab(0)), blk(C_NA_K, slab(1)), blk(C_NA_K, slab(2)),
                  blk(C_NA_V, lambda qb: 0), blk(C_NA_V, slab(0)), blk(C_NA_V, slab(1)), blk(C_NA_V, slab(2)),
                  pl.BlockSpec((NA_HEADS, None, ROW_BLK, NA_SLAB * ROW_BLK), lambda qb, b: (0, qb, 0, 0))],
        out_specs=pl.BlockSpec((None, ROW_BLK, w), lambda qb, b: (b, qb, 0)),
        out_shape=jax.ShapeDtypeStruct((nb, l, BRANCH_W), BF16),
        compiler_params=_cparams(("arbitrary", "arbitrary")),
        name="neighbourhood_attention",
    )(*([proj] * 9), bias)


def _na_bias(rpb, seq):
    rows = seq // GRID_W
    kr = min(NA_WIN_R, rows)
    nlat = seq // ROW_BLK
    cq = np.arange(GRID_W)[:, None]
    ck = np.arange(GRID_W)[None, :]
    win_start = np.clip(cq - NA_WIN_C // 2, 0, GRID_W - NA_WIN_C)
    in_win = (ck >= win_start) & (ck < win_start + NA_WIN_C)
    rel_c = np.clip(ck - cq, 1 - NA_WIN_C, NA_WIN_C - 1) + NA_WIN_C - 1
    tiles = jnp.where(in_win[None, None], rpb[:, :, rel_c].astype(F32), NEG)
    masked = jnp.full((rpb.shape[0], GRID_W, GRID_W), NEG, F32)
    blocks = [jnp.full((rpb.shape[0], ROW_BLK, NA_SLAB * ROW_BLK), NEG, F32)]
    for qb in range(nlat):
        s0 = int(np.clip(qb - 1, 0, nlat - NA_SLAB)) * NA_QROWS
        qrows = []
        for qr in range(NA_QROWS):
            r = qb * NA_QROWS + qr
            k0 = int(np.clip(r - kr // 2, 0, rows - kr))
            assert s0 <= k0 and k0 + kr <= s0 + NA_SLAB * NA_QROWS
            krows = []
            for kk in range(NA_SLAB * NA_QROWS):
                krow = s0 + kk
                if k0 <= krow < k0 + kr:
                    krows.append(tiles[:, krow - r + NA_WIN_R - 1])
                else:
                    krows.append(masked)
            qrows.append(jnp.concatenate(krows, axis=2))
        blocks.append(jnp.concatenate(qrows, axis=1))
    return jnp.stack(blocks, axis=1)


def _fourier_kernel(u_ref, cg_ref, sg_ref, dft_ref, o_ref, ab_s):
    l = u_ref.shape[0]

    @pl.when(pl.program_id(1) == 0)
    def _():
        for g in range(FNET_GROUPS):
            cols = slice(g * FNET_GW, (g + 1) * FNET_GW)
            u = u_ref[:, cols]
            ab_s[0:l, cols] = jnp.dot(u, cg_ref[...], preferred_element_type=F32).astype(BF16)
            ab_s[l:2 * l, cols] = jnp.dot(u, sg_ref[...], preferred_element_type=F32).astype(BF16)

    o_ref[...] = jnp.dot(dft_ref[...], ab_s[...], preferred_element_type=F32).astype(o_ref.dtype)


def _fourier(proj, cg, sg, dft):
    nb, l, _ = proj.shape
    tr = 3 * ROW_BLK
    w = FNET_GROUPS * FNET_GW
    return pl.pallas_call(
        _fourier_kernel,
        grid=(nb, l // tr),
        in_specs=[pl.BlockSpec((None, l, w), lambda b, i: (b, 0, C_FU * LANE // w)),
                  pl.BlockSpec(cg.shape, lambda b, i: (0, 0)),
                  pl.BlockSpec(sg.shape, lambda b, i: (0, 0)),
                  pl.BlockSpec((tr, 2 * l), lambda b, i: (i, 0))],
        out_specs=pl.BlockSpec((None, tr, w), lambda b, i: (b, i, 0)),
        out_shape=jax.ShapeDtypeStruct((nb, l, BRANCH_W), BF16),
        scratch_shapes=[pltpu.VMEM((2 * l, w), BF16)],
        compiler_params=_cparams(("arbitrary", "arbitrary")),
        name="fourier_mix",
    )(proj, cg, sg, dft)


def _dft_tables(n_ctx, seq):
    def cs(n):
        jk = (np.arange(n)[:, None] * np.arange(n)[None, :]) % n
        ang = 2.0 * np.pi * jk / n
        return np.cos(ang) / np.sqrt(n), np.sin(ang) / np.sqrt(n)

    cg, sg = cs(FNET_GW)
    l = n_ctx + seq
    cl = np.zeros((l, l))
    sl = np.zeros((l, l))
    cc, sc = cs(n_ctx)
    cs_, ss_ = cs(seq)
    cl[:n_ctx, :n_ctx], sl[:n_ctx, :n_ctx] = cc, sc
    cl[n_ctx:, n_ctx:], sl[n_ctx:, n_ctx:] = cs_, ss_
    dft = np.concatenate([cl, -sl], axis=1)
    return (jnp.asarray(cg, BF16), jnp.asarray(sg, BF16), jnp.asarray(dft, BF16))


def _merge_kernel(h_ref, y0_ref, y1_ref, y2_ref, y3_ref, w_ref, g0_ref, g1_ref, g2_ref, g3_ref, o_ref):
    h = h_ref[...]
    acc = None
    for i, (y_ref, g_ref) in enumerate(zip((y0_ref, y1_ref, y2_ref, y3_ref), (g0_ref, g1_ref, g2_ref, g3_ref))):
        gate = _sigmoid(jnp.dot(h, g_ref[...], preferred_element_type=F32))
        t = gate * jnp.dot(y_ref[...], w_ref[i], preferred_element_type=F32)
        acc = t if acc is None else acc + t
    o_ref[...] = acc.astype(o_ref.dtype)


def _merge(h2d, ys, w_up, w_in_b, li, tm=1024, tn=512):
    t, k = h2d.shape
    d = w_up.shape[3]
    gate0 = C_GATE * LANE // tn
    yspec = pl.BlockSpec((tm, BRANCH_W), lambda i, j: (i, 0))
    gspec = lambda br: pl.BlockSpec((None, k, tn), lambda i, j: (li, 0, gate0 + br * (d // tn) + j))
    return pl.pallas_call(
        _merge_kernel,
        grid=(t // tm, d // tn),
        in_specs=[pl.BlockSpec((tm, k), lambda i, j: (i, 0))] + [yspec] * 4
        + [pl.BlockSpec((None, 4, BRANCH_W, tn), lambda i, j: (li, 0, 0, j))] + [gspec(br) for br in range(4)],
        out_specs=pl.BlockSpec((tm, tn), lambda i, j: (i, j)),
        out_shape=jax.ShapeDtypeStruct((t, d), BF16),
        compiler_params=_cparams(("arbitrary", "arbitrary")),
        name="gated_merge",
    )(h2d, *ys, w_up, w_in_b, w_in_b, w_in_b, w_in_b)


def _ffn_up_kernel(x_ref, w1_ref, w3_ref, o_ref, w1b_s, w3b_s):
    @pl.when(pl.program_id(1) == 0)
    def _():
        w1b_s[...] = w1_ref[...].astype(BF16)
        w3b_s[...] = w3_ref[...].astype(BF16)

    x = x_ref[...]
    a = jnp.dot(x, w1b_s[...], preferred_element_type=F32)
    b = jnp.dot(x, w3b_s[...], preferred_element_type=F32)
    o_ref[...] = (_silu(a) * b).astype(o_ref.dtype)


def _ffn_down_kernel(g_ref, w2_ref, o_ref, w2b_s):
    @pl.when(pl.program_id(1) == 0)
    def _():
        w2b_s[...] = w2_ref[...].astype(BF16)

    o_ref[...] = jnp.dot(g_ref[...], w2b_s[...], preferred_element_type=F32).astype(o_ref.dtype)


def _ffn(x, w1, w3, w2, li, tm=1024, tf=512, tm2=512, tn=512):
    t, d = x.shape
    ff = w1.shape[2]
    g = pl.pallas_call(
        _ffn_up_kernel,
        grid=(ff // tf, t // tm),
        in_specs=[pl.BlockSpec((tm, d), lambda j, i: (i, 0)),
                  pl.BlockSpec((None, d, tf), lambda j, i: (li, 0, j)),
                  pl.BlockSpec((None, d, tf), lambda j, i: (li, 0, j))],
        out_specs=pl.BlockSpec((tm, tf), lambda j, i: (i, j)),
        out_shape=jax.ShapeDtypeStruct((t, ff), BF16),
        scratch_shapes=[pltpu.VMEM((d, tf), BF16), pltpu.VMEM((d, tf), BF16)],
        compiler_params=_cparams(("arbitrary", "arbitrary")),
        name="swiglu_up",
    )(x, w1, w3)
    return pl.pallas_call(
        _ffn_down_kernel,
        grid=(d // tn, t // tm2),
        in_specs=[pl.BlockSpec((tm2, ff), lambda j, i: (i, 0)),
                  pl.BlockSpec((None, ff, tn), lambda j, i: (li, 0, j))],
        out_specs=pl.BlockSpec((tm2, tn), lambda j, i: (i, j)),
        out_shape=jax.ShapeDtypeStruct((t, d), BF16),
        scratch_shapes=[pltpu.VMEM((ff, tn), BF16)],
        compiler_params=_cparams(("arbitrary", "arbitrary")),
        name="swiglu_down",
    )(g, w2)


MOE_TM = 512
MOE_TF = 1024
MOE_TN = 512


def _dispatch_kernel(nu_ref, tok_ref, h_hbm, o_ref, xg_s, sem):
    i = pl.program_id(0)
    tm = xg_s.shape[1]
    nu = nu_ref[0]
    used = i < nu

    def start_tile(t, slot):
        def issue(r, carry):
            tok = tok_ref[t * tm + r]
            pltpu.make_async_copy(h_hbm.at[pl.ds(tok, 1), :], xg_s.at[slot, pl.ds(r, 1), :], sem.at[slot]).start()
            return carry

        lax.fori_loop(0, tm, issue, 0, unroll=8)

    @pl.when(jnp.logical_and(i == 0, used))
    def _():
        start_tile(0, 0)

    @pl.when(jnp.logical_and(i + 1 < pl.num_programs(0), i + 1 < nu))
    def _():
        start_tile(i + 1, (i + 1) % 2)

    @pl.when(used)
    def _():
        slot = i % 2
        pltpu.make_async_copy(h_hbm.at[pl.ds(0, tm), :], xg_s.at[slot], sem.at[slot]).wait()
        lo, hi = _unpack_halves(xg_s[slot])
        half = xg_s.shape[2]
        o_ref[:, :half] = lo.astype(o_ref.dtype)
        o_ref[:, half:] = hi.astype(o_ref.dtype)

    @pl.when(jnp.logical_not(used))
    def _():
        o_ref[...] = jnp.zeros_like(o_ref)


def _moe_dispatch(hp, n_used, src_tok):
    p = src_tok.shape[0]
    half = hp.shape[1]
    d = 2 * half
    tm = MOE_TM
    grid_spec = pltpu.PrefetchScalarGridSpec(
        num_scalar_prefetch=2,
        grid=(p // tm,),
        in_specs=[pl.BlockSpec(memory_space=pl.ANY)],
        out_specs=pl.BlockSpec((tm, d), lambda i, nu, tok: (i, 0)),
        scratch_shapes=[pltpu.VMEM((2, tm, half), jnp.uint32), pltpu.SemaphoreType.DMA((2,))],
    )
    return pl.pallas_call(
        _dispatch_kernel,
        grid_spec=grid_spec,
        out_shape=jax.ShapeDtypeStruct((p, d), BF16),
        compiler_params=_cparams(("arbitrary",), row_dma=True),
        name="moe_dispatch",
    )(n_used, src_tok, hp)


def _expert_changed(te_ref, i):
    return jnp.logical_or(i == 0, te_ref[i] != te_ref[jnp.maximum(i - 1, 0)])


def _moe_up_kernel(te_ref, nu_ref, x_ref, w1_ref, w3_ref, o_ref, w1b_s, w3b_s):
    i = pl.program_id(1)

    @pl.when(_expert_changed(te_ref, i))
    def _():
        w1b_s[...] = w1_ref[...].astype(BF16)
        w3b_s[...] = w3_ref[...].astype(BF16)

    @pl.when(i < nu_ref[0])
    def _():
        x = x_ref[...]
        a = jnp.dot(x, w1b_s[...], preferred_element_type=F32)
        b = jnp.dot(x, w3b_s[...], preferred_element_type=F32)
        o_ref[...] = (_silu(a) * b).astype(o_ref.dtype)

    @pl.when(i >= nu_ref[0])
    def _():
        o_ref[...] = jnp.zeros_like(o_ref)


def _moe_down_kernel(te_ref, nu_ref, g_ref, w2a_ref, w2b_ref, o_ref, w2a_s, w2b_s):
    i = pl.program_id(1)

    @pl.when(_expert_changed(te_ref, i))
    def _():
        w2a_s[...] = w2a_ref[...].astype(BF16)
        w2b_s[...] = w2b_ref[...].astype(BF16)

    @pl.when(i < nu_ref[0])
    def _():
        g = g_ref[...]
        ya = jnp.dot(g, w2a_s[...], preferred_element_type=F32)
        yb = jnp.dot(g, w2b_s[...], preferred_element_type=F32)
        o_ref[...] = _pack_halves(jnp.concatenate([ya, yb], axis=1))

    @pl.when(i >= nu_ref[0])
    def _():
        o_ref[...] = jnp.zeros_like(o_ref)


def _moe_ffn(xs, tile_e, n_used, w1, w3, w2, li):
    p, d = xs.shape
    ff = w1.shape[3]
    tm, tf, tn = MOE_TM, MOE_TF, MOE_TN
    up_spec = pltpu.PrefetchScalarGridSpec(
        num_scalar_prefetch=2,
        grid=(ff // tf, p // tm),
        in_specs=[pl.BlockSpec((tm, d), lambda j, i, te, nu: (i, 0)),
                  pl.BlockSpec((None, None, d, tf), lambda j, i, te, nu: (li, te[i], 0, j)),
                  pl.BlockSpec((None, None, d, tf), lambda j, i, te, nu: (li, te[i], 0, j))],
        out_specs=pl.BlockSpec((tm, tf), lambda j, i, te, nu: (i, j)),
        scratch_shapes=[pltpu.VMEM((d, tf), BF16), pltpu.VMEM((d, tf), BF16)],
    )
    g = pl.pallas_call(
        _moe_up_kernel,
        grid_spec=up_spec,
        out_shape=jax.ShapeDtypeStruct((p, ff), BF16),
        compiler_params=_cparams(("arbitrary", "arbitrary")),
        name="moe_up",
    )(tile_e, n_used, xs, w1, w3)
    half_blks = d // 2 // tn
    down_spec = pltpu.PrefetchScalarGridSpec(
        num_scalar_prefetch=2,
        grid=(half_blks, p // tm),
        in_specs=[pl.BlockSpec((tm, ff), lambda j, i, te, nu: (i, 0)),
                  pl.BlockSpec((None, None, ff, tn), lambda j, i, te, nu: (li, te[i], 0, j)),
                  pl.BlockSpec((None, None, ff, tn), lambda j, i, te, nu: (li, te[i], 0, half_blks + j))],
        out_specs=pl.BlockSpec((tm, tn), lambda j, i, te, nu: (i, j)),
        scratch_shapes=[pltpu.VMEM((ff, tn), BF16), pltpu.VMEM((ff, tn), BF16)],
    )
    return pl.pallas_call(
        _moe_down_kernel,
        grid_spec=down_spec,
        out_shape=jax.ShapeDtypeStruct((p, d // 2), jnp.uint32),
        compiler_params=_cparams(("arbitrary", "arbitrary")),
        name="moe_down",
    )(tile_e, n_used, g, w2, w2)


def _combine_kernel(pos_ref, y_hbm, rt_ref, o_ref, g0_s, g1_s, sem):
    i = pl.program_id(0)
    tc = g0_s.shape[1]

    def start_block(blk, slot):
        def issue(r, carry):
            t = blk * tc + r
            pltpu.make_async_copy(y_hbm.at[pl.ds(pos_ref[2 * t], 1), :], g0_s.at[slot, pl.ds(r, 1), :],
                                  sem.at[0, slot]).start()
            pltpu.make_async_copy(y_hbm.at[pl.ds(pos_ref[2 * t + 1], 1), :], g1_s.at[slot, pl.ds(r, 1), :],
                                  sem.at[1, slot]).start()
            return carry

        lax.fori_loop(0, tc, issue, 0, unroll=8)

    @pl.when(i == 0)
    def _():
        start_block(0, 0)

    @pl.when(i + 1 < pl.num_programs(0))
    def _():
        start_block(i + 1, (i + 1) % 2)

    slot = i % 2
    pltpu.make_async_copy(y_hbm.at[pl.ds(0, tc), :], g0_s.at[slot], sem.at[0, slot]).wait()
    pltpu.make_async_copy(y_hbm.at[pl.ds(0, tc), :], g1_s.at[slot], sem.at[1, slot]).wait()
    w0 = rt_ref[:, 10:11]
    w1 = rt_ref[:, 11:12]
    lo0, hi0 = _unpack_halves(g0_s[slot])
    lo1, hi1 = _unpack_halves(g1_s[slot])
    half = g0_s.shape[2]
    o_ref[:, :half] = (w0 * lo0 + w1 * lo1).astype(o_ref.dtype)
    o_ref[:, half:] = (w0 * hi0 + w1 * hi1).astype(o_ref.dtype)


def _moe_combine(y, pos, route2d):
    t = route2d.shape[0]
    half = y.shape[1]
    d = 2 * half
    tc = ROW_BLK
    grid_spec = pltpu.PrefetchScalarGridSpec(
        num_scalar_prefetch=1,
        grid=(t // tc,),
        in_specs=[pl.BlockSpec(memory_space=pl.ANY),
                  pl.BlockSpec((tc, LANE), lambda i, pos: (i, 0))],
        out_specs=pl.BlockSpec((tc, d), lambda i, pos: (i, 0)),
        scratch_shapes=[pltpu.VMEM((2, tc, half), jnp.uint32), pltpu.VMEM((2, tc, half), jnp.uint32),
                        pltpu.SemaphoreType.DMA((2, 2))],
    )
    return pl.pallas_call(
        _combine_kernel,
        grid_spec=grid_spec,
        out_shape=jax.ShapeDtypeStruct((t, d), BF16),
        compiler_params=_cparams(("arbitrary",), row_dma=True),
        name="moe_combine",
    )(pos, y, route2d)


def _moe_plan(route, tm):
    t = route.shape[0]
    e_flat = route[:, 8:10].astype(jnp.int32).reshape(-1)
    na = 2 * t
    n_tiles = -(-(na + N_EXPERTS * (tm - 1)) // tm)
    p = n_tiles * tm
    onehot = (e_flat[:, None] == jnp.arange(N_EXPERTS)[None, :]).astype(jnp.int32)
    csum = jnp.cumsum(onehot, axis=0)
    counts = csum[-1]
    rank = jnp.sum((csum - onehot) * onehot, axis=1)
    tiles_e = (counts + tm - 1) // tm
    tile_end = jnp.cumsum(tiles_e)
    tile_start = tile_end - tiles_e
    n_used = tile_end[-1]
    cstart = jnp.cumsum(counts) - counts
    pos_of = (jnp.sum(onehot * tile_start[None, :], axis=1) * tm + rank).astype(jnp.int32)
    _, tok_sorted = lax.sort((pos_of, jnp.arange(na, dtype=jnp.int32) // 2), num_keys=1)
    tid = jnp.arange(n_tiles, dtype=jnp.int32)
    e_raw = jnp.sum((tid[:, None] >= tile_end[None, :]).astype(jnp.int32), axis=1)
    tile_e = jnp.minimum(jnp.sum((jnp.minimum(tid, n_used - 1)[:, None] >= tile_end[None, :]).astype(jnp.int32),
                                 axis=1), N_EXPERTS - 1)
    oh_t = (jnp.minimum(e_raw, N_EXPERTS - 1)[:, None] == jnp.arange(N_EXPERTS)[None, :]).astype(jnp.int32)
    t_start = jnp.sum(oh_t * tile_start[None, :], axis=1)
    t_count = jnp.where(tid < n_used, jnp.sum(oh_t * counts[None, :], axis=1), 0)
    t_cstart = jnp.sum(oh_t * cstart[None, :], axis=1)
    local = (tid - t_start)[:, None] * tm + jnp.arange(tm, dtype=jnp.int32)[None, :]
    valid = (local < t_count[:, None]).reshape(p)
    src = jnp.clip(t_cstart[:, None] + local, 0, na - 1).reshape(p)
    src_tok = jnp.where(valid, tok_sorted[src], 0)
    return tile_e.astype(jnp.int32), n_used.reshape(1).astype(jnp.int32), src_tok, pos_of


def _moe(hp, route, w1, w3, w2, li):
    nb, rows, half = hp.shape
    t = nb * rows
    route2d = route.reshape(t, LANE)
    tile_e, n_used, src_tok, pos_of = _moe_plan(route2d, MOE_TM)
    xs = _moe_dispatch(hp.reshape(t, half), n_used, src_tok)
    y = _moe_ffn(xs, tile_e, n_used, w1, w3, w2, li)
    return _moe_combine(y, pos_of, route2d).reshape(nb, rows, 2 * half)


def _rope_tables(seq):
    quarter = RET_DK // 4
    inv_freq = ROPE_BASE ** (-jnp.arange(quarter, dtype=F32) / quarter)
    t = jnp.arange(seq, dtype=jnp.int32)
    ang_r = (t // GRID_W).astype(F32)[:, None] * inv_freq[None]
    ang_c = (t % GRID_W).astype(F32)[:, None] * inv_freq[None]
    ang = jnp.concatenate([ang_r, ang_r, ang_c, ang_c], axis=1)
    rot = np.zeros((RET_DK, RET_DK), np.float32)
    for j in range(RET_DK):
        if (j % (2 * quarter)) < quarter:
            rot[j + quarter, j] = -1.0
        else:
            rot[j - quarter, j] = 1.0
    return jnp.cos(ang), jnp.sin(ang), jnp.asarray(rot, BF16)


PACK_TN = 512


def _pack_kernel(a_ref, b_ref, o_ref):
    blk = pl.program_id(1)
    main_blks = W_IN_SPLIT // PACK_TN
    sh = 2 * GLA_RANK

    @pl.when(blk < main_blks)
    def _():
        o_ref[...] = a_ref[...].T.astype(BF16)

    @pl.when(blk == main_blks)
    def _():
        row = lax.broadcasted_iota(jnp.int32, a_ref.shape, 0)
        o_ref[...] = jnp.where(row < sh, a_ref[...], 0.0).T.astype(BF16)

    @pl.when(blk > main_blks)
    def _():
        o_ref[...] = jnp.concatenate([a_ref[sh:, :], b_ref[:sh, :]], axis=0).T.astype(BF16)


def _pack_w_in(w):
    depth, d, n_src = w.shape
    wt = jnp.swapaxes(w, 1, 2)
    main_blks = W_IN_SPLIT // PACK_TN
    a_idx = lambda j: jnp.where(j <= main_blks, j, j - 1)
    last_b = (n_src - 1) // LANE
    return pl.pallas_call(
        _pack_kernel,
        grid=(depth, N_PROJ // PACK_TN),
        in_specs=[pl.BlockSpec((None, PACK_TN, d), lambda l, j: (l, a_idx(j), 0)),
                  pl.BlockSpec((None, LANE, d),
                               lambda l, j: (l, jnp.minimum((a_idx(j) + 1) * (PACK_TN // LANE), last_b), 0))],
        out_specs=pl.BlockSpec((None, d, PACK_TN), lambda l, j: (l, 0, j)),
        out_shape=jax.ShapeDtypeStruct((depth, d, N_PROJ), BF16),
        compiler_params=_cparams(("arbitrary", "arbitrary")),
        name="pack_w_in",
    )(wt, wt)


def kernel(x, c, ctx, c_ctx, w_ada, b_ada, norms, w_in, ret_decay, gla_w2, gla_b2, na_rpb, w_up, w_out,
           ffn_w1, ffn_w3, ffn_w2, moe_router, moe_router_b, moe_w1, moe_w3, moe_w2):
    nb, seq, d = x.shape
    n_ctx = ctx.shape[1]
    l = n_ctx + seq
    nblk = l // ROW_BLK
    assert n_ctx == ROW_BLK and seq % ROW_BLK == 0 and seq // ROW_BLK >= NA_SLAB and d == D_MODEL and nb < 16

    cvec = jnp.zeros((16, d), F32).at[:nb].set(c).at[nb].set(c_ctx)
    mod = _adaln(cvec, w_ada, b_ada)
    cos, sin, rot = _rope_tables(seq)
    cg, sg, dft = _dft_tables(n_ctx, seq)
    log_gamma = jnp.log1p(-jnp.exp(ret_decay.astype(F32)))

    w_in_b = _pack_w_in(w_in)
    w_up_b, w_out_b = w_up.astype(BF16), w_out.astype(BF16)

    xs = jnp.concatenate([ctx, x], axis=1)
    _, h, _ = _resid(xs, None, None, None, mod[0], norms[0], None, x_off=0, f_off=0, nblk=nblk,
                     ctx_first=True, ib=0, sh_k=0, sc_k=1)
    for li in range(DEPTH):
        last = li == DEPTH - 1
        is_moe = li % 2 == 1
        h2d = h.reshape(nb * l, d)
        n_mix = C_GATE * LANE
        proj = _matmul(h2d, w_in_b, li, BF16, n_mix).reshape(nb, l, n_mix)
        w2p = jnp.zeros((2, LANE, GLA_HEADS * GLA_DK), F32)
        w2p = w2p.at[0, :GLA_RANK].set(gla_w2[li, 0]).at[1, GLA_RANK:2 * GLA_RANK].set(gla_w2[li, 1]).astype(BF16)
        y_ret = _retention(proj, log_gamma[li], cos, sin, rot)
        y_na = _na(proj, _na_bias(na_rpb[li], seq))
        y_fn = _fourier(proj, cg, sg, dft)
        y_gla = _gla(proj, w2p, gla_b2[li].reshape(2, 1, -1).astype(F32))
        ys = [y.reshape(nb * l, BRANCH_W) for y in (y_ret, y_na, y_fn, y_gla)]
        merged = _merge(h2d, ys, w_up_b, w_in_b, li).reshape(nb, l, d)
        out_proj = (w_out_b, li)

        router = None
        if is_moe:
            wr = jnp.zeros((d, LANE), F32).at[:, :N_EXPERTS].set(moe_router[li // 2])
            wr_hi = wr.astype(BF16)
            wr_lo = (wr - wr_hi.astype(F32)).astype(BF16)
            br = jnp.zeros((1, LANE), F32).at[0, :N_EXPERTS].set(moe_router_b[li // 2])
            router = (jnp.stack([wr_hi, wr_lo]), br)
        if last:
            xs, h2, route = _resid(xs, merged, mod[li], norms[li], mod[li], norms[li], router, x_off=1, f_off=1,
                                   nblk=nblk - 1, ctx_first=False, ia=1, gate_k=2, ib=2, sh_k=3, sc_k=4,
                                   proj=out_proj)
        else:
            xs, h2, route = _resid(xs, merged, mod[li], norms[li], mod[li], norms[li], router, x_off=0, f_off=0,
                                   nblk=nblk, ctx_first=True, ia=1, gate_k=2, ib=2, sh_k=3, sc_k=4,
                                   proj=out_proj)
        rows = h2.shape[1]
        if is_moe:
            f = _moe(h2, route, moe_w1, moe_w3, moe_w2, li // 2)
        else:
            f = _ffn(h2.reshape(nb * rows, d), ffn_w1, ffn_w3, ffn_w2, li // 2).reshape(nb, rows, d)
        if last:
            xs, _, _ = _resid(xs, f, mod[li], norms[li], None, None, None, x_off=0, f_off=0, nblk=nblk - 1,
                              ctx_first=False, ia=3, gate_k=5)
        else:
            xs, h, _ = _resid(xs, f, mod[li], norms[li], mod[li + 1], norms[li + 1], None, x_off=0, f_off=0,
                              nblk=nblk, ctx_first=True, ia=3, gate_k=5, ib=0, sh_k=0, sc_k=1)
    return xs
```
